```python
import math
import jax, jax.numpy as jnp
from jax import lax
import numpy as np

D_MODEL = 1024
BATCH = 4
SEQ = 4096
DEPTH = 2

GRID_W = 64
N_EVEN = (DEPTH + 1) // 2
N_ODD = DEPTH // 2
D_FF = 4 * D_MODEL
NORM_EPS = 1e-6

SSD_HEAD_DIM = 64
SSD_WIDTH = D_MODEL
SSD_HEADS = SSD_WIDTH // SSD_HEAD_DIM
SSD_GROUPS = 2
SSD_STATE = 128
SSD_CONV = 5
SSD_CHUNK = 128
SSD_CONV_CH = SSD_WIDTH + 2 * SSD_GROUPS * SSD_STATE

NA_HEAD_DIM = 64
NA_WIDTH = D_MODEL
NA_HEADS = NA_WIDTH // NA_HEAD_DIM
NA_KH_MAX = 8
NA_KW = 16

AB_IN = SSD_WIDTH + SSD_CONV_CH + 2 * SSD_HEADS + 3 * NA_WIDTH
AB_MIX = SSD_WIDTH + NA_WIDTH

ML_HEADS = 8
ML_DV = 2 * D_MODEL // ML_HEADS
ML_DK = ML_DV // 2
ML_CHUNK = 128
ML_WIDTH = ML_HEADS * ML_DV
ML_QK = ML_HEADS * ML_DK
ML_IN = 2 * ML_QK + 2 * ML_WIDTH + 4 * ML_HEADS

kernel_name = "hybrid_ssd_natten_mlstm_encoder"


def rmsnorm(x, g):
    xf = x.astype(jnp.float32)
    y = xf * lax.rsqrt(jnp.mean(xf * xf, axis=-1, keepdims=True) + NORM_EPS)
    return (y * g.astype(jnp.float32)).astype(x.dtype)


def modulate(h, shift, scale):
    return h * (1 + scale[:, None, :]) + shift[:, None, :]


def centred_dwconv(u, w, b):
    pad = w.shape[0] // 2
    out = lax.conv_general_dilated(u, w[:, None, :].astype(u.dtype), window_strides=(1,),
                                   padding=[(pad, pad)], dimension_numbers=('NWC', 'WIO', 'NWC'),
                                   feature_group_count=u.shape[-1])
    return out + b


def segsum(a):
    T = a.shape[-1]
    cs = jnp.cumsum(a, axis=-1)
    diff = cs[..., :, None] - cs[..., None, :]
    mask = jnp.tril(jnp.ones((T, T), dtype=bool))
    return jnp.where(mask, diff, -jnp.inf)


def ssd_scan(xh, dt, a, bm, cm):
    Bsz, L, H, P = xh.shape
    G, N = bm.shape[-2:]
    R = H // G
    T = SSD_CHUNK
    nc = L // T
    X = (xh * dt[..., None]).reshape(Bsz, nc, T, G, R, P)
    adt = (dt * a).reshape(Bsz, nc, T, G, R).transpose(0, 3, 4, 1, 2)
    Bc = bm.reshape(Bsz, nc, T, G, N)
    Cc = cm.reshape(Bsz, nc, T, G, N)
    a_cs = jnp.cumsum(adt, axis=-1)
    Lmat = jnp.exp(segsum(adt))
    cb = jnp.einsum('bctgn,bcsgn->bgcts', Cc, Bc)
    y_diag = jnp.einsum('bgrcts,bcsgrp->bctgrp', cb[:, :, None] * Lmat, X)
    decay_states = jnp.exp(a_cs[..., -1:] - a_cs)
    states = jnp.einsum('bctgn,bgrct,bctgrp->bcgrpn', Bc, decay_states, X)
    states = jnp.concatenate([jnp.zeros_like(states[:, :1]), states], axis=1)
    chunk_decay = jnp.exp(segsum(jnp.pad(a_cs[..., -1], ((0, 0), (0, 0), (0, 0), (1, 0)))))
    new_states = jnp.einsum('bgrzc,bcgrpn->bzgrpn', chunk_decay, states)
    states = new_states[:, :-1]
    y_off = jnp.einsum('bctgn,bcgrpn,bgrct->bctgrp', Cc, states, jnp.exp(a_cs))
    return (y_diag + y_off).reshape(Bsz, L, H, P)


def ssd_branch(z, xbc, dt_raw, conv_w, conv_b, dt_bias, a_log, d_skip, norm_g):
    f32 = jnp.float32
    xbc = jax.nn.silu(centred_dwconv(xbc, conv_w, conv_b)).astype(f32)
    xs, bm, cm = jnp.split(xbc, [SSD_WIDTH, SSD_WIDTH + SSD_GROUPS * SSD_STATE], axis=-1)
    Bsz, L, _ = xs.shape
    xh = xs.reshape(Bsz, L, SSD_HEADS, SSD_HEAD_DIM)
    bm = bm.reshape(Bsz, L, SSD_GROUPS, SSD_STATE)
    cm = cm.reshape(Bsz, L, SSD_GROUPS, SSD_STATE)
    dt = jax.nn.softplus(dt_raw.astype(f32).reshape(Bsz, L, 2, SSD_HEADS) + dt_bias.astype(f32))
    a = -jnp.exp(a_log.astype(f32))
    y_f = ssd_scan(xh, dt[:, :, 0], a[0], bm, cm)
    fl = lambda t: jnp.flip(t, axis=1)
    y_b = fl(ssd_scan(fl(xh), fl(dt[:, :, 1]), a[1], fl(bm), fl(cm)))
    y = y_f + y_b + d_skip.astype(f32)[:, None] * xh
    y = y.reshape(Bsz, L, SSD_WIDTH)
    return rmsnorm(y * jax.nn.silu(z.astype(f32)), norm_g)


def na_branch(qkv, rpb):
    Bsz, L, _ = qkv.shape
    rows = L // GRID_W
    kh = min(NA_KH_MAX, rows)
    q, k, v = jnp.split(qkv.astype(jnp.float32), 3, axis=-1)
    grid = lambda t: t.reshape(Bsz, rows, GRID_W, NA_HEADS, NA_HEAD_DIM)
    q = grid(q) * (NA_HEAD_DIM ** -0.5)
    k = grid(k)
    v = grid(v)
    r_idx = np.arange(rows)
    row_start = np.clip(r_idx - kh // 2, 0, rows - kh)
    c_idx = np.arange(GRID_W)
    col_start = np.clip(c_idx - NA_KW // 2, 0, GRID_W - NA_KW)
    col_keys = col_start[:, None] + np.arange(NA_KW)[None, :]
    row_off = row_start[:, None] + np.arange(kh)[None, :] - r_idx[:, None] + NA_KH_MAX - 1
    col_off = col_keys - c_idx[:, None] + NA_KW - 1
    rpb = rpb.astype(jnp.float32)

    def one_row(args):
        q_r, rs, ro = args
        k_band = lax.dynamic_slice_in_dim(k, rs, kh, axis=1)
        v_band = lax.dynamic_slice_in_dim(v, rs, kh, axis=1)
        k_win = jnp.take(k_band, col_keys, axis=2)
        v_win = jnp.take(v_band, col_keys, axis=2)
        bias = rpb[:, ro[:, None, None], col_off[None, :, :]]
        s = jnp.einsum('bwhd,biwjhd->bhwij', q_r, k_win) + bias.transpose(0, 2, 1, 3)[None]
        p = jax.nn.softmax(s.reshape(Bsz, NA_HEADS, GRID_W, kh * NA_KW), axis=-1).reshape(s.shape)
        return jnp.einsum('bhwij,biwjhd->bwhd', p, v_win)

    out = lax.map(one_row, (q.transpose(1, 0, 2, 3, 4),
                            jnp.asarray(row_start, jnp.int32), jnp.asarray(row_off, jnp.int32)))
    return out.transpose(1, 0, 2, 3, 4).reshape(Bsz, L, NA_WIDTH)


def mlstm_chunkwise(q, k, v, i_pre, f_pre):
    Bsz, H, L, DK = q.shape
    DV = v.shape[-1]
    T = ML_CHUNK
    nc = L // T
    q = q.reshape(Bsz, H, nc, T, DK)
    k = k.reshape(Bsz, H, nc, T, DK)
    v = v.reshape(Bsz, H, nc, T, DV)
    ig = i_pre.reshape(Bsz, H, nc, T)
    b = jnp.cumsum(jax.nn.log_sigmoid(f_pre).reshape(Bsz, H, nc, T), axis=-1)
    g = b[..., -1]
    a = g[..., None] - b + ig
    m_loc = jnp.max(a, axis=-1)
    w = jnp.exp(a - m_loc[..., None])
    S_loc = jnp.einsum('bhct,bhctk,bhctv->bhckv', w, k, v)
    n_loc = jnp.einsum('bhct,bhctk->bhck', w, k)

    def step(carry, inp):
        C, n, m = carry
        S_c, n_c, m_c, g_c = inp
        m_new = jnp.maximum(g_c + m, m_c)
        s_old = jnp.exp(g_c + m - m_new)
        s_new = jnp.exp(m_c - m_new)
        C_new = s_old[..., None, None] * C + s_new[..., None, None] * S_c
        n_new = s_old[..., None] * n + s_new[..., None] * n_c
        return (C_new, n_new, m_new), (C, n, m)

    init = (jnp.zeros((Bsz, H, DK, DV), jnp.float32), jnp.zeros((Bsz, H, DK), jnp.float32),
            jnp.zeros((Bsz, H), jnp.float32))
    mv = lambda t: jnp.moveaxis(t, 2, 0)
    _, (C_prev, n_prev, m_prev) = lax.scan(step, init, (mv(S_loc), mv(n_loc), mv(m_loc), mv(g)))
    C_prev = jnp.moveaxis(C_prev, 0, 2)
    n_prev = jnp.moveaxis(n_prev, 0, 2)
    m_prev = jnp.moveaxis(m_prev, 0, 2)
    mask = jnp.tril(jnp.ones((T, T), dtype=bool))
    Dm = jnp.where(mask, b[..., :, None] - b[..., None, :] + ig[..., None, :], -jnp.inf)
    m_inter = b + m_prev[..., None]
    m_t = jnp.maximum(m_inter, jnp.max(Dm, axis=-1))
    sc = jnp.einsum('bhctk,bhcsk->bhcts', q, k) * jnp.exp(Dm - m_t[..., None])
    inter_scale = jnp.exp(m_inter - m_t)
    num = jnp.einsum('bhcts,bhcsv->bhctv', sc, v) + inter_scale[..., None] * jnp.einsum('bhctk,bhckv->bhctv', q, C_prev)
    den = jnp.sum(sc, axis=-1) + inter_scale * jnp.einsum('bhctk,bhck->bhct', q, n_prev)
    h = num / jnp.maximum(jnp.abs(den), jnp.exp(-m_t))[..., None]
    return h.reshape(Bsz, H, L, DV)


def mlstm_branch(proj, gate_b, head_g):
    Bsz, L, _ = proj.shape
    q, k, v, o, gates = jnp.split(proj.astype(jnp.float32),
                                  [ML_QK, 2 * ML_QK, 2 * ML_QK + ML_WIDTH, 2 * ML_QK + 2 * ML_WIDTH], axis=-1)
    heads = lambda t, d: t.reshape(Bsz, L, ML_HEADS, d).transpose(0, 2, 1, 3)
    q = heads(q, ML_DK) * (ML_DK ** -0.5)
    k = heads(k, ML_DK)
    v = heads(v, ML_DV)
    gates = (gates.reshape(Bsz, L, 4, ML_HEADS) + gate_b.astype(jnp.float32)).transpose(0, 2, 3, 1)
    h_f = mlstm_chunkwise(q, k, v, gates[:, 0], gates[:, 1])
    fl = lambda t: jnp.flip(t, axis=2)
    h_b = fl(mlstm_chunkwise(fl(q), fl(k), fl(v), fl(gates[:, 2]), fl(gates[:, 3])))
    h = (h_f + h_b).transpose(0, 2, 1, 3)
    h = rmsnorm(h, head_g.reshape(ML_HEADS, ML_DV))
    return jax.nn.sigmoid(o) * h.reshape(Bsz, L, ML_WIDTH)


def ssd_na_mixer(h, w_in, conv_w, conv_b, dt_bias, a_log, d_skip, ssd_norm, rpb, w_out):
    proj = h @ w_in
    s1 = SSD_WIDTH
    s2 = s1 + SSD_CONV_CH
    s3 = s2 + 2 * SSD_HEADS
    z, xbc, dt_raw, qkv = jnp.split(proj, [s1, s2, s3], axis=-1)
    y_ssd = ssd_branch(z, xbc, dt_raw, conv_w, conv_b, dt_bias, a_log, d_skip, ssd_norm)
    y_na = na_branch(qkv, rpb)
    y = jnp.concatenate([y_ssd.astype(jnp.float32), y_na], axis=-1)
    return y.astype(h.dtype) @ w_out


def mlstm_mixer(h, w_in, gate_b, head_g, w_out):
    y = mlstm_branch(h @ w_in, gate_b, head_g)
    return y.astype(h.dtype) @ w_out


def setup_inputs(seed: int = 0) -> dict:
    key = jax.random.key(seed)
    ks = jax.random.split(key, 20)
    nrm = lambda k, shape, s: jax.random.normal(k, shape, jnp.float32) * s
    x = nrm(ks[0], (BATCH, SEQ, D_MODEL), 1.0)
    c = nrm(ks[1], (BATCH, D_MODEL), 1.0)
    ada_w = nrm(ks[2], (DEPTH, D_MODEL, 6 * D_MODEL), 0.5 * D_MODEL ** -0.5)
    ada_b = nrm(ks[3], (DEPTH, 6 * D_MODEL), 0.02)
    norm_g = 1.0 + nrm(ks[4], (DEPTH, 4, D_MODEL), 0.02)
    mlp_w1 = nrm(ks[5], (DEPTH, D_MODEL, D_FF), D_MODEL ** -0.5)
    mlp_w2 = nrm(ks[6], (DEPTH, D_FF, D_MODEL), D_FF ** -0.5)
    ab_w_in = nrm(ks[7], (N_EVEN, D_MODEL, AB_IN), D_MODEL ** -0.5)
    ab_conv_w = nrm(ks[8], (N_EVEN, SSD_CONV, SSD_CONV_CH), SSD_CONV ** -0.5)
    ab_conv_b = nrm(ks[9], (N_EVEN, SSD_CONV_CH), 0.02)
    dt0 = jnp.exp(jax.random.uniform(ks[10], (N_EVEN, 2, SSD_HEADS), jnp.float32,
                                     minval=math.log(1e-3), maxval=math.log(1e-1)))
    ab_dt_bias = dt0 + jnp.log(-jnp.expm1(-dt0))
    ab_a_log = jnp.log(jax.random.uniform(ks[11], (N_EVEN, 2, SSD_HEADS), jnp.float32, minval=1.0, maxval=16.0))
    ab_d_skip = 1.0 + nrm(ks[12], (N_EVEN, SSD_HEADS), 0.02)
    ab_ssd_norm = 1.0 + nrm(ks[13], (N_EVEN, SSD_WIDTH), 0.02)
    ab_rpb = nrm(ks[14], (N_EVEN, NA_HEADS, 2 * NA_KH_MAX - 1, 2 * NA_KW - 1), 0.1)
    ab_w_out = nrm(ks[15], (N_EVEN, AB_MIX, D_MODEL), AB_MIX ** -0.5)
    ml_w_in = nrm(ks[16], (N_ODD, D_MODEL, ML_IN), D_MODEL ** -0.5)
    gate_base = jnp.array([0.0, 4.0, 0.0, 4.0], jnp.float32)[None, :, None]
    ml_gate_b = gate_base + nrm(ks[17], (N_ODD, 4, ML_HEADS), 0.1)
    ml_head_norm = 1.0 + nrm(ks[18], (N_ODD, ML_WIDTH), 0.02)
    ml_w_out = nrm(ks[19], (N_ODD, ML_WIDTH, D_MODEL), ML_WIDTH ** -0.5)
    return {"x": x, "c": c, "ada_w": ada_w, "ada_b": ada_b, "norm_g": norm_g,
            "mlp_w1": mlp_w1, "mlp_w2": mlp_w2,
            "ab_w_in": ab_w_in, "ab_conv_w": ab_conv_w, "ab_conv_b": ab_conv_b,
            "ab_dt_bias": ab_dt_bias, "ab_a_log": ab_a_log, "ab_d_skip": ab_d_skip,
            "ab_ssd_norm": ab_ssd_norm, "ab_rpb": ab_rpb, "ab_w_out": ab_w_out,
            "ml_w_in": ml_w_in, "ml_gate_b": ml_gate_b, "ml_head_norm": ml_head_norm, "ml_w_out": ml_w_out}


def reference(x, c, ada_w, ada_b, norm_g, mlp_w1, mlp_w2,
              ab_w_in, ab_conv_w, ab_conv_b, ab_dt_bias, ab_a_log, ab_d_skip, ab_ssd_norm, ab_rpb, ab_w_out,
              ml_w_in, ml_gate_b, ml_head_norm, ml_w_out):
    cond = jax.nn.silu(c)
    for layer in range(DEPTH):
        mod = cond @ ada_w[layer] + ada_b[layer]
        sh1, sc1, g1, sh2, sc2, g2 = jnp.split(mod, 6, axis=-1)
        h = modulate(rmsnorm(x, norm_g[layer, 0]), sh1, sc1)
        if layer % 2 == 0:
            j = layer // 2
            mixed = ssd_na_mixer(h, ab_w_in[j], ab_conv_w[j], ab_conv_b[j], ab_dt_bias[j], ab_a_log[j],
                                 ab_d_skip[j], ab_ssd_norm[j], ab_rpb[j], ab_w_out[j])
        else:
            j = layer // 2
            mixed = mlstm_mixer(h, ml_w_in[j], ml_gate_b[j], ml_head_norm[j], ml_w_out[j])
        x = x + g1[:, None, :] * rmsnorm(mixed, norm_g[layer, 1])
        h = modulate(rmsnorm(x, norm_g[layer, 2]), sh2, sc2)
        u = jnp.square(jax.nn.relu(h @ mlp_w1[layer])) @ mlp_w2[layer]
        x = x + g2[:, None, :] * rmsnorm(u, norm_g[layer, 3])
    return x
```

```python
import functools

import numpy as np
import jax
import jax.numpy as jnp
from jax import lax
from jax.experimental import pallas as pl
from jax.experimental.pallas import tpu as pltpu

F32 = jnp.float32
BF16 = jnp.bfloat16

NORM_EPS = 1e-6
GRID_W = 64

SSD_HEAD_DIM = 64
SSD_HEADS = 16
SSD_GROUPS = 2
SSD_STATE = 128
SSD_CONV = 5
SSD_CHUNK = 128
SSD_WIDTH = SSD_HEADS * SSD_HEAD_DIM
SSD_BC = 2 * SSD_GROUPS * SSD_STATE

NA_HEAD_DIM = 64
NA_HEADS = 16
NA_KH = 8
NA_KW = 16
NA_WIDTH = NA_HEADS * NA_HEAD_DIM

ML_HEADS = 8
ML_DV = 256
ML_DK = 128
ML_CHUNK = 128
ML_WIDTH = ML_HEADS * ML_DV
ML_QK = ML_HEADS * ML_DK

LANES = 128
HALO = 16
NEG_BIG = -1e30
VMEM_LIMIT = 56 * 1024 * 1024


def _cparams(sem):
    return pltpu.CompilerParams(dimension_semantics=sem, vmem_limit_bytes=VMEM_LIMIT)


def _dot(a, b):
    return jnp.dot(a, b, preferred_element_type=F32)


def _dot_nt(a, b):
    return lax.dot_general(a, b, (((1,), (1,)), ((), ())), preferred_element_type=F32)


def _split3(x):
    hi = x.astype(BF16)
    r1 = x - hi.astype(F32)
    mid = r1.astype(BF16)
    lo = (r1 - mid.astype(F32)).astype(BF16)
    return hi, mid, lo


def _dot_exact_rhs01(x, sel):
    hi, mid, lo = _split3(x)
    return _dot(hi, sel) + _dot(mid, sel) + _dot(lo, sel)


def _dot_exact_lhs01(sel, x):
    hi, mid, lo = _split3(x)
    return _dot(sel, hi) + _dot(sel, mid) + _dot(sel, lo)


def _iota(shape, dim):
    return lax.broadcasted_iota(jnp.int32, shape, dim)


def _sigmoid(x):
    return 1.0 / (1.0 + jnp.exp(-x))


def _softplus(x):
    return jnp.maximum(x, 0.0) + jnp.log(1.0 + jnp.exp(-jnp.abs(x)))


def _rms(x):
    return x * lax.rsqrt(jnp.mean(x * x, axis=-1, keepdims=True) + NORM_EPS)


def _mod_kernel(c_ref, w_ref, b_ref, o_ref):
    c = c_ref[...]
    cond = c * _sigmoid(c)
    o_ref[0] = jnp.dot(cond, w_ref[0], preferred_element_type=F32,
                       precision=lax.Precision.HIGHEST) + b_ref[0]


def _adaln(c, ada_w, ada_b):
    depth, d, d6 = ada_w.shape
    bsz = c.shape[0]
    tn = 1024
    return pl.pallas_call(
        _mod_kernel,
        grid=(depth, d6 // tn),
        in_specs=[pl.BlockSpec((bsz, d), lambda l, j: (0, 0)),
                  pl.BlockSpec((1, d, tn), lambda l, j: (l, 0, j)),
                  pl.BlockSpec((1, 1, tn), lambda l, j: (l, 0, j))],
        out_specs=pl.BlockSpec((1, bsz, tn), lambda l, j: (l, 0, j)),
        out_shape=jax.ShapeDtypeStruct((depth, bsz, d6), F32),
        compiler_params=_cparams(("arbitrary", "arbitrary")),
    )(c, ada_w, ada_b.reshape(depth, 1, d6))


def _proj_kernel(x_ref, g_ref, sh_ref, sc_ref, w_ref, ws_ref, *rest, transposed_small):
    if transposed_small:
        wst_ref, o_ref, os_ref, ost_ref, h_scr = rest
    else:
        o_ref, os_ref, h_scr = rest

    @pl.when(pl.program_id(1) == 0)
    def _():
        h = _rms(x_ref[...]) * g_ref[...] * (1.0 + sc_ref[0]) + sh_ref[0]
        hb = h.astype(BF16)
        h_scr[...] = hb
        os_ref[...] = _dot(hb, ws_ref[...])
        if transposed_small:
            ost_ref[...] = _dot_nt(wst_ref[...], hb)

    o_ref[...] = _dot(h_scr[...], w_ref[...]).astype(o_ref.dtype)


def _project(x2, g, shift, scale, w, ws, wst, seq, tm, tn):
    n, d = x2.shape
    wn = w.shape[1]
    tiles_per_batch = seq // tm
    bsz = shift.shape[0]
    in_specs = [pl.BlockSpec((tm, d), lambda i, j: (i, 0)),
                pl.BlockSpec((1, d), lambda i, j: (0, 0)),
                pl.BlockSpec((1, 1, d), lambda i, j: (i // tiles_per_batch, 0, 0)),
                pl.BlockSpec((1, 1, d), lambda i, j: (i // tiles_per_batch, 0, 0)),
                pl.BlockSpec((d, tn), lambda i, j: (0, j)),
                pl.BlockSpec((d, LANES), lambda i, j: (0, 0))]
    args = [x2, g.reshape(1, d), shift.reshape(bsz, 1, d), scale.reshape(bsz, 1, d), w, ws]
    out_specs = [pl.BlockSpec((tm, tn), lambda i, j: (i, j)),
                 pl.BlockSpec((tm, LANES), lambda i, j: (i, 0))]
    out_shape = [jax.ShapeDtypeStruct((n, wn), BF16), jax.ShapeDtypeStruct((n, LANES), F32)]
    if wst is not None:
        r = wst.shape[0]
        in_specs.append(pl.BlockSpec((r, d), lambda i, j: (0, 0)))
        args.append(wst)
        out_specs.append(pl.BlockSpec((r, tm), lambda i, j: (0, i)))
        out_shape.append(jax.ShapeDtypeStruct((r, n), F32))
    return pl.pallas_call(
        functools.partial(_proj_kernel, transposed_small=wst is not None),
        grid=(n // tm, wn // tn),
        in_specs=in_specs, out_specs=out_specs, out_shape=out_shape,
        scratch_shapes=[pltpu.VMEM((tm, d), BF16)],
        compiler_params=_cparams(("arbitrary", "arbitrary")),
    )(*args)


def _out_kernel(ya_ref, yb_ref, wa_ref, wb_ref, x_ref, gate_ref, g_ref, o_ref):
    acc = _dot(ya_ref[...], wa_ref[...]) + _dot(yb_ref[...], wb_ref[...])
    o_ref[...] = x_ref[...] + gate_ref[0] * (_rms(acc) * g_ref[...])


def _out_project(ya, ia, yb, ib, w, x2, gate, g, seq, tm):
    n, d = x2.shape
    kh = w.shape[0] // 2
    tiles_per_batch = seq // tm
    bsz = gate.shape[0]
    return pl.pallas_call(
        _out_kernel,
        grid=(n // tm,),
        in_specs=[pl.BlockSpec((tm, kh), lambda i: (i, ia)),
                  pl.BlockSpec((tm, kh), lambda i: (i, ib)),
                  pl.BlockSpec((kh, d), lambda i: (0, 0)),
                  pl.BlockSpec((kh, d), lambda i: (1, 0)),
                  pl.BlockSpec((tm, d), lambda i: (i, 0)),
                  pl.BlockSpec((1, 1, d), lambda i: (i // tiles_per_batch, 0, 0)),
                  pl.BlockSpec((1, d), lambda i: (0, 0))],
        out_specs=pl.BlockSpec((tm, d), lambda i: (i, 0)),
        out_shape=jax.ShapeDtypeStruct((n, d), F32),
        compiler_params=_cparams(("arbitrary",)),
    )(ya, yb, w, w, x2, gate.reshape(bsz, 1, d), g.reshape(1, d))


def _mlp_kernel(x_ref, g_ref, sh_ref, sc_ref, w1_ref, w2_ref, gate_ref, g2_ref, o_ref, h_scr, acc_scr):
    f = pl.program_id(1)

    @pl.when(f == 0)
    def _():
        h = _rms(x_ref[...]) * g_ref[...] * (1.0 + sc_ref[0]) + sh_ref[0]
        h_scr[...] = h.astype(BF16)
        acc_scr[...] = jnp.zeros_like(acc_scr)

    a = jnp.maximum(_dot(h_scr[...], w1_ref[...]), 0.0)
    acc_scr[...] += _dot((a * a).astype(BF16), w2_ref[...])

    @pl.when(f == pl.num_programs(1) - 1)
    def _():
        o_ref[...] = x_ref[...] + gate_ref[0] * (_rms(acc_scr[...]) * g2_ref[...])


def _mlp(x2, g, shift, scale, w1, w2, gate, g2, seq, tm, tf):
    n, d = x2.shape
    dff = w1.shape[1]
    tiles_per_batch = seq // tm
    bsz = gate.shape[0]
    vec = lambda a: a.reshape(bsz, 1, d)
    bspec = pl.BlockSpec((1, 1, d), lambda i, f: (i // tiles_per_batch, 0, 0))
    return pl.pallas_call(
        _mlp_kernel,
        grid=(n // tm, dff // tf),
        in_specs=[pl.BlockSpec((tm, d), lambda i, f: (i, 0)),
                  pl.BlockSpec((1, d), lambda i, f: (0, 0)),
                  bspec, bspec,
                  pl.BlockSpec((d, tf), lambda i, f: (0, f)),
                  pl.BlockSpec((tf, d), lambda i, f: (f, 0)),
                  bspec,
                  pl.BlockSpec((1, d), lambda i, f: (0, 0))],
        out_specs=pl.BlockSpec((tm, d), lambda i, f: (i, 0)),
        out_shape=jax.ShapeDtypeStruct((n, d), F32),
        scratch_shapes=[pltpu.VMEM((tm, d), BF16), pltpu.VMEM((tm, d), F32)],
        compiler_params=_cparams(("arbitrary", "arbitrary")),
    )(x2, g.reshape(1, d), vec(shift), vec(scale), w1, w2, vec(gate), g2.reshape(1, d))


def _conv_silu(main_ref, prev_ref, next_ref, w_ref, b_ref, scr, keep_prev, keep_next):
    t = main_ref.shape[0]
    scr[0:HALO, :] = prev_ref[...].astype(F32) * keep_prev
    scr[HALO:HALO + t, :] = main_ref[...].astype(F32)
    scr[HALO + t:2 * HALO + t, :] = next_ref[...].astype(F32) * keep_next
    pad = SSD_CONV // 2
    acc = b_ref[...] + w_ref[0:1, :] * scr[HALO - pad:HALO - pad + t, :]
    for k in range(1, SSD_CONV):
        acc = acc + w_ref[k:k + 1, :] * scr[HALO - pad + k:HALO - pad + k + t, :]
    return acc * _sigmoid(acc)


def _ssd_kernel(xs_ref, xsp_ref, xsn_ref, bc_ref, bcp_ref, bcn_ref, dt_ref,
                wx_ref, bx_ref, wbc_ref, bbc_ref, dtb_ref, alog_ref, *rest, reverse, nchunks):
    if reverse:
        z_ref, yf_ref, dskip_ref, ng_ref, o_ref, xs_scr, bc_scr, st_scr = rest
    else:
        o_ref, xs_scr, bc_scr, st_scr = rest
    t = SSD_CHUNK
    c = pl.program_id(1)
    chunk = nchunks - 1 - c if reverse else c

    @pl.when(c == 0)
    def _():
        st_scr[...] = jnp.zeros_like(st_scr)

    keep_prev = (chunk > 0).astype(F32)
    keep_next = (chunk < nchunks - 1).astype(F32)
    xs = _conv_silu(xs_ref, xsp_ref, xsn_ref, wx_ref, bx_ref, xs_scr, keep_prev, keep_next)
    bc = _conv_silu(bc_ref, bcp_ref, bcn_ref, wbc_ref, bbc_ref, bc_scr, keep_prev, keep_next)

    dt = _softplus(dt_ref[...] + dtb_ref[...])
    adt = dt * (-jnp.exp(alog_ref[...]))
    row = _iota((t, t), 0)
    col = _iota((t, t), 1)
    tri = (col >= row) if reverse else (col <= row)
    cs = _dot_exact_lhs01(tri.astype(BF16), adt)
    cs_t = cs.T
    dt_t = dt.T
    tot = cs[0:1, :] if reverse else cs[t - 1:t, :]

    base = SSD_HEADS if reverse else 0
    hd = SSD_HEAD_DIM
    e64 = (_iota((LANES, SSD_WIDTH), 0) == base + _iota((LANES, SSD_WIDTH), 1) // hd).astype(BF16)
    e128 = (_iota((LANES, SSD_HEADS * t), 0) == base + _iota((LANES, SSD_HEADS * t), 1) // t).astype(BF16)
    cs_x = _dot_exact_rhs01(cs, e64)
    tot_x = _dot_exact_rhs01(tot, e64)
    cs_b = _dot_exact_rhs01(cs, e128)
    decay_x = jnp.exp(tot_x - cs_x)
    xd = (xs * decay_x).astype(BF16)
    ecs_x = jnp.exp(cs_x)

    lane = _iota((t, LANES), 1)
    ys = []
    for g in range(SSD_GROUPS):
        bg = bc[:, g * SSD_STATE:(g + 1) * SSD_STATE]
        cg = bc[:, (SSD_GROUPS + g) * SSD_STATE:(SSD_GROUPS + g + 1) * SSD_STATE].astype(BF16)
        cb = _dot_nt(cg, bg.astype(BF16))
        bg_t = bg.T
        hpg = SSD_HEADS // SSD_GROUPS
        s_prev = st_scr[g]
        y_off = _dot(cg, s_prev.astype(BF16)) * ecs_x[:, g * hpg * hd:(g + 1) * hpg * hd]
        s_new = []
        for pr in range(hpg // 2):
            acc = None
            snew_pair = None
            for sub in range(2):
                h = g * hpg + 2 * pr + sub
                j = base + h
                dts = dt_t[j:j + 1, :]
                lmat = jnp.exp(jnp.where(tri, cs_b[:, h * t:(h + 1) * t] - cs_t[j:j + 1, :], NEG_BIG))
                m = (cb * lmat * dts).astype(BF16)
                xpair = xs[:, (h // 2) * LANES:(h // 2 + 1) * LANES]
                keep = (lane < hd) if sub == 0 else (lane >= hd)
                xh = jnp.where(keep, xpair, 0.0).astype(BF16)
                part = _dot(m, xh)
                acc = part if acc is None else acc + part
                xdh = jnp.where(keep, xd[:, (h // 2) * LANES:(h // 2 + 1) * LANES], jnp.zeros((), BF16))
                sp = _dot((bg_t * dts).astype(BF16), xdh)
                snew_pair = sp if snew_pair is None else snew_pair + sp
            ys.append(acc + y_off[:, pr * LANES:(pr + 1) * LANES])
            s_new.append(snew_pair)
        etot = jnp.exp(tot_x[:, g * hpg * hd:(g + 1) * hpg * hd])
        st_scr[g] = s_prev * etot + jnp.concatenate(s_new, axis=1)
    y = jnp.concatenate(ys, axis=1)

    if reverse:
        y = y + yf_ref[...] + dskip_ref[...] * xs
        z = z_ref[...].astype(F32)
        y = y * (z * _sigmoid(z))
        o_ref[...] = (_rms(y) * ng_ref[...]).astype(o_ref.dtype)
    else:
        o_ref[...] = y


def _ssd_pass(main, dtf, wx, bx, wbc, bbc, dtb, alog, bsz, seq, reverse, extra=None):
    n = main.shape[0]
    t = SSD_CHUNK
    nc = seq // t
    xs_blk = SSD_WIDTH // SSD_WIDTH
    bc_blk = (2 * SSD_WIDTH) // SSD_BC
    rpb = t // HALO
    last_halo = n // HALO - 1

    def ch(c):
        return nc - 1 - c if reverse else c

    def rows(b, c):
        return b * nc + ch(c)

    def prev(b, c):
        return jnp.maximum(rows(b, c) * rpb - 1, 0)

    def nxt(b, c):
        return jnp.minimum((rows(b, c) + 1) * rpb, last_halo)

    in_specs = [pl.BlockSpec((t, SSD_WIDTH), lambda b, c: (rows(b, c), xs_blk)),
                pl.BlockSpec((HALO, SSD_WIDTH), lambda b, c: (prev(b, c), xs_blk)),
                pl.BlockSpec((HALO, SSD_WIDTH), lambda b, c: (nxt(b, c), xs_blk)),
                pl.BlockSpec((t, SSD_BC), lambda b, c: (rows(b, c), bc_blk)),
                pl.BlockSpec((HALO, SSD_BC), lambda b, c: (prev(b, c), bc_blk)),
                pl.BlockSpec((HALO, SSD_BC), lambda b, c: (nxt(b, c), bc_blk)),
                pl.BlockSpec((t, LANES), lambda b, c: (rows(b, c), 0)),
                pl.BlockSpec((SSD_CONV, SSD_WIDTH), lambda b, c: (0, 0)),
                pl.BlockSpec((1, SSD_WIDTH), lambda b, c: (0, 0)),
                pl.BlockSpec((SSD_CONV, SSD_BC), lambda b, c: (0, 0)),
                pl.BlockSpec((1, SSD_BC), lambda b, c: (0, 0)),
                pl.BlockSpec((1, LANES), lambda b, c: (0, 0)),
                pl.BlockSpec((1, LANES), lambda b, c: (0, 0))]
    args = [main, main, main, main, main, main, dtf, wx, bx, wbc, bbc, dtb, alog]
    if reverse:
        yf, dskip, ng = extra
        in_specs += [pl.BlockSpec((t, SSD_WIDTH), lambda b, c: (rows(b, c), 0)),
                     pl.BlockSpec((t, SSD_WIDTH), lambda b, c: (rows(b, c), 0)),
                     pl.BlockSpec((1, SSD_WIDTH), lambda b, c: (0, 0)),
                     pl.BlockSpec((1, SSD_WIDTH), lambda b, c: (0, 0))]
        args += [main, yf, dskip, ng]
        out_dtype = BF16
    else:
        out_dtype = F32
    return pl.pallas_call(
        functools.partial(_ssd_kernel, reverse=reverse, nchunks=nc),
        grid=(bsz, nc),
        in_specs=in_specs,
        out_specs=pl.BlockSpec((t, SSD_WIDTH), lambda b, c: (rows(b, c), 0)),
        out_shape=jax.ShapeDtypeStruct((n, SSD_WIDTH), out_dtype),
        scratch_shapes=[pltpu.VMEM((t + 2 * HALO, SSD_WIDTH), F32),
                        pltpu.VMEM((t + 2 * HALO, SSD_BC), F32),
                        pltpu.VMEM((SSD_GROUPS, SSD_STATE, SSD_WIDTH // SSD_GROUPS), F32)],
        compiler_params=_cparams(("arbitrary", "arbitrary")),
    )(*args)


def _na_kernel(q_ref, k_ref, v_ref, bias_ref, o_ref, *, nrows):
    w = GRID_W
    band = NA_KH * w
    lane = _iota((w, LANES), 1)
    first = lane < NA_HEAD_DIM

    def body(r, carry):
        rs = jnp.clip(r - NA_KH // 2, 0, nrows - NA_KH)
        q = q_ref[pl.ds(pl.multiple_of(r * w, w), w), :] * jnp.asarray(NA_HEAD_DIM ** -0.5, BF16)
        zero = jnp.zeros((), BF16)
        qs = jnp.concatenate([jnp.where(first, q, zero), jnp.where(first, zero, q)], axis=0)
        kb = k_ref[pl.ds(pl.multiple_of(rs * w, w), band), :]
        vb = v_ref[pl.ds(pl.multiple_of(rs * w, w), band), :]
        s = _dot_nt(qs, kb) + bias_ref[0, r - rs]
        m = jnp.max(s, axis=-1, keepdims=True)
        p = jnp.exp(s - m)
        l = jnp.sum(p, axis=-1, keepdims=True)
        o = _dot(p.astype(BF16), vb) / l
        o_ref[pl.ds(pl.multiple_of(r * w, w), w), :] = jnp.where(first, o[0:w], o[w:2 * w]).astype(o_ref.dtype)
        return carry

    lax.fori_loop(0, nrows, body, 0)


def _na_bias(rpb):
    w = GRID_W
    delta = np.arange(NA_KH)
    i = np.arange(NA_KH)
    ro = i[None, :] - delta[:, None] + NA_KH - 1
    c = np.arange(w)
    kc = np.arange(w)
    col_start = np.clip(c - NA_KW // 2, 0, w - NA_KW)
    valid = (kc[None, :] >= col_start[:, None]) & (kc[None, :] < col_start[:, None] + NA_KW)
    co = np.clip(kc[None, :] - c[:, None] + NA_KW - 1, 0, 2 * NA_KW - 2)
    b = rpb.astype(F32)[:, ro[:, None, :, None], co[None, :, None, :]]
    b = jnp.where(valid[None, None, :, None, :], b, NEG_BIG)
    h = rpb.shape[0]
    b = b.reshape(h // 2, 2, NA_KH, w, NA_KH * w).transpose(0, 2, 1, 3, 4)
    return b.reshape(h // 2, NA_KH, 2 * w, NA_KH * w)


def _na(main, bias, bsz, seq, q_blk):
    n = main.shape[0]
    nrows = seq // GRID_W
    assert nrows >= NA_KH
    pairs = NA_HEADS // 2
    return pl.pallas_call(
        functools.partial(_na_kernel, nrows=nrows),
        grid=(bsz, pairs),
        in_specs=[pl.BlockSpec((seq, LANES), lambda b, p: (b, q_blk + p)),
                  pl.BlockSpec((seq, LANES), lambda b, p: (b, q_blk + pairs + p)),
                  pl.BlockSpec((seq, LANES), lambda b, p: (b, q_blk + 2 * pairs + p)),
                  pl.BlockSpec((1, NA_KH, 2 * GRID_W, NA_KH * GRID_W), lambda b, p: (p, 0, 0, 0))],
        out_specs=pl.BlockSpec((seq, LANES), lambda b, p: (b, p)),
        out_shape=jax.ShapeDtypeStruct((n, NA_WIDTH), BF16),
        compiler_params=_cparams(("arbitrary", "arbitrary")),
    )(main, main, main, bias)


def _log_sigmoid(x):
    return jnp.minimum(x, 0.0) - jnp.log(1.0 + jnp.exp(-jnp.abs(x)))


def _mlstm_kernel(q_ref, k_ref, v_ref, og_ref, g_ref, gt_ref, gb_ref, gbt_ref, hg_ref, o_ref,
                  hf_scr, c_scr, n_scr, m_scr, *, nchunks):
    t = ML_CHUNK
    head = pl.program_id(1)
    scale = ML_DK ** -0.5
    sel = (_iota((LANES, 4 * LANES), 0) == (_iota((LANES, 4 * LANES), 1) // LANES) * ML_HEADS + head).astype(BF16)
    sel_t = (_iota((8, 4 * ML_HEADS), 1) == _iota((8, 4 * ML_HEADS), 0) * ML_HEADS + head).astype(BF16)
    row = _iota((t, t), 0)
    col = _iota((t, t), 1)

    def chunk_step(c, reverse):
        r0 = pl.multiple_of(c * t, t)
        tri = (col >= row) if reverse else (col <= row)
        tri_b = tri.astype(BF16)
        tri_tb = ((row >= col) if reverse else (row <= col)).astype(BF16)
        gi, gf = (2, 3) if reverse else (0, 1)
        gcols = _dot_exact_rhs01(g_ref[pl.ds(r0, t), :] + gb_ref[...], sel)
        grows = _dot_exact_lhs01(sel_t, gt_ref[:, pl.ds(r0, t)] + gbt_ref[...])
        ig_c = gcols[:, gi * LANES:(gi + 1) * LANES]
        lf_c = _log_sigmoid(gcols[:, gf * LANES:(gf + 1) * LANES])
        ig_r = grows[gi:gi + 1, :]
        lf_r = _log_sigmoid(grows[gf:gf + 1, :])
        b_c = _dot_exact_lhs01(tri_b, lf_c)
        b_r = _dot_exact_rhs01(jnp.broadcast_to(lf_r, (8, t)), tri_tb)[0:1, :]
        g_tot = b_c[0:1, :] if reverse else b_c[t - 1:t, :]

        q = q_ref[pl.ds(r0, t), :]
        k = k_ref[pl.ds(r0, t), :]
        v = v_ref[pl.ds(r0, t), :]
        c_prev = c_scr[...]
        n_prev = n_scr[...]
        m_prev = m_scr[...]

        a_c = g_tot - b_c + ig_c
        m_loc = jnp.max(a_c, axis=0, keepdims=True)
        kw = k.astype(F32) * jnp.exp(a_c - m_loc)
        s_loc = _dot(kw.T.astype(BF16), v)
        n_loc = jnp.sum(kw, axis=0, keepdims=True)

        dm = jnp.where(tri, b_c - b_r + ig_r, NEG_BIG)
        m_inter = (b_c + m_prev)[:, 0:1]
        m_t = jnp.maximum(m_inter, jnp.max(dm, axis=-1, keepdims=True))
        sc = _dot_nt(q, k) * (jnp.exp(dm - m_t) * scale)
        inter = jnp.exp(m_inter - m_t) * scale
        num = _dot(sc.astype(BF16), v) + inter * _dot(q, c_prev.astype(BF16))
        den = (jnp.sum(sc, axis=-1, keepdims=True)
               + inter * jnp.sum(q.astype(F32) * n_prev, axis=-1, keepdims=True))
        hch = num / jnp.maximum(jnp.abs(den), jnp.exp(-m_t))

        m_new = jnp.maximum(g_tot + m_prev, m_loc)
        s_old = jnp.exp(g_tot + m_prev - m_new)
        s_new = jnp.exp(m_loc - m_new)
        c_scr[...] = (jnp.concatenate([s_old, s_old], axis=1) * c_prev
                      + jnp.concatenate([s_new, s_new], axis=1) * s_loc)
        n_scr[...] = s_old * n_prev + s_new * n_loc
        m_scr[...] = m_new

        if reverse:
            hsum = hf_scr[pl.ds(r0, t), :] + hch
            og = og_ref[pl.ds(r0, t), :].astype(F32)
            o_ref[pl.ds(r0, t), :] = (_sigmoid(og) * (_rms(hsum) * hg_ref[0])).astype(o_ref.dtype)
        else:
            hf_scr[pl.ds(r0, t), :] = hch

    def reset():
        c_scr[...] = jnp.zeros_like(c_scr)
        n_scr[...] = jnp.zeros_like(n_scr)
        m_scr[...] = jnp.zeros_like(m_scr)

    reset()
    lax.fori_loop(0, nchunks, lambda c, x: (chunk_step(c, False), x)[1], 0)
    reset()
    lax.fori_loop(0, nchunks, lambda c, x: (chunk_step(nchunks - 1 - c, True), x)[1], 0)


def _mlstm(main, gates, gates_t, gate_b, gate_bt, head_g, bsz, seq):
    n = main.shape[0]
    nc = seq // ML_CHUNK
    nh = ML_HEADS
    r = gates_t.shape[0]
    return pl.pallas_call(
        functools.partial(_mlstm_kernel, nchunks=nc),
        grid=(bsz, nh),
        in_specs=[pl.BlockSpec((seq, ML_DK), lambda b, h: (b, h)),
                  pl.BlockSpec((seq, ML_DK), lambda b, h: (b, nh + h)),
                  pl.BlockSpec((seq, ML_DV), lambda b, h: (b, nh + h)),
                  pl.BlockSpec((seq, ML_DV), lambda b, h: (b, 2 * nh + h)),
                  pl.BlockSpec((seq, LANES), lambda b, h: (b, 0)),
                  pl.BlockSpec((r, seq), lambda b, h: (0, b)),
                  pl.BlockSpec((1, LANES), lambda b, h: (0, 0)),
                  pl.BlockSpec((r, 1), lambda b, h: (0, 0)),
                  pl.BlockSpec((1, 1, ML_DV), lambda b, h: (h, 0, 0))],
        out_specs=pl.BlockSpec((seq, ML_DV), lambda b, h: (b, h)),
        out_shape=jax.ShapeDtypeStruct((n, ML_WIDTH), BF16),
        scratch_shapes=[pltpu.VMEM((seq, ML_DV), F32),
                        pltpu.VMEM((ML_DK, ML_DV), F32),
                        pltpu.VMEM((1, ML_DK), F32),
                        pltpu.VMEM((1, LANES), F32)],
        compiler_params=_cparams(("arbitrary", "arbitrary")),
    )(main, main, main, main, gates, gates_t, gate_b, gate_bt, head_g.reshape(nh, 1, ML_DV))


def _pad_cols(a, width):
    return jnp.pad(a, ((0, 0), (0, width - a.shape[1])))


def _ssd_na_layer(x2, mods, norm_g, w_in, conv_w, conv_b, dt_bias, a_log, d_skip, ssd_norm, rpb, w_out,
                  bsz, seq):
    sh1, sc1, g1 = mods
    s1 = SSD_WIDTH
    s2 = s1 + SSD_WIDTH + SSD_BC
    s3 = s2 + 2 * SSD_HEADS
    w_main = jnp.concatenate([w_in[:, :s2], w_in[:, s3:]], axis=1).astype(BF16)
    w_dt = _pad_cols(w_in[:, s2:s3], LANES).astype(BF16)
    main, dtf = _project(x2, norm_g[0], sh1, sc1, w_main, w_dt, None, seq, tm=min(1024, seq), tn=1408)

    wx, wbc = conv_w[:, :SSD_WIDTH], conv_w[:, SSD_WIDTH:]
    bx, bbc = conv_b[None, :SSD_WIDTH], conv_b[None, SSD_WIDTH:]
    dtb = _pad_cols(dt_bias.reshape(1, 2 * SSD_HEADS), LANES)
    alog = _pad_cols(a_log.reshape(1, 2 * SSD_HEADS), LANES)
    dskip = jnp.repeat(d_skip, SSD_HEAD_DIM)[None, :]
    y_f = _ssd_pass(main, dtf, wx, bx, wbc, bbc, dtb, alog, bsz, seq, reverse=False)
    y_ssd = _ssd_pass(main, dtf, wx, bx, wbc, bbc, dtb, alog, bsz, seq, reverse=True,
                      extra=(y_f, dskip, ssd_norm[None, :]))

    q_blk = (2 * SSD_WIDTH + SSD_BC) // LANES
    y_na = _na(main, _na_bias(rpb), bsz, seq, q_blk)
    return _out_project(y_ssd, 0, y_na, 0, w_out.astype(BF16), x2, g1, norm_g[1], seq, tm=min(512, seq))


def _mlstm_layer(x2, mods, norm_g, w_in, gate_b, head_g, w_out, bsz, seq):
    sh1, sc1, g1 = mods
    wm = 2 * ML_QK + 2 * ML_WIDTH
    w_main = w_in[:, :wm].astype(BF16)
    w_g = w_in[:, wm:]
    main, gates, gates_t = _project(x2, norm_g[0], sh1, sc1, w_main, _pad_cols(w_g, LANES).astype(BF16),
                                    w_g.T.astype(BF16), seq, tm=min(1024, seq), tn=1536)
    gb = gate_b.reshape(1, 4 * ML_HEADS).astype(F32)
    y = _mlstm(main, gates, gates_t, _pad_cols(gb, LANES), gb.reshape(4 * ML_HEADS, 1), head_g, bsz, seq)
    return _out_project(y, 0, y, 1, w_out.astype(BF16), x2, g1, norm_g[1], seq, tm=min(512, seq))


def kernel(x, c, ada_w, ada_b, norm_g, mlp_w1, mlp_w2, ab_w_in, ab_conv_w, ab_conv_b, ab_dt_bias, ab_a_log,
           ab_d_skip, ab_ssd_norm, ab_rpb, ab_w_out, ml_w_in, ml_gate_b, ml_head_norm, ml_w_out):
    bsz, seq, d = x.shape
    depth = ada_w.shape[0]
    mod = _adaln(c, ada_w, ada_b)
    x2 = x.reshape(bsz * seq, d)
    for layer in range(depth):
        sh1, sc1, g1, sh2, sc2, g2 = [mod[layer, :, i * d:(i + 1) * d] for i in range(6)]
        j = layer // 2
        if layer % 2 == 0:
            x2 = _ssd_na_layer(x2, (sh1, sc1, g1), norm_g[layer], ab_w_in[j], ab_conv_w[j], ab_conv_b[j],
                               ab_dt_bias[j], ab_a_log[j], ab_d_skip[j], ab_ssd_norm[j], ab_rpb[j],
                               ab_w_out[j], bsz, seq)
        else:
            x2 = _mlstm_layer(x2, (sh1, sc1, g1), norm_g[layer], ml_w_in[j], ml_gate_b[j], ml_head_norm[j],
                              ml_w_out[j], bsz, seq)
        x2 = _mlp(x2, norm_g[layer, 2], sh2, sc2, mlp_w1[layer].astype(BF16), mlp_w2[layer].astype(BF16),
                  g2, norm_g[layer, 3], seq, tm=min(1024, seq), tf=1024)
    return x2.reshape(bsz, seq, d)
```

```python
import functools

import numpy as np
import jax
import jax.numpy as jnp
from jax import lax
from jax.experimental import pallas as pl
from jax.experimental.pallas import tpu as pltpu

F32 = jnp.float32
BF16 = jnp.bfloat16

NORM_EPS = 1e-6
GRID_W = 64

SSD_HEAD_DIM = 64
SSD_HEADS = 16
SSD_GROUPS = 2
SSD_STATE = 128
SSD_CONV = 5
SSD_CHUNK = 128
SSD_WIDTH = SSD_HEADS * SSD_HEAD_DIM
SSD_BC = 2 * SSD_GROUPS * SSD_STATE

NA_HEAD_DIM = 64
NA_HEADS = 16
NA_KH = 8
NA_KW = 16
NA_WIDTH = NA_HEADS * NA_HEAD_DIM

ML_HEADS = 8
ML_DV = 256
ML_DK = 128
ML_CHUNK = 128
ML_WIDTH = ML_HEADS * ML_DV
ML_QK = ML_HEADS * ML_DK

LANES = 128
SUBLANES = 8
HALO = 16
NEG_BIG = -1e30
VMEM_LIMIT = 56 * 1024 * 1024


def _cparams(sem):
    return pltpu.CompilerParams(dimension_semantics=sem, vmem_limit_bytes=VMEM_LIMIT)


def _dot(a, b):
    return jnp.dot(a, b, preferred_element_type=F32)


def _dot_nt(a, b):
    return lax.dot_general(a, b, (((1,), (1,)), ((), ())), preferred_element_type=F32)


def _split3(x):
    hi = x.astype(BF16)
    r1 = x - hi.astype(F32)
    mid = r1.astype(BF16)
    lo = (r1 - mid.astype(F32)).astype(BF16)
    return hi, mid, lo


def _dot_exact_rhs01(x, sel):
    hi, mid, lo = _split3(x)
    return _dot(hi, sel) + _dot(mid, sel) + _dot(lo, sel)


def _dot_exact_lhs01(sel, x):
    hi, mid, lo = _split3(x)
    return _dot(sel, hi) + _dot(sel, mid) + _dot(sel, lo)


def _iota(shape, dim):
    return lax.broadcasted_iota(jnp.int32, shape, dim)


def _sigmoid(x):
    return 1.0 / (1.0 + jnp.exp(-x))


def _softplus(x):
    return jnp.maximum(x, 0.0) + jnp.log(1.0 + jnp.exp(-jnp.abs(x)))


def _log_sigmoid(x):
    return jnp.minimum(x, 0.0) - jnp.log(1.0 + jnp.exp(-jnp.abs(x)))


def _rms(x):
    return x * lax.rsqrt(jnp.mean(x * x, axis=-1, keepdims=True) + NORM_EPS)


def _mod_kernel(c_ref, w_ref, b_ref, o_ref):
    c = c_ref[...]
    cond = c * _sigmoid(c)
    o_ref[0] = jnp.dot(cond, w_ref[0], preferred_element_type=F32,
                       precision=lax.Precision.HIGHEST) + b_ref[0]


def _adaln(c, ada_w, ada_b):
    depth, d, d6 = ada_w.shape
    bsz = c.shape[0]
    tn = 1024
    return pl.pallas_call(
        _mod_kernel,
        grid=(depth, d6 // tn),
        in_specs=[pl.BlockSpec((bsz, d), lambda l, j: (0, 0)),
                  pl.BlockSpec((1, d, tn), lambda l, j: (l, 0, j)),
                  pl.BlockSpec((1, 1, tn), lambda l, j: (l, 0, j))],
        out_specs=pl.BlockSpec((1, bsz, tn), lambda l, j: (l, 0, j)),
        out_shape=jax.ShapeDtypeStruct((depth, bsz, d6), F32),
        compiler_params=_cparams(("arbitrary", "arbitrary")),
        name="adaln",
    )(c, ada_w, ada_b.reshape(depth, 1, d6))


def _proj_kernel(x_ref, g_ref, sh_ref, sc_ref, w_ref, ws_ref, *rest, transposed_small):
    if transposed_small:
        wst_ref, o_ref, os_ref, ost_ref, h_scr = rest
    else:
        o_ref, os_ref, h_scr = rest

    @pl.when(pl.program_id(1) == 0)
    def _():
        h = _rms(x_ref[...]) * g_ref[...] * (1.0 + sc_ref[0]) + sh_ref[0]
        hb = h.astype(BF16)
        h_scr[...] = hb
        os_ref[...] = _dot(hb, ws_ref[...])
        if transposed_small:
            ost_ref[...] = _dot_nt(wst_ref[...], hb)

    o_ref[...] = _dot(h_scr[...], w_ref[...]).astype(o_ref.dtype)


def _project(x2, g, shift, scale, w, ws, wst, seq, tm, tn, name):
    n, d = x2.shape
    wn = w.shape[1]
    tiles_per_batch = seq // tm
    bsz = shift.shape[0]
    in_specs = [pl.BlockSpec((tm, d), lambda i, j: (i, 0)),
                pl.BlockSpec((1, d), lambda i, j: (0, 0)),
                pl.BlockSpec((1, 1, d), lambda i, j: (i // tiles_per_batch, 0, 0)),
                pl.BlockSpec((1, 1, d), lambda i, j: (i // tiles_per_batch, 0, 0)),
                pl.BlockSpec((d, tn), lambda i, j: (0, j)),
                pl.BlockSpec((d, LANES), lambda i, j: (0, 0))]
    args = [x2, g.reshape(1, d), shift.reshape(bsz, 1, d), scale.reshape(bsz, 1, d), w, ws]
    out_specs = [pl.BlockSpec((tm, tn), lambda i, j: (i, j)),
                 pl.BlockSpec((tm, LANES), lambda i, j: (i, 0))]
    out_shape = [jax.ShapeDtypeStruct((n, wn), BF16), jax.ShapeDtypeStruct((n, LANES), F32)]
    if wst is not None:
        r = wst.shape[0]
        in_specs.append(pl.BlockSpec((r, d), lambda i, j: (0, 0)))
        args.append(wst)
        out_specs.append(pl.BlockSpec((r, tm), lambda i, j: (0, i)))
        out_shape.append(jax.ShapeDtypeStruct((r, n), F32))
    return pl.pallas_call(
        functools.partial(_proj_kernel, transposed_small=wst is not None),
        grid=(n // tm, wn // tn),
        in_specs=in_specs, out_specs=out_specs, out_shape=out_shape,
        scratch_shapes=[pltpu.VMEM((tm, d), BF16)],
        compiler_params=_cparams(("arbitrary", "arbitrary")),
        name=name,
    )(*args)


def _out_kernel(ya_ref, yb_ref, wa_ref, wb_ref, x_ref, gate_ref, g_ref, o_ref):
    acc = _dot(ya_ref[...], wa_ref[...]) + _dot(yb_ref[...], wb_ref[...])
    o_ref[...] = x_ref[...] + gate_ref[0] * (_rms(acc) * g_ref[...])


def _out_project(ya, ia, yb, ib, w, x2, gate, g, seq, tm, name):
    n, d = x2.shape
    kh = w.shape[0] // 2
    tiles_per_batch = seq // tm
    bsz = gate.shape[0]
    return pl.pallas_call(
        _out_kernel,
        grid=(n // tm,),
        in_specs=[pl.BlockSpec((tm, kh), lambda i: (i, ia)),
                  pl.BlockSpec((tm, kh), lambda i: (i, ib)),
                  pl.BlockSpec((kh, d), lambda i: (0, 0)),
                  pl.BlockSpec((kh, d), lambda i: (1, 0)),
                  pl.BlockSpec((tm, d), lambda i: (i, 0)),
                  pl.BlockSpec((1, 1, d), lambda i: (i // tiles_per_batch, 0, 0)),
                  pl.BlockSpec((1, d), lambda i: (0, 0))],
        out_specs=pl.BlockSpec((tm, d), lambda i: (i, 0)),
        out_shape=jax.ShapeDtypeStruct((n, d), F32),
        compiler_params=_cparams(("arbitrary",)),
        name=name,
    )(ya, yb, w, w, x2, gate.reshape(bsz, 1, d), g.reshape(1, d))


def _mlp_kernel(x_ref, g_ref, sh_ref, sc_ref, w1_ref, w2_ref, gate_ref, g2_ref, o_ref, h_scr, acc_scr):
    f = pl.program_id(1)

    @pl.when(f == 0)
    def _():
        h = _rms(x_ref[...]) * g_ref[...] * (1.0 + sc_ref[0]) + sh_ref[0]
        h_scr[...] = h.astype(BF16)
        acc_scr[...] = jnp.zeros_like(acc_scr)

    a = jnp.maximum(_dot(h_scr[...], w1_ref[...]), 0.0)
    acc_scr[...] += _dot((a * a).astype(BF16), w2_ref[...])

    @pl.when(f == pl.num_programs(1) - 1)
    def _():
        o_ref[...] = x_ref[...] + gate_ref[0] * (_rms(acc_scr[...]) * g2_ref[...])


def _mlp(x2, g, shift, scale, w1, w2, gate, g2, seq, tm, tf, name):
    n, d = x2.shape
    dff = w1.shape[1]
    tiles_per_batch = seq // tm
    bsz = gate.shape[0]
    vec = lambda a: a.reshape(bsz, 1, d)
    bspec = pl.BlockSpec((1, 1, d), lambda i, f: (i // tiles_per_batch, 0, 0))
    return pl.pallas_call(
        _mlp_kernel,
        grid=(n // tm, dff // tf),
        in_specs=[pl.BlockSpec((tm, d), lambda i, f: (i, 0)),
                  pl.BlockSpec((1, d), lambda i, f: (0, 0)),
                  bspec, bspec,
                  pl.BlockSpec((d, tf), lambda i, f: (0, f)),
                  pl.BlockSpec((tf, d), lambda i, f: (f, 0)),
                  bspec,
                  pl.BlockSpec((1, d), lambda i, f: (0, 0))],
        out_specs=pl.BlockSpec((tm, d), lambda i, f: (i, 0)),
        out_shape=jax.ShapeDtypeStruct((n, d), F32),
        scratch_shapes=[pltpu.VMEM((tm, d), BF16), pltpu.VMEM((tm, d), F32)],
        compiler_params=_cparams(("arbitrary", "arbitrary")),
        name=name,
    )(x2, g.reshape(1, d), vec(shift), vec(scale), w1, w2, vec(gate), g2.reshape(1, d))


def _conv_silu(main_ref, prev_ref, next_ref, w_ref, b_ref, scr, keep_prev, keep_next):
    t = main_ref.shape[0]
    scr[0:HALO, :] = prev_ref[...].astype(F32) * keep_prev
    scr[HALO:HALO + t, :] = main_ref[...].astype(F32)
    scr[HALO + t:2 * HALO + t, :] = next_ref[...].astype(F32) * keep_next
    pad = SSD_CONV // 2
    acc = b_ref[...] + w_ref[0:1, :] * scr[HALO - pad:HALO - pad + t, :]
    for k in range(1, SSD_CONV):
        acc = acc + w_ref[k:k + 1, :] * scr[HALO - pad + k:HALO - pad + k + t, :]
    return acc * _sigmoid(acc)


def _ssd_kernel(xs_ref, xsp_ref, xsn_ref, bc_ref, bcp_ref, bcn_ref, dt_ref,
                wx_ref, bx_ref, wbc_ref, bbc_ref, dtb_ref, alog_ref, *rest, reverse, nchunks):
    if reverse:
        z_ref, yf_ref, dskip_ref, ng_ref, o_ref, xs_scr, bc_scr, st_scr = rest
    else:
        o_ref, xs_scr, bc_scr, st_scr = rest
    t = SSD_CHUNK
    c = pl.program_id(1)
    chunk = nchunks - 1 - c if reverse else c

    @pl.when(c == 0)
    def _():
        st_scr[...] = jnp.zeros_like(st_scr)

    keep_prev = (chunk > 0).astype(F32)
    keep_next = (chunk < nchunks - 1).astype(F32)
    xs = _conv_silu(xs_ref, xsp_ref, xsn_ref, wx_ref, bx_ref, xs_scr, keep_prev, keep_next)
    bc = _conv_silu(bc_ref, bcp_ref, bcn_ref, wbc_ref, bbc_ref, bc_scr, keep_prev, keep_next)

    dt = _softplus(dt_ref[...] + dtb_ref[...])
    adt = dt * (-jnp.exp(alog_ref[...]))
    row = _iota((t, t), 0)
    col = _iota((t, t), 1)
    tri = (col >= row) if reverse else (col <= row)
    cs = _dot_exact_lhs01(tri.astype(BF16), adt)
    cs_t = cs.T
    dt_t = dt.T
    tot = cs[0:1, :] if reverse else cs[t - 1:t, :]

    base = SSD_HEADS if reverse else 0
    hd = SSD_HEAD_DIM
    e64 = (_iota((LANES, SSD_WIDTH), 0) == base + _iota((LANES, SSD_WIDTH), 1) // hd).astype(BF16)
    e128 = (_iota((LANES, SSD_HEADS * t), 0) == base + _iota((LANES, SSD_HEADS * t), 1) // t).astype(BF16)
    cs_x = _dot_exact_rhs01(cs, e64)
    tot_x = _dot_exact_rhs01(tot, e64)
    cs_b = _dot_exact_rhs01(cs, e128)
    decay_x = jnp.exp(tot_x - cs_x)
    xd = (xs * decay_x).astype(BF16)
    ecs_x = jnp.exp(cs_x)

    lane = _iota((t, LANES), 1)
    ys = []
    for g in range(SSD_GROUPS):
        bg = bc[:, g * SSD_STATE:(g + 1) * SSD_STATE]
        cg = bc[:, (SSD_GROUPS + g) * SSD_STATE:(SSD_GROUPS + g + 1) * SSD_STATE].astype(BF16)
        cb = _dot_nt(cg, bg.astype(BF16))
        bg_t = bg.T
        hpg = SSD_HEADS // SSD_GROUPS
        s_prev = st_scr[g]
        y_off = _dot(cg, s_prev.astype(BF16)) * ecs_x[:, g * hpg * hd:(g + 1) * hpg * hd]
        s_new = []
        for pr in range(hpg // 2):
            acc = None
            snew_pair = None
            for sub in range(2):
                h = g * hpg + 2 * pr + sub
                j = base + h
                dts = dt_t[j:j + 1, :]
                lmat = jnp.exp(jnp.where(tri, cs_b[:, h * t:(h + 1) * t] - cs_t[j:j + 1, :], NEG_BIG))
                m = (cb * lmat * dts).astype(BF16)
                xpair = xs[:, (h // 2) * LANES:(h // 2 + 1) * LANES]
                keep = (lane < hd) if sub == 0 else (lane >= hd)
                xh = jnp.where(keep, xpair, 0.0).astype(BF16)
                part = _dot(m, xh)
                acc = part if acc is None else acc + part
                xdh = jnp.where(keep, xd[:, (h // 2) * LANES:(h // 2 + 1) * LANES], jnp.zeros((), BF16))
                sp = _dot((bg_t * dts).astype(BF16), xdh)
                snew_pair = sp if snew_pair is None else snew_pair + sp
            ys.append(acc + y_off[:, pr * LANES:(pr + 1) * LANES])
            s_new.append(snew_pair)
        etot = jnp.exp(tot_x[:, g * hpg * hd:(g + 1) * hpg * hd])
        st_scr[g] = s_prev * etot + jnp.concatenate(s_new, axis=1)
    y = jnp.concatenate(ys, axis=1)

    if reverse:
        y = y + yf_ref[...] + dskip_ref[...] * xs
        z = z_ref[...].astype(F32)
        y = y * (z * _sigmoid(z))
        o_ref[...] = (_rms(y) * ng_ref[...]).astype(o_ref.dtype)
    else:
        o_ref[...] = y


def _ssd_pass(main, dtf, wx, bx, wbc, bbc, dtb, alog, bsz, seq, reverse, extra=None):
    n = main.shape[0]
    t = SSD_CHUNK
    nc = seq // t
    xs_blk = 1
    bc_blk = (2 * SSD_WIDTH) // SSD_BC
    rpb = t // HALO
    last_halo = n // HALO - 1

    def ch(c):
        return nc - 1 - c if reverse else c

    def rows(b, c):
        return b * nc + ch(c)

    def prev(b, c):
        return jnp.maximum(rows(b, c) * rpb - 1, 0)

    def nxt(b, c):
        return jnp.minimum((rows(b, c) + 1) * rpb, last_halo)

    in_specs = [pl.BlockSpec((t, SSD_WIDTH), lambda b, c: (rows(b, c), xs_blk)),
                pl.BlockSpec((HALO, SSD_WIDTH), lambda b, c: (prev(b, c), xs_blk)),
                pl.BlockSpec((HALO, SSD_WIDTH), lambda b, c: (nxt(b, c), xs_blk)),
                pl.BlockSpec((t, SSD_BC), lambda b, c: (rows(b, c), bc_blk)),
                pl.BlockSpec((HALO, SSD_BC), lambda b, c: (prev(b, c), bc_blk)),
                pl.BlockSpec((HALO, SSD_BC), lambda b, c: (nxt(b, c), bc_blk)),
                pl.BlockSpec((t, LANES), lambda b, c: (rows(b, c), 0)),
                pl.BlockSpec((SSD_CONV, SSD_WIDTH), lambda b, c: (0, 0)),
                pl.BlockSpec((1, SSD_WIDTH), lambda b, c: (0, 0)),
                pl.BlockSpec((SSD_CONV, SSD_BC), lambda b, c: (0, 0)),
                pl.BlockSpec((1, SSD_BC), lambda b, c: (0, 0)),
                pl.BlockSpec((1, LANES), lambda b, c: (0, 0)),
                pl.BlockSpec((1, LANES), lambda b, c: (0, 0))]
    args = [main, main, main, main, main, main, dtf, wx, bx, wbc, bbc, dtb, alog]
    if reverse:
        yf, dskip, ng = extra
        in_specs += [pl.BlockSpec((t, SSD_WIDTH), lambda b, c: (rows(b, c), 0)),
                     pl.BlockSpec((t, SSD_WIDTH), lambda b, c: (rows(b, c), 0)),
                     pl.BlockSpec((1, SSD_WIDTH), lambda b, c: (0, 0)),
                     pl.BlockSpec((1, SSD_WIDTH), lambda b, c: (0, 0))]
        args += [main, yf, dskip, ng]
        out_dtype = BF16
    else:
        out_dtype = F32
    return pl.pallas_call(
        functools.partial(_ssd_kernel, reverse=reverse, nchunks=nc),
        grid=(bsz, nc),
        in_specs=in_specs,
        out_specs=pl.BlockSpec((t, SSD_WIDTH), lambda b, c: (rows(b, c), 0)),
        out_shape=jax.ShapeDtypeStruct((n, SSD_WIDTH), out_dtype),
        scratch_shapes=[pltpu.VMEM((t + 2 * HALO, SSD_WIDTH), F32),
                        pltpu.VMEM((t + 2 * HALO, SSD_BC), F32),
                        pltpu.VMEM((SSD_GROUPS, SSD_STATE, SSD_WIDTH // SSD_GROUPS), F32)],
        compiler_params=_cparams(("arbitrary", "arbitrary")),
        name="ssd_bwd" if reverse else "ssd_fwd",
    )(*args)


def _na_kernel(q_ref, k_ref, v_ref, bias_ref, o_ref, *, nrows):
    w = GRID_W
    band = NA_KH * w
    lane = _iota((w, LANES), 1)
    first = lane < NA_HEAD_DIM

    def body(r, carry):
        rs = jnp.clip(r - NA_KH // 2, 0, nrows - NA_KH)
        q = q_ref[pl.ds(pl.multiple_of(r * w, w), w), :] * jnp.asarray(NA_HEAD_DIM ** -0.5, BF16)
        zero = jnp.zeros((), BF16)
        qs = jnp.concatenate([jnp.where(first, q, zero), jnp.where(first, zero, q)], axis=0)
        kb = k_ref[pl.ds(pl.multiple_of(rs * w, w), band), :]
        vb = v_ref[pl.ds(pl.multiple_of(rs * w, w), band), :]
        s = _dot_nt(qs, kb) + bias_ref[0, r - rs]
        m = jnp.max(s, axis=-1, keepdims=True)
        p = jnp.exp(s - m)
        l = jnp.sum(p, axis=-1, keepdims=True)
        o = _dot(p.astype(BF16), vb) / l
        o_ref[pl.ds(pl.multiple_of(r * w, w), w), :] = jnp.where(first, o[0:w], o[w:2 * w]).astype(o_ref.dtype)
        return carry

    lax.fori_loop(0, nrows, body, 0)


def _na_bias(rpb):
    w = GRID_W
    h, nro, nrel = rpb.shape
    c = np.arange(w)
    kc = np.arange(w)
    col_start = np.clip(c - NA_KW // 2, 0, w - NA_KW)
    valid = (kc[None, :] >= col_start[:, None]) & (kc[None, :] < col_start[:, None] + NA_KW)
    co = kc[None, :] - c[:, None] + NA_KW - 1
    onehot = ((np.arange(nrel)[:, None, None] == co[None]) & valid[None]).astype(np.float32)
    toe = jnp.dot(rpb.astype(F32).reshape(h * nro, nrel), onehot.reshape(nrel, w * w),
                  precision=lax.Precision.HIGHEST).reshape(h, nro, w, w)
    toe = toe + np.where(valid, 0.0, NEG_BIG).astype(np.float32)
    slabs = [toe[:, NA_KH - 1 - d:2 * NA_KH - 1 - d] for d in range(NA_KH)]
    b = jnp.stack(slabs, axis=1).transpose(0, 1, 3, 2, 4)
    b = b.reshape(h // 2, 2, NA_KH, w, NA_KH * w).transpose(0, 2, 1, 3, 4)
    return b.reshape(h // 2, NA_KH, 2 * w, NA_KH * w)


def _na(main, bias, bsz, seq, q_blk):
    n = main.shape[0]
    nrows = seq // GRID_W
    assert nrows >= NA_KH
    pairs = NA_HEADS // 2
    return pl.pallas_call(
        functools.partial(_na_kernel, nrows=nrows),
        grid=(bsz, pairs),
        in_specs=[pl.BlockSpec((seq, LANES), lambda b, p: (b, q_blk + p)),
                  pl.BlockSpec((seq, LANES), lambda b, p: (b, q_blk + pairs + p)),
                  pl.BlockSpec((seq, LANES), lambda b, p: (b, q_blk + 2 * pairs + p)),
                  pl.BlockSpec((1, NA_KH, 2 * GRID_W, NA_KH * GRID_W), lambda b, p: (p, 0, 0, 0))],
        out_specs=pl.BlockSpec((seq, LANES), lambda b, p: (b, p)),
        out_shape=jax.ShapeDtypeStruct((n, NA_WIDTH), BF16),
        compiler_params=_cparams(("arbitrary", "arbitrary")),
        name="nbr_attn",
    )(main, main, main, bias)


def _mlstm_kernel(q_ref, k_ref, v_ref, og_ref, g_ref, gt_ref, gb_ref, gbt_ref, hg_ref, o_ref,
                  bc_scr, rs_scr, cp_scr, np_scr, mp_scr, c_st, n_st, m_st, *, nchunks):
    t = ML_CHUNK
    head = pl.program_id(1)
    scale = ML_DK ** -0.5
    row = _iota((t, t), 0)
    col = _iota((t, t), 1)
    tris = (col <= row, col >= row)
    lane1 = _iota((1, LANES), 1)
    sub1 = _iota((4 * ML_HEADS, 1), 0)

    def gate_col(gc, kind):
        return jnp.sum(jnp.where(lane1 == kind * ML_HEADS + head, gc, 0.0), axis=-1, keepdims=True)

    def gate_row(gr, kind):
        return jnp.sum(jnp.where(sub1 == kind * ML_HEADS + head, gr, 0.0), axis=0, keepdims=True)

    c_st[...] = jnp.zeros_like(c_st)
    n_st[...] = jnp.zeros_like(n_st)
    m_st[...] = jnp.zeros_like(m_st)

    def scan_step(i, carry):
        for d in range(2):
            c = nchunks - 1 - i if d else i
            r0 = pl.multiple_of(c * t, t)
            gc = g_ref[pl.ds(r0, t), :] + gb_ref[...]
            gr = gt_ref[:, pl.ds(r0, t)] + gbt_ref[...]
            ig_c = gate_col(gc, 2 * d)
            lf_c = _log_sigmoid(gate_col(gc, 2 * d + 1))
            b_c = _dot_exact_lhs01(tris[d].astype(BF16), jnp.broadcast_to(lf_c, (t, LANES)))
            g_tot = b_c[0:1, :] if d else b_c[t - 1:t, :]
            a_c = g_tot - b_c + ig_c
            m_loc = jnp.max(a_c, axis=0, keepdims=True)
            kw = k_ref[pl.ds(r0, t), :].astype(F32) * jnp.exp(a_c - m_loc)
            s_loc = _dot(kw.T.astype(BF16), v_ref[pl.ds(r0, t), :])
            n_loc = jnp.sum(kw, axis=0, keepdims=True)
            bc_scr[d, pl.ds(r0, t), :] = b_c
            rs_scr[d, c] = jnp.broadcast_to(gate_row(gr, 2 * d) - b_c.T[0:1, :], (SUBLANES, t))
            c_prev = c_st[d]
            n_prev = n_st[d]
            m_prev = m_st[d]
            cp_scr[d, c] = c_prev.astype(BF16)
            np_scr[d, c] = jnp.broadcast_to(n_prev, (SUBLANES, ML_DK))
            mp_scr[d, c] = jnp.broadcast_to(m_prev, (SUBLANES, LANES))
            m_new = jnp.maximum(g_tot + m_prev, m_loc)
            s_old = jnp.exp(g_tot + m_prev - m_new)
            s_new = jnp.exp(m_loc - m_new)
            c_st[d] = (jnp.concatenate([s_old, s_old], axis=1) * c_prev
                       + jnp.concatenate([s_new, s_new], axis=1) * s_loc)
            n_st[d] = s_old * n_prev + s_new * n_loc
            m_st[d] = m_new
        return carry

    lax.fori_loop(0, nchunks, scan_step, 0, unroll=2)

    def out_step(c, carry):
        r0 = pl.multiple_of(c * t, t)
        q = q_ref[pl.ds(r0, t), :]
        k = k_ref[pl.ds(r0, t), :]
        v = v_ref[pl.ds(r0, t), :]
        qk = _dot_nt(q, k)
        qf = q.astype(F32)
        hsum = None
        for d in range(2):
            b_c = bc_scr[d, pl.ds(r0, t), :]
            m_prev = mp_scr[d, c][0:1, :]
            dm = jnp.where(tris[d], b_c + rs_scr[d, c][0:1, :], NEG_BIG)
            m_inter = (b_c + m_prev)[:, 0:1]
            m_t = jnp.maximum(m_inter, jnp.max(dm, axis=-1, keepdims=True))
            sc = qk * (jnp.exp(dm - m_t) * scale)
            inter = jnp.exp(m_inter - m_t) * scale
            num = _dot(sc.astype(BF16), v) + inter * _dot(q, cp_scr[d, c])
            den = (jnp.sum(sc, axis=-1, keepdims=True)
                   + inter * jnp.sum(qf * np_scr[d, c][0:1, :], axis=-1, keepdims=True))
            hd = num / jnp.maximum(jnp.abs(den), jnp.exp(-m_t))
            hsum = hd if hsum is None else hsum + hd
        og = og_ref[pl.ds(r0, t), :].astype(F32)
        o_ref[pl.ds(r0, t), :] = (_sigmoid(og) * (_rms(hsum) * hg_ref[0])).astype(o_ref.dtype)
        return carry

    lax.fori_loop(0, nchunks, out_step, 0)


def _mlstm(main, gates, gates_t, gate_b, gate_bt, head_g, bsz, seq):
    n = main.shape[0]
    nc = seq // ML_CHUNK
    nh = ML_HEADS
    r = gates_t.shape[0]
    return pl.pallas_call(
        functools.partial(_mlstm_kernel, nchunks=nc),
        grid=(bsz, nh),
        in_specs=[pl.BlockSpec((seq, ML_DK), lambda b, h: (b, h)),
                  pl.BlockSpec((seq, ML_DK), lambda b, h: (b, nh + h)),
                  pl.BlockSpec((seq, ML_DV), lambda b, h: (b, nh + h)),
                  pl.BlockSpec((seq, ML_DV), lambda b, h: (b, 2 * nh + h)),
                  pl.BlockSpec((seq, LANES), lambda b, h: (b, 0)),
                  pl.BlockSpec((r, seq), lambda b, h: (0, b)),
                  pl.BlockSpec((1, LANES), lambda b, h: (0, 0)),
                  pl.BlockSpec((r, 1), lambda b, h: (0, 0)),
                  pl.BlockSpec((1, 1, ML_DV), lambda b, h: (h, 0, 0))],
        out_specs=pl.BlockSpec((seq, ML_DV), lambda b, h: (b, h)),
        out_shape=jax.ShapeDtypeStruct((n, ML_WIDTH), BF16),
        scratch_shapes=[pltpu.VMEM((2, seq, LANES), F32),
                        pltpu.VMEM((2, nc, SUBLANES, ML_CHUNK), F32),
                        pltpu.VMEM((2, nc, ML_DK, ML_DV), BF16),
                        pltpu.VMEM((2, nc, SUBLANES, ML_DK), F32),
                        pltpu.VMEM((2, nc, SUBLANES, LANES), F32),
                        pltpu.VMEM((2, ML_DK, ML_DV), F32),
                        pltpu.VMEM((2, 1, ML_DK), F32),
                        pltpu.VMEM((2, 1, LANES), F32)],
        compiler_params=_cparams(("arbitrary", "arbitrary")),
        name="mlstm",
    )(main, main, main, main, gates, gates_t, gate_b, gate_bt, head_g.reshape(nh, 1, ML_DV))


def _pad_cols(a, width):
    return jnp.pad(a, ((0, 0), (0, width - a.shape[1])))


def _ssd_na_layer(x2, mods, norm_g, w_in, conv_w, conv_b, dt_bias, a_log, d_skip, ssd_norm, rpb, w_out,
                  bsz, seq):
    sh1, sc1, g1 = mods
    s1 = SSD_WIDTH
    s2 = s1 + SSD_WIDTH + SSD_BC
    s3 = s2 + 2 * SSD_HEADS
    w_main = jnp.concatenate([w_in[:, :s2], w_in[:, s3:]], axis=1).astype(BF16)
    w_dt = _pad_cols(w_in[:, s2:s3], LANES).astype(BF16)
    main, dtf = _project(x2, norm_g[0], sh1, sc1, w_main, w_dt, None, seq, tm=min(1024, seq), tn=1408,
                         name="proj_ssd_na")

    wx, wbc = conv_w[:, :SSD_WIDTH], conv_w[:, SSD_WIDTH:]
    bx, bbc = conv_b[None, :SSD_WIDTH], conv_b[None, SSD_WIDTH:]
    dtb = _pad_cols(dt_bias.reshape(1, 2 * SSD_HEADS), LANES)
    alog = _pad_cols(a_log.reshape(1, 2 * SSD_HEADS), LANES)
    dskip = jnp.repeat(d_skip, SSD_HEAD_DIM)[None, :]
    y_f = _ssd_pass(main, dtf, wx, bx, wbc, bbc, dtb, alog, bsz, seq, reverse=False)
    y_ssd = _ssd_pass(main, dtf, wx, bx, wbc, bbc, dtb, alog, bsz, seq, reverse=True,
                      extra=(y_f, dskip, ssd_norm[None, :]))

    q_blk = (2 * SSD_WIDTH + SSD_BC) // LANES
    y_na = _na(main, _na_bias(rpb), bsz, seq, q_blk)
    return _out_project(y_ssd, 0, y_na, 0, w_out.astype(BF16), x2, g1, norm_g[1], seq, tm=min(512, seq),
                        name="out_ssd_na")


def _mlstm_layer(x2, mods, norm_g, w_in, gate_b, head_g, w_out, bsz, seq):
    sh1, sc1, g1 = mods
    wm = 2 * ML_QK + 2 * ML_WIDTH
    w_main = w_in[:, :wm].astype(BF16)
    w_g = w_in[:, wm:]
    main, gates, gates_t = _project(x2, norm_g[0], sh1, sc1, w_main, _pad_cols(w_g, LANES).astype(BF16),
                                    w_g.T.astype(BF16), seq, tm=min(1024, seq), tn=1536, name="proj_mlstm")
    gb = gate_b.reshape(1, 4 * ML_HEADS).astype(F32)
    y = _mlstm(main, gates, gates_t, _pad_cols(gb, LANES), gb.reshape(4 * ML_HEADS, 1), head_g, bsz, seq)
    return _out_project(y, 0, y, 1, w_out.astype(BF16), x2, g1, norm_g[1], seq, tm=min(512, seq),
                        name="out_mlstm")


def kernel(x, c, ada_w, ada_b, norm_g, mlp_w1, mlp_w2, ab_w_in, ab_conv_w, ab_conv_b, ab_dt_bias, ab_a_log,
           ab_d_skip, ab_ssd_norm, ab_rpb, ab_w_out, ml_w_in, ml_gate_b, ml_head_norm, ml_w_out):
    bsz, seq, d = x.shape
    depth = ada_w.shape[0]
    mod = _adaln(c, ada_w, ada_b)
    x2 = x.reshape(bsz * seq, d)
    for layer in range(depth):
        sh1, sc1, g1, sh2, sc2, g2 = [mod[layer, :, i * d:(i + 1) * d] for i in range(6)]
        j = layer // 2
        if layer % 2 == 0:
            x2 = _ssd_na_layer(x2, (sh1, sc1, g1), norm_g[layer], ab_w_in[j], ab_conv_w[j], ab_conv_b[j],
                               ab_dt_bias[j], ab_a_log[j], ab_d_skip[j], ab_ssd_norm[j], ab_rpb[j],
                               ab_w_out[j], bsz, seq)
        else:
            x2 = _mlstm_layer(x2, (sh1, sc1, g1), norm_g[layer], ml_w_in[j], ml_gate_b[j], ml_head_norm[j],
                              ml_w_out[j], bsz, seq)
        x2 = _mlp(x2, norm_g[layer, 2], sh2, sc2, mlp_w1[layer].astype(BF16), mlp_w2[layer].astype(BF16),
                  g2, norm_g[layer, 3], seq, tm=min(1024, seq), tf=1024, name="mlp%d" % layer)
    return x2.reshape(bsz, seq, d)
```

```python
import functools

import numpy as np
import jax
import jax.numpy as jnp
from jax import lax
from jax.experimental import pallas as pl
from jax.experimental.pallas import tpu as pltpu

F32 = jnp.float32
BF16 = jnp.bfloat16

NORM_EPS = 1e-6
GRID_W = 64

SSD_HEAD_DIM = 64
SSD_HEADS = 16
SSD_GROUPS = 2
SSD_STATE = 128
SSD_CONV = 5
SSD_CHUNK = 128
SSD_WIDTH = SSD_HEADS * SSD_HEAD_DIM
SSD_BC = 2 * SSD_GROUPS * SSD_STATE
SSD_CONV_CH = SSD_WIDTH + SSD_BC

NA_HEAD_DIM = 64
NA_HEADS = 16
NA_KH = 8
NA_KW = 16
NA_WIDTH = NA_HEADS * NA_HEAD_DIM

ML_HEADS = 8
ML_DV = 256
ML_DK = 128
ML_CHUNK = 128
ML_WIDTH = ML_HEADS * ML_DV
ML_QK = ML_HEADS * ML_DK

LANES = 128
SUBLANES = 8
HALO = 16
NEG_BIG = -1e30
VMEM_LIMIT = 56 * 1024 * 1024


def _cparams(sem):
    return pltpu.CompilerParams(dimension_semantics=sem, vmem_limit_bytes=VMEM_LIMIT)


def _dot(a, b):
    return jnp.dot(a, b, preferred_element_type=F32)


def _dot_nt(a, b):
    return lax.dot_general(a, b, (((1,), (1,)), ((), ())), preferred_element_type=F32)


def _split3(x):
    hi = x.astype(BF16)
    r1 = x - hi.astype(F32)
    mid = r1.astype(BF16)
    lo = (r1 - mid.astype(F32)).astype(BF16)
    return hi, mid, lo


def _dot_exact_lhs01(sel, x):
    hi, mid, lo = _split3(x)
    return _dot(sel, hi) + _dot(sel, mid) + _dot(sel, lo)


def _iota(shape, dim):
    return lax.broadcasted_iota(jnp.int32, shape, dim)


def _sigmoid(x):
    return 0.5 * (jnp.tanh(0.5 * x) + 1.0)


def _softplus(x):
    return jnp.maximum(x, 0.0) + jnp.log(1.0 + jnp.exp(-jnp.abs(x)))


def _log_sigmoid(x):
    return jnp.minimum(x, 0.0) - jnp.log(1.0 + jnp.exp(-jnp.abs(x)))


def _rms(x):
    return x * lax.rsqrt(jnp.mean(x * x, axis=-1, keepdims=True) + NORM_EPS)


def _mod_kernel(c_ref, w_ref, b_ref, o_ref):
    c = c_ref[...]
    cond = c * _sigmoid(c)
    o_ref[0] = jnp.dot(cond, w_ref[0], preferred_element_type=F32,
                       precision=lax.Precision.HIGHEST) + b_ref[0]


def _adaln(c, ada_w, ada_b):
    depth, d, d6 = ada_w.shape
    bsz = c.shape[0]
    tn = 1024
    return pl.pallas_call(
        _mod_kernel,
        grid=(depth, d6 // tn),
        in_specs=[pl.BlockSpec((bsz, d), lambda l, j: (0, 0)),
                  pl.BlockSpec((1, d, tn), lambda l, j: (l, 0, j)),
                  pl.BlockSpec((1, 1, tn), lambda l, j: (l, 0, j))],
        out_specs=pl.BlockSpec((1, bsz, tn), lambda l, j: (l, 0, j)),
        out_shape=jax.ShapeDtypeStruct((depth, bsz, d6), F32),
        compiler_params=_cparams(("arbitrary", "arbitrary")),
        name="adaln",
    )(c, ada_w, ada_b.reshape(depth, 1, d6))


def _proj_kernel(x_ref, g_ref, sh_ref, sc_ref, w_ref, ws_ref, *rest, transposed_small):
    if transposed_small:
        wst_ref, o_ref, os_ref, ost_ref, h_scr = rest
    else:
        o_ref, os_ref, h_scr = rest

    @pl.when(pl.program_id(1) == 0)
    def _():
        h = _rms(x_ref[...]) * g_ref[...] * (1.0 + sc_ref[0]) + sh_ref[0]
        hb = h.astype(BF16)
        h_scr[...] = hb
        os_ref[...] = _dot(hb, ws_ref[...])
        if transposed_small:
            ost_ref[...] = _dot_nt(wst_ref[...], hb)

    o_ref[...] = _dot(h_scr[...], w_ref[...]).astype(o_ref.dtype)


def _project(x2, g, shift, scale, w, ws, wst, seq, tm, tn, name):
    n, d = x2.shape
    wn = w.shape[1]
    tiles_per_batch = seq // tm
    bsz = shift.shape[0]
    in_specs = [pl.BlockSpec((tm, d), lambda i, j: (i, 0)),
                pl.BlockSpec((1, d), lambda i, j: (0, 0)),
                pl.BlockSpec((1, 1, d), lambda i, j: (i // tiles_per_batch, 0, 0)),
                pl.BlockSpec((1, 1, d), lambda i, j: (i // tiles_per_batch, 0, 0)),
                pl.BlockSpec((d, tn), lambda i, j: (0, j)),
                pl.BlockSpec((d, LANES), lambda i, j: (0, 0))]
    args = [x2, g.reshape(1, d), shift.reshape(bsz, 1, d), scale.reshape(bsz, 1, d), w, ws]
    out_specs = [pl.BlockSpec((tm, tn), lambda i, j: (i, j)),
                 pl.BlockSpec((tm, LANES), lambda i, j: (i, 0))]
    out_shape = [jax.ShapeDtypeStruct((n, wn), BF16), jax.ShapeDtypeStruct((n, LANES), F32)]
    if wst is not None:
        r = wst.shape[0]
        in_specs.append(pl.BlockSpec((r, d), lambda i, j: (0, 0)))
        args.append(wst)
        out_specs.append(pl.BlockSpec((r, tm), lambda i, j: (0, i)))
        out_shape.append(jax.ShapeDtypeStruct((r, n), F32))
    return pl.pallas_call(
        functools.partial(_proj_kernel, transposed_small=wst is not None),
        grid=(n // tm, wn // tn),
        in_specs=in_specs, out_specs=out_specs, out_shape=out_shape,
        scratch_shapes=[pltpu.VMEM((tm, d), BF16)],
        compiler_params=_cparams(("arbitrary", "arbitrary")),
        name=name,
    )(*args)


def _out_kernel(ya_ref, yb_ref, wa_ref, wb_ref, x_ref, gate_ref, g_ref, o_ref):
    acc = _dot(ya_ref[...], wa_ref[...]) + _dot(yb_ref[...], wb_ref[...])
    o_ref[...] = x_ref[...] + gate_ref[0] * (_rms(acc) * g_ref[...])


def _out_project(ya, ia, yb, ib, w, x2, gate, g, seq, tm, name):
    n, d = x2.shape
    kh = w.shape[0] // 2
    tiles_per_batch = seq // tm
    bsz = gate.shape[0]
    return pl.pallas_call(
        _out_kernel,
        grid=(n // tm,),
        in_specs=[pl.BlockSpec((tm, kh), lambda i: (i, ia)),
                  pl.BlockSpec((tm, kh), lambda i: (i, ib)),
                  pl.BlockSpec((kh, d), lambda i: (0, 0)),
                  pl.BlockSpec((kh, d), lambda i: (1, 0)),
                  pl.BlockSpec((tm, d), lambda i: (i, 0)),
                  pl.BlockSpec((1, 1, d), lambda i: (i // tiles_per_batch, 0, 0)),
                  pl.BlockSpec((1, d), lambda i: (0, 0))],
        out_specs=pl.BlockSpec((tm, d), lambda i: (i, 0)),
        out_shape=jax.ShapeDtypeStruct((n, d), F32),
        compiler_params=_cparams(("arbitrary",)),
        name=name,
    )(ya, yb, w, w, x2, gate.reshape(bsz, 1, d), g.reshape(1, d))


def _mlp_kernel(x_ref, g_ref, sh_ref, sc_ref, w1_ref, w2_ref, gate_ref, g2_ref, o_ref, h_scr, acc_scr):
    f = pl.program_id(1)

    @pl.when(f == 0)
    def _():
        h = _rms(x_ref[...]) * g_ref[...] * (1.0 + sc_ref[0]) + sh_ref[0]
        h_scr[...] = h.astype(BF16)
        acc_scr[...] = jnp.zeros_like(acc_scr)

    a = jnp.maximum(_dot(h_scr[...], w1_ref[...]), 0.0)
    acc_scr[...] += _dot((a * a).astype(BF16), w2_ref[...])

    @pl.when(f == pl.num_programs(1) - 1)
    def _():
        o_ref[...] = x_ref[...] + gate_ref[0] * (_rms(acc_scr[...]) * g2_ref[...])


def _mlp(x2, g, shift, scale, w1, w2, gate, g2, seq, tm, tf, name):
    n, d = x2.shape
    dff = w1.shape[1]
    tiles_per_batch = seq // tm
    bsz = gate.shape[0]
    vec = lambda a: a.reshape(bsz, 1, d)
    bspec = pl.BlockSpec((1, 1, d), lambda i, f: (i // tiles_per_batch, 0, 0))
    return pl.pallas_call(
        _mlp_kernel,
        grid=(n // tm, dff // tf),
        in_specs=[pl.BlockSpec((tm, d), lambda i, f: (i, 0)),
                  pl.BlockSpec((1, d), lambda i, f: (0, 0)),
                  bspec, bspec,
                  pl.BlockSpec((d, tf), lambda i, f: (0, f)),
                  pl.BlockSpec((tf, d), lambda i, f: (f, 0)),
                  bspec,
                  pl.BlockSpec((1, d), lambda i, f: (0, 0))],
        out_specs=pl.BlockSpec((tm, d), lambda i, f: (i, 0)),
        out_shape=jax.ShapeDtypeStruct((n, d), F32),
        scratch_shapes=[pltpu.VMEM((tm, d), BF16), pltpu.VMEM((tm, d), F32)],
        compiler_params=_cparams(("arbitrary", "arbitrary")),
        name=name,
    )(x2, g.reshape(1, d), vec(shift), vec(scale), w1, w2, vec(gate), g2.reshape(1, d))


def _conv_kernel(main_ref, prev_ref, next_ref, w_ref, b_ref, o_ref, scr, *, blocks_per_seq):
    t = main_ref.shape[0]
    i = pl.program_id(0) % blocks_per_seq
    keep_prev = (i > 0).astype(F32)
    keep_next = (i < blocks_per_seq - 1).astype(F32)
    scr[0:HALO, :] = prev_ref[...].astype(F32) * keep_prev
    scr[HALO:HALO + t, :] = main_ref[...].astype(F32)
    scr[HALO + t:2 * HALO + t, :] = next_ref[...].astype(F32) * keep_next
    pad = SSD_CONV // 2
    acc = b_ref[...] + w_ref[0:1, :] * scr[HALO - pad:HALO - pad + t, :]
    for k in range(1, SSD_CONV):
        acc = acc + w_ref[k:k + 1, :] * scr[HALO - pad + k:HALO - pad + k + t, :]
    o_ref[...] = (acc * _sigmoid(acc)).astype(o_ref.dtype)


def _conv_silu(main, col0, conv_w, conv_b, seq, tr, tc):
    n = main.shape[0]
    ch = conv_w.shape[1]
    cb0 = col0 // tc
    rpb = tr // HALO
    last_halo = n // HALO - 1
    return pl.pallas_call(
        functools.partial(_conv_kernel, blocks_per_seq=seq // tr),
        grid=(n // tr, ch // tc),
        in_specs=[pl.BlockSpec((tr, tc), lambda i, j: (i, cb0 + j)),
                  pl.BlockSpec((HALO, tc), lambda i, j: (jnp.maximum(i * rpb - 1, 0), cb0 + j)),
                  pl.BlockSpec((HALO, tc), lambda i, j: (jnp.minimum((i + 1) * rpb, last_halo), cb0 + j)),
                  pl.BlockSpec((SSD_CONV, tc), lambda i, j: (0, j)),
                  pl.BlockSpec((1, tc), lambda i, j: (0, j))],
        out_specs=pl.BlockSpec((tr, tc), lambda i, j: (i, j)),
        out_shape=jax.ShapeDtypeStruct((n, ch), BF16),
        scratch_shapes=[pltpu.VMEM((tr + 2 * HALO, tc), F32)],
        compiler_params=_cparams(("arbitrary", "arbitrary")),
        name="conv_silu",
    )(main, main, main, conv_w, conv_b.reshape(1, ch))


def _ssd_kernel(xs_ref, bc_ref, dt_ref, dtb_ref, alog_ref, *rest, reverse):
    if reverse:
        z_ref, yf_ref, dskip_ref, ng_ref, o_ref, st_scr = rest
    else:
        o_ref, st_scr = rest
    t = SSD_CHUNK
    hd = SSD_HEAD_DIM
    hpg = SSD_HEADS // SSD_GROUPS

    @pl.when(pl.program_id(1) == 0)
    def _():
        st_scr[...] = jnp.zeros_like(st_scr)

    xs = xs_ref[...].astype(F32)
    bc = bc_ref[...]

    dt = _softplus(dt_ref[...] + dtb_ref[...])
    adt = dt * (-jnp.exp(alog_ref[...]))
    row = _iota((t, t), 0)
    col = _iota((t, t), 1)
    tri = (col >= row) if reverse else (col <= row)
    cs = _dot_exact_lhs01(tri.astype(BF16), adt)
    cs_t = cs.T
    dt_t = dt.T
    base = SSD_HEADS if reverse else 0
    last = 0 if reverse else t - 1

    first_half = _iota((t, LANES), 1) < hd
    ys = []
    for g in range(SSD_GROUPS):
        bg = bc[:, g * SSD_STATE:(g + 1) * SSD_STATE]
        cg = bc[:, (SSD_GROUPS + g) * SSD_STATE:(SSD_GROUPS + g + 1) * SSD_STATE]
        cb = _dot_nt(cg, bg)
        bg_t = bg.astype(F32).T
        s_prev = st_scr[g]
        y_off = _dot(cg, s_prev.astype(BF16))
        s_new = []
        etot = []
        for pr in range(hpg // 2):
            pair = g * (hpg // 2) + pr
            xpair = xs[:, pair * LANES:(pair + 1) * LANES]
            cs_cols = []
            acc = None
            for sub in range(2):
                j = base + 2 * pair + sub
                cs_col = jnp.broadcast_to(cs[:, j:j + 1], (t, t))
                cs_cols.append(cs_col)
                dts = dt_t[j:j + 1, :]
                lmat = jnp.exp(jnp.where(tri, cs_col - cs_t[j:j + 1, :], NEG_BIG))
                m = (cb * lmat * dts).astype(BF16)
                keep = first_half if sub == 0 else jnp.logical_not(first_half)
                part = _dot(m, jnp.where(keep, xpair, 0.0).astype(BF16))
                acc = part if acc is None else acc + part
            cs_pair = jnp.where(first_half, cs_cols[0], cs_cols[1])
            tot_pair = cs_pair[last:last + 1, :]
            xd = (xpair * jnp.exp(tot_pair - cs_pair)).astype(BF16)
            sp = None
            for sub in range(2):
                j = base + 2 * pair + sub
                keep = first_half if sub == 0 else jnp.logical_not(first_half)
                part = _dot((bg_t * dt_t[j:j + 1, :]).astype(BF16), jnp.where(keep, xd, jnp.zeros((), BF16)))
                sp = part if sp is None else sp + part
            ys.append(acc + y_off[:, pr * LANES:(pr + 1) * LANES] * jnp.exp(cs_pair))
            s_new.append(sp)
            etot.append(jnp.exp(tot_pair))
        st_scr[g] = s_prev * jnp.concatenate(etot, axis=1) + jnp.concatenate(s_new, axis=1)
    y = jnp.concatenate(ys, axis=1)

    if reverse:
        y = y + yf_ref[...] + dskip_ref[...] * xs
        z = z_ref[...].astype(F32)
        y = y * (z * _sigmoid(z))
        o_ref[...] = (_rms(y) * ng_ref[...]).astype(o_ref.dtype)
    else:
        o_ref[...] = y


def _ssd_pass(conv, dtf, dtb, alog, bsz, seq, reverse, extra=None):
    n = conv.shape[0]
    t = SSD_CHUNK
    nc = seq // t
    bc_blk = SSD_WIDTH // SSD_BC

    def rows(b, c):
        return b * nc + (nc - 1 - c if reverse else c)

    in_specs = [pl.BlockSpec((t, SSD_WIDTH), lambda b, c: (rows(b, c), 0)),
                pl.BlockSpec((t, SSD_BC), lambda b, c: (rows(b, c), bc_blk)),
                pl.BlockSpec((t, LANES), lambda b, c: (rows(b, c), 0)),
                pl.BlockSpec((1, LANES), lambda b, c: (0, 0)),
                pl.BlockSpec((1, LANES), lambda b, c: (0, 0))]
    args = [conv, conv, dtf, dtb, alog]
    if reverse:
        main, yf, dskip, ng = extra
        in_specs += [pl.BlockSpec((t, SSD_WIDTH), lambda b, c: (rows(b, c), 0)),
                     pl.BlockSpec((t, SSD_WIDTH), lambda b, c: (rows(b, c), 0)),
                     pl.BlockSpec((1, SSD_WIDTH), lambda b, c: (0, 0)),
                     pl.BlockSpec((1, SSD_WIDTH), lambda b, c: (0, 0))]
        args += [main, yf, dskip, ng]
        out_dtype = BF16
    else:
        out_dtype = F32
    return pl.pallas_call(
        functools.partial(_ssd_kernel, reverse=reverse),
        grid=(bsz, nc),
        in_specs=in_specs,
        out_specs=pl.BlockSpec((t, SSD_WIDTH), lambda b, c: (rows(b, c), 0)),
        out_shape=jax.ShapeDtypeStruct((n, SSD_WIDTH), out_dtype),
        scratch_shapes=[pltpu.VMEM((SSD_GROUPS, SSD_STATE, SSD_WIDTH // SSD_GROUPS), F32)],
        compiler_params=_cparams(("arbitrary", "arbitrary")),
        name="ssd_bwd" if reverse else "ssd_fwd",
    )(*args)


def _na_kernel(q_ref, k_ref, v_ref, bias_ref, o_ref, *, nrows):
    w = GRID_W
    band = NA_KH * w
    lane = _iota((w, LANES), 1)
    first = lane < NA_HEAD_DIM

    def body(r, carry):
        rs = jnp.clip(r - NA_KH // 2, 0, nrows - NA_KH)
        q = q_ref[pl.ds(pl.multiple_of(r * w, w), w), :] * jnp.asarray(NA_HEAD_DIM ** -0.5, BF16)
        zero = jnp.zeros((), BF16)
        qs = jnp.concatenate([jnp.where(first, q, zero), jnp.where(first, zero, q)], axis=0)
        kb = k_ref[pl.ds(pl.multiple_of(rs * w, w), band), :]
        vb = v_ref[pl.ds(pl.multiple_of(rs * w, w), band), :]
        s = _dot_nt(qs, kb) + bias_ref[0, r - rs]
        m = jnp.max(s, axis=-1, keepdims=True)
        p = jnp.exp(s - m)
        l = jnp.sum(p, axis=-1, keepdims=True)
        o = _dot(p.astype(BF16), vb) / l
        o_ref[pl.ds(pl.multiple_of(r * w, w), w), :] = jnp.where(first, o[0:w], o[w:2 * w]).astype(o_ref.dtype)
        return carry

    lax.fori_loop(0, nrows, body, 0, unroll=2)


def _na_bias(rpb):
    w = GRID_W
    h, nro, nrel = rpb.shape
    c = np.arange(w)
    kc = np.arange(w)
    col_start = np.clip(c - NA_KW // 2, 0, w - NA_KW)
    valid = (kc[None, :] >= col_start[:, None]) & (kc[None, :] < col_start[:, None] + NA_KW)
    co = kc[None, :] - c[:, None] + NA_KW - 1
    onehot = ((np.arange(nrel)[:, None, None] == co[None]) & valid[None]).astype(np.float32)
    toe = jnp.dot(rpb.astype(F32).reshape(h * nro, nrel), onehot.reshape(nrel, w * w),
                  precision=lax.Precision.HIGHEST).reshape(h, nro, w, w)
    toe = toe + np.where(valid, 0.0, NEG_BIG).astype(np.float32)
    slabs = [toe[:, NA_KH - 1 - d:2 * NA_KH - 1 - d] for d in range(NA_KH)]
    b = jnp.stack(slabs, axis=1).transpose(0, 1, 3, 2, 4)
    b = b.reshape(h // 2, 2, NA_KH, w, NA_KH * w).transpose(0, 2, 1, 3, 4)
    return b.reshape(h // 2, NA_KH, 2 * w, NA_KH * w)


def _na(main, bias, bsz, seq, q_blk):
    n = main.shape[0]
    nrows = seq // GRID_W
    assert nrows >= NA_KH
    pairs = NA_HEADS // 2
    return pl.pallas_call(
        functools.partial(_na_kernel, nrows=nrows),
        grid=(bsz, pairs),
        in_specs=[pl.BlockSpec((seq, LANES), lambda b, p: (b, q_blk + p)),
                  pl.BlockSpec((seq, LANES), lambda b, p: (b, q_blk + pairs + p)),
                  pl.BlockSpec((seq, LANES), lambda b, p: (b, q_blk + 2 * pairs + p)),
                  pl.BlockSpec((1, NA_KH, 2 * GRID_W, NA_KH * GRID_W), lambda b, p: (p, 0, 0, 0))],
        out_specs=pl.BlockSpec((seq, LANES), lambda b, p: (b, p)),
        out_shape=jax.ShapeDtypeStruct((n, NA_WIDTH), BF16),
        compiler_params=_cparams(("arbitrary", "arbitrary")),
        name="nbr_attn",
    )(main, main, main, bias)


def _mlstm_kernel(q_ref, k_ref, v_ref, og_ref, g_ref, gt_ref, gb_ref, gbt_ref, hg_ref, o_ref,
                  bc_scr, rs_scr, cp_scr, np_scr, mp_scr, c_st, n_st, m_st, *, nchunks):
    t = ML_CHUNK
    head = pl.program_id(1)
    scale = ML_DK ** -0.5
    row = _iota((t, t), 0)
    col = _iota((t, t), 1)
    tris = (col <= row, col >= row)
    lane1 = _iota((1, LANES), 1)
    sub1 = _iota((4 * ML_HEADS, 1), 0)

    def gate_col(gc, kind):
        return jnp.sum(jnp.where(lane1 == kind * ML_HEADS + head, gc, 0.0), axis=-1, keepdims=True)

    def gate_row(gr, kind):
        return jnp.sum(jnp.where(sub1 == kind * ML_HEADS + head, gr, 0.0), axis=0, keepdims=True)

    c_st[...] = jnp.zeros_like(c_st)
    n_st[...] = jnp.zeros_like(n_st)
    m_st[...] = jnp.zeros_like(m_st)

    def scan_step(i, carry):
        for d in range(2):
            c = nchunks - 1 - i if d else i
            r0 = pl.multiple_of(c * t, t)
            gc = g_ref[pl.ds(r0, t), :] + gb_ref[...]
            gr = gt_ref[:, pl.ds(r0, t)] + gbt_ref[...]
            ig_c = gate_col(gc, 2 * d)
            lf_c = _log_sigmoid(gate_col(gc, 2 * d + 1))
            b_c = _dot_exact_lhs01(tris[d].astype(BF16), jnp.broadcast_to(lf_c, (t, LANES)))
            g_tot = b_c[0:1, :] if d else b_c[t - 1:t, :]
            a_c = g_tot - b_c + ig_c
            m_loc = jnp.max(a_c, axis=0, keepdims=True)
            kw = k_ref[pl.ds(r0, t), :].astype(F32) * jnp.exp(a_c - m_loc)
            s_loc = _dot(kw.T.astype(BF16), v_ref[pl.ds(r0, t), :])
            n_loc = jnp.sum(kw, axis=0, keepdims=True)
            bc_scr[d, pl.ds(r0, t), :] = b_c
            rs_scr[d, c] = jnp.broadcast_to(gate_row(gr, 2 * d) - b_c.T[0:1, :], (SUBLANES, t))
            c_prev = c_st[d]
            n_prev = n_st[d]
            m_prev = m_st[d]
            cp_scr[d, c] = c_prev.astype(BF16)
            np_scr[d, c] = jnp.broadcast_to(n_prev, (SUBLANES, ML_DK))
            mp_scr[d, c] = jnp.broadcast_to(m_prev, (SUBLANES, LANES))
            m_new = jnp.maximum(g_tot + m_prev, m_loc)
            s_old = jnp.exp(g_tot + m_prev - m_new)
            s_new = jnp.exp(m_loc - m_new)
            c_st[d] = (jnp.concatenate([s_old, s_old], axis=1) * c_prev
                       + jnp.concatenate([s_new, s_new], axis=1) * s_loc)
            n_st[d] = s_old * n_prev + s_new * n_loc
            m_st[d] = m_new
        return carry

    lax.fori_loop(0, nchunks, scan_step, 0, unroll=2)

    def out_step(c, carry):
        r0 = pl.multiple_of(c * t, t)
        q = q_ref[pl.ds(r0, t), :]
        k = k_ref[pl.ds(r0, t), :]
        v = v_ref[pl.ds(r0, t), :]
        qk = _dot_nt(q, k)
        qf = q.astype(F32)
        hsum = None
        for d in range(2):
            b_c = bc_scr[d, pl.ds(r0, t), :]
            m_prev = mp_scr[d, c][0:1, :]
            dm = jnp.where(tris[d], b_c + rs_scr[d, c][0:1, :], NEG_BIG)
            m_inter = (b_c + m_prev)[:, 0:1]
            m_t = jnp.maximum(m_inter, jnp.max(dm, axis=-1, keepdims=True))
            sc = qk * (jnp.exp(dm - m_t) * scale)
            inter = jnp.exp(m_inter - m_t) * scale
            num = _dot(sc.astype(BF16), v) + inter * _dot(q, cp_scr[d, c])
            den = (jnp.sum(sc, axis=-1, keepdims=True)
                   + inter * jnp.sum(qf * np_scr[d, c][0:1, :], axis=-1, keepdims=True))
            hd = num / jnp.maximum(jnp.abs(den), jnp.exp(-m_t))
            hsum = hd if hsum is None else hsum + hd
        og = og_ref[pl.ds(r0, t), :].astype(F32)
        o_ref[pl.ds(r0, t), :] = (_sigmoid(og) * (_rms(hsum) * hg_ref[0])).astype(o_ref.dtype)
        return carry

    lax.fori_loop(0, nchunks, out_step, 0)


def _mlstm(main, gates, gates_t, gate_b, gate_bt, head_g, bsz, seq):
    n = main.shape[0]
    nc = seq // ML_CHUNK
    nh = ML_HEADS
    r = gates_t.shape[0]
    return pl.pallas_call(
        functools.partial(_mlstm_kernel, nchunks=nc),
        grid=(bsz, nh),
        in_specs=[pl.BlockSpec((seq, ML_DK), lambda b, h: (b, h)),
                  pl.BlockSpec((seq, ML_DK), lambda b, h: (b, nh + h)),
                  pl.BlockSpec((seq, ML_DV), lambda b, h: (b, nh + h)),
                  pl.BlockSpec((seq, ML_DV), lambda b, h: (b, 2 * nh + h)),
                  pl.BlockSpec((seq, LANES), lambda b, h: (b, 0)),
                  pl.BlockSpec((r, seq), lambda b, h: (0, b)),
                  pl.BlockSpec((1, LANES), lambda b, h: (0, 0)),
                  pl.BlockSpec((r, 1), lambda b, h: (0, 0)),
                  pl.BlockSpec((1, 1, ML_DV), lambda b, h: (h, 0, 0))],
        out_specs=pl.BlockSpec((seq, ML_DV), lambda b, h: (b, h)),
        out_shape=jax.ShapeDtypeStruct((n, ML_WIDTH), BF16),
        scratch_shapes=[pltpu.VMEM((2, seq, LANES), F32),
                        pltpu.VMEM((2, nc, SUBLANES, ML_CHUNK), F32),
                        pltpu.VMEM((2, nc, ML_DK, ML_DV), BF16),
                        pltpu.VMEM((2, nc, SUBLANES, ML_DK), F32),
                        pltpu.VMEM((2, nc, SUBLANES, LANES), F32),
                        pltpu.VMEM((2, ML_DK, ML_DV), F32),
                        pltpu.VMEM((2, 1, ML_DK), F32),
                        pltpu.VMEM((2, 1, LANES), F32)],
        compiler_params=_cparams(("arbitrary", "arbitrary")),
        name="mlstm",
    )(main, main, main, main, gates, gates_t, gate_b, gate_bt, head_g.reshape(nh, 1, ML_DV))


def _pad_cols(a, width):
    return jnp.pad(a, ((0, 0), (0, width - a.shape[1])))


def _ssd_na_layer(x2, mods, norm_g, w_in, conv_w, conv_b, dt_bias, a_log, d_skip, ssd_norm, rpb, w_out,
                  bsz, seq):
    sh1, sc1, g1 = mods
    s1 = SSD_WIDTH
    s2 = s1 + SSD_CONV_CH
    s3 = s2 + 2 * SSD_HEADS
    w_main = jnp.concatenate([w_in[:, :s2], w_in[:, s3:]], axis=1).astype(BF16)
    w_dt = _pad_cols(w_in[:, s2:s3], LANES).astype(BF16)
    main, dtf = _project(x2, norm_g[0], sh1, sc1, w_main, w_dt, None, seq, tm=min(1024, seq), tn=1408,
                         name="proj_ssd_na")

    conv = _conv_silu(main, s1, conv_w, conv_b, seq, tr=min(512, seq), tc=512)
    dtb = _pad_cols(dt_bias.reshape(1, 2 * SSD_HEADS), LANES)
    alog = _pad_cols(a_log.reshape(1, 2 * SSD_HEADS), LANES)
    dskip = jnp.repeat(d_skip, SSD_HEAD_DIM)[None, :]
    y_f = _ssd_pass(conv, dtf, dtb, alog, bsz, seq, reverse=False)
    y_ssd = _ssd_pass(conv, dtf, dtb, alog, bsz, seq, reverse=True,
                      extra=(main, y_f, dskip, ssd_norm[None, :]))

    q_blk = (s1 + SSD_CONV_CH) // LANES
    y_na = _na(main, _na_bias(rpb), bsz, seq, q_blk)
    return _out_project(y_ssd, 0, y_na, 0, w_out.astype(BF16), x2, g1, norm_g[1], seq, tm=min(512, seq),
                        name="out_ssd_na")


def _mlstm_layer(x2, mods, norm_g, w_in, gate_b, head_g, w_out, bsz, seq):
    sh1, sc1, g1 = mods
    wm = 2 * ML_QK + 2 * ML_WIDTH
    w_main = w_in[:, :wm].astype(BF16)
    w_g = w_in[:, wm:]
    main, gates, gates_t = _project(x2, norm_g[0], sh1, sc1, w_main, _pad_cols(w_g, LANES).astype(BF16),
                                    w_g.T.astype(BF16), seq, tm=min(1024, seq), tn=1536, name="proj_mlstm")
    gb = gate_b.reshape(1, 4 * ML_HEADS).astype(F32)
    y = _mlstm(main, gates, gates_t, _pad_cols(gb, LANES), gb.reshape(4 * ML_HEADS, 1), head_g, bsz, seq)
    return _out_project(y, 0, y, 1, w_out.astype(BF16), x2, g1, norm_g[1], seq, tm=min(512, seq),
                        name="out_mlstm")


def kernel(x, c, ada_w, ada_b, norm_g, mlp_w1, mlp_w2, ab_w_in, ab_conv_w, ab_conv_b, ab_dt_bias, ab_a_log,
           ab_d_skip, ab_ssd_norm, ab_rpb, ab_w_out, ml_w_in, ml_gate_b, ml_head_norm, ml_w_out):
    bsz, seq, d = x.shape
    depth = ada_w.shape[0]
    mod = _adaln(c, ada_w, ada_b)
    x2 = x.reshape(bsz * seq, d)
    for layer in range(depth):
        sh1, sc1, g1, sh2, sc2, g2 = [mod[layer, :, i * d:(i + 1) * d] for i in range(6)]
        j = layer // 2
        if layer % 2 == 0:
            x2 = _ssd_na_layer(x2, (sh1, sc1, g1), norm_g[layer], ab_w_in[j], ab_conv_w[j], ab_conv_b[j],
                               ab_dt_bias[j], ab_a_log[j], ab_d_skip[j], ab_ssd_norm[j], ab_rpb[j],
                               ab_w_out[j], bsz, seq)
        else:
            x2 = _mlstm_layer(x2, (sh1, sc1, g1), norm_g[layer], ml_w_in[j], ml_gate_b[j], ml_head_norm[j],
                              ml_w_out[j], bsz, seq)
        x2 = _mlp(x2, norm_g[layer, 2], sh2, sc2, mlp_w1[layer].astype(BF16), mlp_w2[layer].astype(BF16),
                  g2, norm_g[layer, 3], seq, tm=min(1024, seq), tf=1024, name="mlp%d" % layer)
    return x2.reshape(bsz, seq, d)
```

```python
import functools

import numpy as np
import jax
import jax.numpy as jnp
from jax import lax
from jax.experimental import pallas as pl
from jax.experimental.pallas import tpu as pltpu

F32 = jnp.float32
BF16 = jnp.bfloat16

NORM_EPS = 1e-6
GRID_W = 64

SSD_HEAD_DIM = 64
SSD_HEADS = 16
SSD_GROUPS = 2
SSD_STATE = 128
SSD_CONV = 5
SSD_CHUNK = 128
SSD_WIDTH = SSD_HEADS * SSD_HEAD_DIM
SSD_BC = 2 * SSD_GROUPS * SSD_STATE
SSD_CONV_CH = SSD_WIDTH + SSD_BC

NA_HEAD_DIM = 64
NA_HEADS = 16
NA_KH = 8
NA_KW = 16
NA_WIDTH = NA_HEADS * NA_HEAD_DIM

ML_HEADS = 8
ML_DV = 256
ML_DK = 128
ML_CHUNK = 256
ML_WIDTH = ML_HEADS * ML_DV
ML_QK = ML_HEADS * ML_DK

LANES = 128
SUBLANES = 8
HALO = 16
NEG_BIG = -1e30
VMEM_LIMIT = 56 * 1024 * 1024


def _cparams(sem):
    return pltpu.CompilerParams(dimension_semantics=sem, vmem_limit_bytes=VMEM_LIMIT)


def _dot(a, b):
    return jnp.dot(a, b, preferred_element_type=F32)


def _dot_nt(a, b):
    return lax.dot_general(a, b, (((1,), (1,)), ((), ())), preferred_element_type=F32)


def _split3(x):
    hi = x.astype(BF16)
    r1 = x - hi.astype(F32)
    mid = r1.astype(BF16)
    lo = (r1 - mid.astype(F32)).astype(BF16)
    return hi, mid, lo


def _dot_exact_lhs01(sel, x):
    hi, mid, lo = _split3(x)
    return _dot(sel, hi) + _dot(sel, mid) + _dot(sel, lo)


def _iota(shape, dim):
    return lax.broadcasted_iota(jnp.int32, shape, dim)


def _sigmoid(x):
    return 0.5 * (jnp.tanh(0.5 * x) + 1.0)


def _softplus(x):
    return jnp.maximum(x, 0.0) + jnp.log(1.0 + jnp.exp(-jnp.abs(x)))


def _log_sigmoid(x):
    return jnp.minimum(x, 0.0) - jnp.log(1.0 + jnp.exp(-jnp.abs(x)))


def _rms(x):
    return x * lax.rsqrt(jnp.mean(x * x, axis=-1, keepdims=True) + NORM_EPS)


def _mod_kernel(c_ref, w_ref, b_ref, o_ref):
    c = c_ref[...]
    cond = c * _sigmoid(c)
    o_ref[0] = jnp.dot(cond, w_ref[0], preferred_element_type=F32,
                       precision=lax.Precision.HIGHEST) + b_ref[0]


def _adaln(c, ada_w, ada_b):
    depth, d, d6 = ada_w.shape
    bsz = c.shape[0]
    tn = 1024
    return pl.pallas_call(
        _mod_kernel,
        grid=(depth, d6 // tn),
        in_specs=[pl.BlockSpec((bsz, d), lambda l, j: (0, 0)),
                  pl.BlockSpec((1, d, tn), lambda l, j: (l, 0, j)),
                  pl.BlockSpec((1, 1, tn), lambda l, j: (l, 0, j))],
        out_specs=pl.BlockSpec((1, bsz, tn), lambda l, j: (l, 0, j)),
        out_shape=jax.ShapeDtypeStruct((depth, bsz, d6), F32),
        compiler_params=_cparams(("arbitrary", "arbitrary")),
        name="adaln",
    )(c, ada_w, ada_b.reshape(depth, 1, d6))


def _proj_kernel(x_ref, g_ref, sh_ref, sc_ref, w_ref, ws_ref, *rest, transposed_small):
    if transposed_small:
        wst_ref, o_ref, os_ref, ost_ref, h_scr = rest
    else:
        o_ref, os_ref, h_scr = rest

    @pl.when(pl.program_id(1) == 0)
    def _():
        h = _rms(x_ref[...]) * g_ref[...] * (1.0 + sc_ref[0]) + sh_ref[0]
        hb = h.astype(BF16)
        h_scr[...] = hb
        os_ref[...] = _dot(hb, ws_ref[...])
        if transposed_small:
            ost_ref[...] = _dot_nt(wst_ref[...], hb)

    o_ref[...] = _dot(h_scr[...], w_ref[...]).astype(o_ref.dtype)


def _project(x2, g, shift, scale, w, ws, wst, seq, tm, tn, name):
    n, d = x2.shape
    wn = w.shape[1]
    tiles_per_batch = seq // tm
    bsz = shift.shape[0]
    in_specs = [pl.BlockSpec((tm, d), lambda i, j: (i, 0)),
                pl.BlockSpec((1, d), lambda i, j: (0, 0)),
                pl.BlockSpec((1, 1, d), lambda i, j: (i // tiles_per_batch, 0, 0)),
                pl.BlockSpec((1, 1, d), lambda i, j: (i // tiles_per_batch, 0, 0)),
                pl.BlockSpec((d, tn), lambda i, j: (0, j)),
                pl.BlockSpec((d, LANES), lambda i, j: (0, 0))]
    args = [x2, g.reshape(1, d), shift.reshape(bsz, 1, d), scale.reshape(bsz, 1, d), w, ws]
    out_specs = [pl.BlockSpec((tm, tn), lambda i, j: (i, j)),
                 pl.BlockSpec((tm, LANES), lambda i, j: (i, 0))]
    out_shape = [jax.ShapeDtypeStruct((n, wn), BF16), jax.ShapeDtypeStruct((n, LANES), F32)]
    if wst is not None:
        r = wst.shape[0]
        in_specs.append(pl.BlockSpec((r, d), lambda i, j: (0, 0)))
        args.append(wst)
        out_specs.append(pl.BlockSpec((r, tm), lambda i, j: (0, i)))
        out_shape.append(jax.ShapeDtypeStruct((r, n), F32))
    return pl.pallas_call(
        functools.partial(_proj_kernel, transposed_small=wst is not None),
        grid=(n // tm, wn // tn),
        in_specs=in_specs, out_specs=out_specs, out_shape=out_shape,
        scratch_shapes=[pltpu.VMEM((tm, d), BF16)],
        compiler_params=_cparams(("arbitrary", "arbitrary")),
        name=name,
    )(*args)


def _out_kernel(ya_ref, yb_ref, wa_ref, wb_ref, x_ref, gate_ref, g_ref, o_ref):
    acc = _dot(ya_ref[...], wa_ref[...]) + _dot(yb_ref[...], wb_ref[...])
    o_ref[...] = x_ref[...] + gate_ref[0] * (_rms(acc) * g_ref[...])


def _out_project(ya, ia, yb, ib, w, x2, gate, g, seq, tm, name):
    n, d = x2.shape
    kh = w.shape[0] // 2
    tiles_per_batch = seq // tm
    bsz = gate.shape[0]
    return pl.pallas_call(
        _out_kernel,
        grid=(n // tm,),
        in_specs=[pl.BlockSpec((tm, kh), lambda i: (i, ia)),
                  pl.BlockSpec((tm, kh), lambda i: (i, ib)),
                  pl.BlockSpec((kh, d), lambda i: (0, 0)),
                  pl.BlockSpec((kh, d), lambda i: (1, 0)),
                  pl.BlockSpec((tm, d), lambda i: (i, 0)),
                  pl.BlockSpec((1, 1, d), lambda i: (i // tiles_per_batch, 0, 0)),
                  pl.BlockSpec((1, d), lambda i: (0, 0))],
        out_specs=pl.BlockSpec((tm, d), lambda i: (i, 0)),
        out_shape=jax.ShapeDtypeStruct((n, d), F32),
        compiler_params=_cparams(("arbitrary",)),
        name=name,
    )(ya, yb, w, w, x2, gate.reshape(bsz, 1, d), g.reshape(1, d))


def _mlp_kernel(x_ref, g_ref, sh_ref, sc_ref, w1_ref, w2_ref, gate_ref, g2_ref, o_ref, h_scr, acc_scr):
    f = pl.program_id(1)

    @pl.when(f == 0)
    def _():
        h = _rms(x_ref[...]) * g_ref[...] * (1.0 + sc_ref[0]) + sh_ref[0]
        h_scr[...] = h.astype(BF16)
        acc_scr[...] = jnp.zeros_like(acc_scr)

    a = jnp.maximum(_dot(h_scr[...], w1_ref[...]), 0.0)
    acc_scr[...] += _dot((a * a).astype(BF16), w2_ref[...])

    @pl.when(f == pl.num_programs(1) - 1)
    def _():
        o_ref[...] = x_ref[...] + gate_ref[0] * (_rms(acc_scr[...]) * g2_ref[...])


def _mlp(x2, g, shift, scale, w1, w2, gate, g2, seq, tm, tf, name):
    n, d = x2.shape
    dff = w1.shape[1]
    tiles_per_batch = seq // tm
    bsz = gate.shape[0]
    vec = lambda a: a.reshape(bsz, 1, d)
    bspec = pl.BlockSpec((1, 1, d), lambda i, f: (i // tiles_per_batch, 0, 0))
    return pl.pallas_call(
        _mlp_kernel,
        grid=(n // tm, dff // tf),
        in_specs=[pl.BlockSpec((tm, d), lambda i, f: (i, 0)),
                  pl.BlockSpec((1, d), lambda i, f: (0, 0)),
                  bspec, bspec,
                  pl.BlockSpec((d, tf), lambda i, f: (0, f)),
                  pl.BlockSpec((tf, d), lambda i, f: (f, 0)),
                  bspec,
                  pl.BlockSpec((1, d), lambda i, f: (0, 0))],
        out_specs=pl.BlockSpec((tm, d), lambda i, f: (i, 0)),
        out_shape=jax.ShapeDtypeStruct((n, d), F32),
        scratch_shapes=[pltpu.VMEM((tm, d), BF16), pltpu.VMEM((tm, d), F32)],
        compiler_params=_cparams(("arbitrary", "arbitrary")),
        name=name,
    )(x2, g.reshape(1, d), vec(shift), vec(scale), w1, w2, vec(gate), g2.reshape(1, d))


def _conv_kernel(main_ref, prev_ref, next_ref, w_ref, b_ref, o_ref, scr, *, blocks_per_seq):
    t = main_ref.shape[0]
    i = pl.program_id(0) % blocks_per_seq
    keep_prev = (i > 0).astype(F32)
    keep_next = (i < blocks_per_seq - 1).astype(F32)
    scr[0:HALO, :] = prev_ref[...].astype(F32) * keep_prev
    scr[HALO:HALO + t, :] = main_ref[...].astype(F32)
    scr[HALO + t:2 * HALO + t, :] = next_ref[...].astype(F32) * keep_next
    pad = SSD_CONV // 2
    acc = b_ref[...] + w_ref[0:1, :] * scr[HALO - pad:HALO - pad + t, :]
    for k in range(1, SSD_CONV):
        acc = acc + w_ref[k:k + 1, :] * scr[HALO - pad + k:HALO - pad + k + t, :]
    o_ref[...] = (acc * _sigmoid(acc)).astype(o_ref.dtype)


def _conv_silu(main, col0, conv_w, conv_b, seq, tr, tc):
    n = main.shape[0]
    ch = conv_w.shape[1]
    cb0 = col0 // tc
    rpb = tr // HALO
    last_halo = n // HALO - 1
    return pl.pallas_call(
        functools.partial(_conv_kernel, blocks_per_seq=seq // tr),
        grid=(n // tr, ch // tc),
        in_specs=[pl.BlockSpec((tr, tc), lambda i, j: (i, cb0 + j)),
                  pl.BlockSpec((HALO, tc), lambda i, j: (jnp.maximum(i * rpb - 1, 0), cb0 + j)),
                  pl.BlockSpec((HALO, tc), lambda i, j: (jnp.minimum((i + 1) * rpb, last_halo), cb0 + j)),
                  pl.BlockSpec((SSD_CONV, tc), lambda i, j: (0, j)),
                  pl.BlockSpec((1, tc), lambda i, j: (0, j))],
        out_specs=pl.BlockSpec((tr, tc), lambda i, j: (i, j)),
        out_shape=jax.ShapeDtypeStruct((n, ch), BF16),
        scratch_shapes=[pltpu.VMEM((tr + 2 * HALO, tc), F32)],
        compiler_params=_cparams(("arbitrary", "arbitrary")),
        name="conv_silu",
    )(main, main, main, conv_w, conv_b.reshape(1, ch))


def _ssd_kernel(xs_ref, bc_ref, dt_ref, dtb_ref, alog_ref, *rest, reverse):
    if reverse:
        z_ref, yf_ref, dskip_ref, ng_ref, o_ref, st_scr = rest
    else:
        o_ref, st_scr = rest
    t = SSD_CHUNK
    hd = SSD_HEAD_DIM
    hpg = SSD_HEADS // SSD_GROUPS

    @pl.when(pl.program_id(1) == 0)
    def _():
        st_scr[...] = jnp.zeros_like(st_scr)

    xs = xs_ref[...].astype(F32)
    bc = bc_ref[...]

    dt = _softplus(dt_ref[...] + dtb_ref[...])
    adt = dt * (-jnp.exp(alog_ref[...]))
    row = _iota((t, t), 0)
    col = _iota((t, t), 1)
    tri = (col >= row) if reverse else (col <= row)
    cs = _dot_exact_lhs01(tri.astype(BF16), adt)
    cs_t = cs.T
    dt_t = dt.T
    base = SSD_HEADS if reverse else 0
    last = 0 if reverse else t - 1

    first_half = _iota((t, LANES), 1) < hd
    ys = []
    for g in range(SSD_GROUPS):
        bg = bc[:, g * SSD_STATE:(g + 1) * SSD_STATE]
        cg = bc[:, (SSD_GROUPS + g) * SSD_STATE:(SSD_GROUPS + g + 1) * SSD_STATE]
        cb = _dot_nt(cg, bg)
        bg_t = bg.astype(F32).T
        s_prev = st_scr[g]
        y_off = _dot(cg, s_prev.astype(BF16))
        s_new = []
        etot = []
        for pr in range(hpg // 2):
            pair = g * (hpg // 2) + pr
            xpair = xs[:, pair * LANES:(pair + 1) * LANES]
            cs_cols = []
            acc = None
            for sub in range(2):
                j = base + 2 * pair + sub
                cs_col = jnp.broadcast_to(cs[:, j:j + 1], (t, t))
                cs_cols.append(cs_col)
                dts = dt_t[j:j + 1, :]
                lmat = jnp.exp(jnp.where(tri, cs_col - cs_t[j:j + 1, :], NEG_BIG))
                m = (cb * lmat * dts).astype(BF16)
                keep = first_half if sub == 0 else jnp.logical_not(first_half)
                part = _dot(m, jnp.where(keep, xpair, 0.0).astype(BF16))
                acc = part if acc is None else acc + part
            cs_pair = jnp.where(first_half, cs_cols[0], cs_cols[1])
            tot_pair = cs_pair[last:last + 1, :]
            xd = (xpair * jnp.exp(tot_pair - cs_pair)).astype(BF16)
            sp = None
            for sub in range(2):
                j = base + 2 * pair + sub
                keep = first_half if sub == 0 else jnp.logical_not(first_half)
                part = _dot((bg_t * dt_t[j:j + 1, :]).astype(BF16), jnp.where(keep, xd, jnp.zeros((), BF16)))
                sp = part if sp is None else sp + part
            ys.append(acc + y_off[:, pr * LANES:(pr + 1) * LANES] * jnp.exp(cs_pair))
            s_new.append(sp)
            etot.append(jnp.exp(tot_pair))
        st_scr[g] = s_prev * jnp.concatenate(etot, axis=1) + jnp.concatenate(s_new, axis=1)
    y = jnp.concatenate(ys, axis=1)

    if reverse:
        y = y + yf_ref[...] + dskip_ref[...] * xs
        z = z_ref[...].astype(F32)
        y = y * (z * _sigmoid(z))
        o_ref[...] = (_rms(y) * ng_ref[...]).astype(o_ref.dtype)
    else:
        o_ref[...] = y


def _ssd_pass(conv, dtf, dtb, alog, bsz, seq, reverse, extra=None):
    n = conv.shape[0]
    t = SSD_CHUNK
    nc = seq // t
    bc_blk = SSD_WIDTH // SSD_BC

    def rows(b, c):
        return b * nc + (nc - 1 - c if reverse else c)

    in_specs = [pl.BlockSpec((t, SSD_WIDTH), lambda b, c: (rows(b, c), 0)),
                pl.BlockSpec((t, SSD_BC), lambda b, c: (rows(b, c), bc_blk)),
                pl.BlockSpec((t, LANES), lambda b, c: (rows(b, c), 0)),
                pl.BlockSpec((1, LANES), lambda b, c: (0, 0)),
                pl.BlockSpec((1, LANES), lambda b, c: (0, 0))]
    args = [conv, conv, dtf, dtb, alog]
    if reverse:
        main, yf, dskip, ng = extra
        in_specs += [pl.BlockSpec((t, SSD_WIDTH), lambda b, c: (rows(b, c), 0)),
                     pl.BlockSpec((t, SSD_WIDTH), lambda b, c: (rows(b, c), 0)),
                     pl.BlockSpec((1, SSD_WIDTH), lambda b, c: (0, 0)),
                     pl.BlockSpec((1, SSD_WIDTH), lambda b, c: (0, 0))]
        args += [main, yf, dskip, ng]
        out_dtype = BF16
    else:
        out_dtype = F32
    return pl.pallas_call(
        functools.partial(_ssd_kernel, reverse=reverse),
        grid=(bsz, nc),
        in_specs=in_specs,
        out_specs=pl.BlockSpec((t, SSD_WIDTH), lambda b, c: (rows(b, c), 0)),
        out_shape=jax.ShapeDtypeStruct((n, SSD_WIDTH), out_dtype),
        scratch_shapes=[pltpu.VMEM((SSD_GROUPS, SSD_STATE, SSD_WIDTH // SSD_GROUPS), F32)],
        compiler_params=_cparams(("arbitrary", "arbitrary")),
        name="ssd_bwd" if reverse else "ssd_fwd",
    )(*args)


def _na_kernel(q_ref, k_ref, v_ref, bias_ref, o_ref, s_scr, p_scr, l_scr, *, nrows):
    w = GRID_W
    band = NA_KH * w
    first = _iota((w, LANES), 1) < NA_HEAD_DIM
    zero = jnp.zeros((), BF16)

    def band_start(r):
        return jnp.clip(r - NA_KH // 2, 0, nrows - NA_KH)

    def scores(r, slot):
        rs = band_start(r)
        q = q_ref[pl.ds(pl.multiple_of(r * w, w), w), :] * jnp.asarray(NA_HEAD_DIM ** -0.5, BF16)
        qs = jnp.concatenate([jnp.where(first, q, zero), jnp.where(first, zero, q)], axis=0)
        kb = k_ref[pl.ds(pl.multiple_of(rs * w, w), band), :]
        s_scr[slot] = _dot_nt(qs, kb) + bias_ref[0, r - rs]

    def softmax(slot):
        s = s_scr[slot]
        p = jnp.exp(s - jnp.max(s, axis=-1, keepdims=True))
        l_scr[slot] = jnp.broadcast_to(jnp.sum(p, axis=-1, keepdims=True), (2 * w, LANES))
        p_scr[slot] = p.astype(BF16)

    def values(r, slot):
        rs = band_start(r)
        vb = v_ref[pl.ds(pl.multiple_of(rs * w, w), band), :]
        o = _dot(p_scr[slot], vb) / l_scr[slot]
        o_ref[pl.ds(pl.multiple_of(r * w, w), w), :] = jnp.where(first, o[0:w], o[w:2 * w]).astype(o_ref.dtype)

    s_scr[...] = jnp.zeros_like(s_scr)
    p_scr[...] = jnp.zeros_like(p_scr)
    l_scr[...] = jnp.ones_like(l_scr)

    def body(i2, carry):
        for slot in range(2):
            i = 2 * i2 + slot
            values(jnp.maximum(i - 2, 0), slot)
            softmax(1 - slot)
            scores(jnp.minimum(i, nrows - 1), slot)
        return carry

    lax.fori_loop(0, nrows // 2 + 1, body, 0)


def _na_bias(rpb):
    w = GRID_W
    h, nro, nrel = rpb.shape
    c = np.arange(w)
    kc = np.arange(w)
    col_start = np.clip(c - NA_KW // 2, 0, w - NA_KW)
    valid = (kc[None, :] >= col_start[:, None]) & (kc[None, :] < col_start[:, None] + NA_KW)
    co = kc[None, :] - c[:, None] + NA_KW - 1
    onehot = ((np.arange(nrel)[:, None, None] == co[None]) & valid[None]).astype(np.float32)
    toe = jnp.dot(rpb.astype(F32).reshape(h * nro, nrel), onehot.reshape(nrel, w * w),
                  precision=lax.Precision.HIGHEST).reshape(h, nro, w, w)
    toe = toe + np.where(valid, 0.0, NEG_BIG).astype(np.float32)
    slabs = [toe[:, NA_KH - 1 - d:2 * NA_KH - 1 - d] for d in range(NA_KH)]
    b = jnp.stack(slabs, axis=1).transpose(0, 1, 3, 2, 4)
    b = b.reshape(h // 2, 2, NA_KH, w, NA_KH * w).transpose(0, 2, 1, 3, 4)
    return b.reshape(h // 2, NA_KH, 2 * w, NA_KH * w)


def _na(main, bias, bsz, seq, q_blk):
    n = main.shape[0]
    nrows = seq // GRID_W
    assert nrows >= NA_KH
    pairs = NA_HEADS // 2
    return pl.pallas_call(
        functools.partial(_na_kernel, nrows=nrows),
        grid=(bsz, pairs),
        in_specs=[pl.BlockSpec((seq, LANES), lambda b, p: (b, q_blk + p)),
                  pl.BlockSpec((seq, LANES), lambda b, p: (b, q_blk + pairs + p)),
                  pl.BlockSpec((seq, LANES), lambda b, p: (b, q_blk + 2 * pairs + p)),
                  pl.BlockSpec((1, NA_KH, 2 * GRID_W, NA_KH * GRID_W), lambda b, p: (p, 0, 0, 0))],
        out_specs=pl.BlockSpec((seq, LANES), lambda b, p: (b, p)),
        out_shape=jax.ShapeDtypeStruct((n, NA_WIDTH), BF16),
        scratch_shapes=[pltpu.VMEM((2, 2 * GRID_W, NA_KH * GRID_W), F32),
                        pltpu.VMEM((2, 2 * GRID_W, NA_KH * GRID_W), BF16),
                        pltpu.VMEM((2, 2 * GRID_W, LANES), F32)],
        compiler_params=_cparams(("arbitrary", "arbitrary")),
        name="nbr_attn",
    )(main, main, main, bias)


def _mlstm_kernel(q_ref, k_ref, v_ref, og_ref, g_ref, gt_ref, gb_ref, gbt_ref, hg_ref, o_ref,
                  bc_scr, rs_scr, cp_scr, np_scr, mp_scr, c_st, n_st, m_st, *, nchunks):
    t = ML_CHUNK
    head = pl.program_id(1)
    scale = ML_DK ** -0.5
    row = _iota((t, t), 0)
    col = _iota((t, t), 1)
    tris = (col <= row, col >= row)
    lane1 = _iota((1, LANES), 1)
    sub1 = _iota((4 * ML_HEADS, 1), 0)

    def gate_col(gc, kind):
        return jnp.sum(jnp.where(lane1 == kind * ML_HEADS + head, gc, 0.0), axis=-1, keepdims=True)

    def gate_row(gr, kind):
        return jnp.sum(jnp.where(sub1 == kind * ML_HEADS + head, gr, 0.0), axis=0, keepdims=True)

    c_st[...] = jnp.zeros_like(c_st)
    n_st[...] = jnp.zeros_like(n_st)
    m_st[...] = jnp.zeros_like(m_st)

    def scan_step(i, carry):
        for d in range(2):
            c = nchunks - 1 - i if d else i
            r0 = pl.multiple_of(c * t, t)
            gc = g_ref[pl.ds(r0, t), :] + gb_ref[...]
            gr = gt_ref[:, pl.ds(r0, t)] + gbt_ref[...]
            ig_c = gate_col(gc, 2 * d)
            lf_c = _log_sigmoid(gate_col(gc, 2 * d + 1))
            b_c = _dot_exact_lhs01(tris[d].astype(BF16), jnp.broadcast_to(lf_c, (t, LANES)))
            g_tot = b_c[0:1, :] if d else b_c[t - 1:t, :]
            a_c = g_tot - b_c + ig_c
            m_loc = jnp.max(a_c, axis=0, keepdims=True)
            kw = k_ref[pl.ds(r0, t), :].astype(F32) * jnp.exp(a_c - m_loc)
            s_loc = _dot(kw.T.astype(BF16), v_ref[pl.ds(r0, t), :])
            n_loc = jnp.sum(kw, axis=0, keepdims=True)
            bc_scr[d, pl.ds(r0, t), :] = b_c
            rs_scr[d, c] = jnp.broadcast_to(gate_row(gr, 2 * d) - b_c.T[0:1, :], (SUBLANES, t))
            c_prev = c_st[d]
            n_prev = n_st[d]
            m_prev = m_st[d]
            cp_scr[d, c] = c_prev.astype(BF16)
            np_scr[d, c] = jnp.broadcast_to(n_prev, (SUBLANES, ML_DK))
            mp_scr[d, c] = jnp.broadcast_to(m_prev, (SUBLANES, LANES))
            m_new = jnp.maximum(g_tot + m_prev, m_loc)
            s_old = jnp.exp(g_tot + m_prev - m_new)
            s_new = jnp.exp(m_loc - m_new)
            c_st[d] = (jnp.concatenate([s_old, s_old], axis=1) * c_prev
                       + jnp.concatenate([s_new, s_new], axis=1) * s_loc)
            n_st[d] = s_old * n_prev + s_new * n_loc
            m_st[d] = m_new
        return carry

    lax.fori_loop(0, nchunks, scan_step, 0, unroll=2)

    def out_step(c, carry):
        r0 = pl.multiple_of(c * t, t)
        q = q_ref[pl.ds(r0, t), :]
        k = k_ref[pl.ds(r0, t), :]
        v = v_ref[pl.ds(r0, t), :]
        qk = _dot_nt(q, k)
        qf = q.astype(F32)
        hsum = None
        for d in range(2):
            b_c = bc_scr[d, pl.ds(r0, t), :]
            m_prev = mp_scr[d, c][0:1, :]
            b_ts = jnp.concatenate([b_c] * (t // LANES), axis=1)
            dm = jnp.where(tris[d], b_ts + rs_scr[d, c][0:1, :], NEG_BIG)
            m_inter = (b_c + m_prev)[:, 0:1]
            m_t = jnp.maximum(m_inter, jnp.max(dm, axis=-1, keepdims=True))
            sc = qk * (jnp.exp(dm - m_t) * scale)
            inter = jnp.exp(m_inter - m_t) * scale
            num = _dot(sc.astype(BF16), v) + inter * _dot(q, cp_scr[d, c])
            den = (jnp.sum(sc, axis=-1, keepdims=True)
                   + inter * jnp.sum(qf * np_scr[d, c][0:1, :], axis=-1, keepdims=True))
            hd = num / jnp.maximum(jnp.abs(den), jnp.exp(-m_t))
            hsum = hd if hsum is None else hsum + hd
        og = og_ref[pl.ds(r0, t), :].astype(F32)
        o_ref[pl.ds(r0, t), :] = (_sigmoid(og) * (_rms(hsum) * hg_ref[0])).astype(o_ref.dtype)
        return carry

    lax.fori_loop(0, nchunks, out_step, 0)


def _mlstm(main, gates, gates_t, gate_b, gate_bt, head_g, bsz, seq):
    n = main.shape[0]
    nc = seq // ML_CHUNK
    nh = ML_HEADS
    r = gates_t.shape[0]
    return pl.pallas_call(
        functools.partial(_mlstm_kernel, nchunks=nc),
        grid=(bsz, nh),
        in_specs=[pl.BlockSpec((seq, ML_DK), lambda b, h: (b, h)),
                  pl.BlockSpec((seq, ML_DK), lambda b, h: (b, nh + h)),
                  pl.BlockSpec((seq, ML_DV), lambda b, h: (b, nh + h)),
                  pl.BlockSpec((seq, ML_DV), lambda b, h: (b, 2 * nh + h)),
                  pl.BlockSpec((seq, LANES), lambda b, h: (b, 0)),
                  pl.BlockSpec((r, seq), lambda b, h: (0, b)),
                  pl.BlockSpec((1, LANES), lambda b, h: (0, 0)),
                  pl.BlockSpec((r, 1), lambda b, h: (0, 0)),
                  pl.BlockSpec((1, 1, ML_DV), lambda b, h: (h, 0, 0))],
        out_specs=pl.BlockSpec((seq, ML_DV), lambda b, h: (b, h)),
        out_shape=jax.ShapeDtypeStruct((n, ML_WIDTH), BF16),
        scratch_shapes=[pltpu.VMEM((2, seq, LANES), F32),
                        pltpu.VMEM((2, nc, SUBLANES, ML_CHUNK), F32),
                        pltpu.VMEM((2, nc, ML_DK, ML_DV), BF16),
                        pltpu.VMEM((2, nc, SUBLANES, ML_DK), F32),
                        pltpu.VMEM((2, nc, SUBLANES, LANES), F32),
                        pltpu.VMEM((2, ML_DK, ML_DV), F32),
                        pltpu.VMEM((2, 1, ML_DK), F32),
                        pltpu.VMEM((2, 1, LANES), F32)],
        compiler_params=_cparams(("arbitrary", "arbitrary")),
        name="mlstm",
    )(main, main, main, main, gates, gates_t, gate_b, gate_bt, head_g.reshape(nh, 1, ML_DV))


def _pad_cols(a, width):
    return jnp.pad(a, ((0, 0), (0, width - a.shape[1])))


def _ssd_na_layer(x2, mods, norm_g, w_in, conv_w, conv_b, dt_bias, a_log, d_skip, ssd_norm, rpb, w_out,
                  bsz, seq):
    sh1, sc1, g1 = mods
    s1 = SSD_WIDTH
    s2 = s1 + SSD_CONV_CH
    s3 = s2 + 2 * SSD_HEADS
    w_main = jnp.concatenate([w_in[:, :s2], w_in[:, s3:]], axis=1).astype(BF16)
    w_dt = _pad_cols(w_in[:, s2:s3], LANES).astype(BF16)
    main, dtf = _project(x2, norm_g[0], sh1, sc1, w_main, w_dt, None, seq, tm=min(1024, seq), tn=1408,
                         name="proj_ssd_na")

    conv = _conv_silu(main, s1, conv_w, conv_b, seq, tr=min(512, seq), tc=512)
    dtb = _pad_cols(dt_bias.reshape(1, 2 * SSD_HEADS), LANES)
    alog = _pad_cols(a_log.reshape(1, 2 * SSD_HEADS), LANES)
    dskip = jnp.repeat(d_skip, SSD_HEAD_DIM)[None, :]
    y_f = _ssd_pass(conv, dtf, dtb, alog, bsz, seq, reverse=False)
    y_ssd = _ssd_pass(conv, dtf, dtb, alog, bsz, seq, reverse=True,
                      extra=(main, y_f, dskip, ssd_norm[None, :]))

    q_blk = (s1 + SSD_CONV_CH) // LANES
    y_na = _na(main, _na_bias(rpb), bsz, seq, q_blk)
    return _out_project(y_ssd, 0, y_na, 0, w_out.astype(BF16), x2, g1, norm_g[1], seq, tm=min(512, seq),
                        name="out_ssd_na")


def _mlstm_layer(x2, mods, norm_g, w_in, gate_b, head_g, w_out, bsz, seq):
    sh1, sc1, g1 = mods
    wm = 2 * ML_QK + 2 * ML_WIDTH
    w_main = w_in[:, :wm].astype(BF16)
    w_g = w_in[:, wm:]
    main, gates, gates_t = _project(x2, norm_g[0], sh1, sc1, w_main, _pad_cols(w_g, LANES).astype(BF16),
                                    w_g.T.astype(BF16), seq, tm=min(1024, seq), tn=1536, name="proj_mlstm")
    gb = gate_b.reshape(1, 4 * ML_HEADS).astype(F32)
    y = _mlstm(main, gates, gates_t, _pad_cols(gb, LANES), gb.reshape(4 * ML_HEADS, 1), head_g, bsz, seq)
    return _out_project(y, 0, y, 1, w_out.astype(BF16), x2, g1, norm_g[1], seq, tm=min(512, seq),
                        name="out_mlstm")


def kernel(x, c, ada_w, ada_b, norm_g, mlp_w1, mlp_w2, ab_w_in, ab_conv_w, ab_conv_b, ab_dt_bias, ab_a_log,
           ab_d_skip, ab_ssd_norm, ab_rpb, ab_w_out, ml_w_in, ml_gate_b, ml_head_norm, ml_w_out):
    bsz, seq, d = x.shape
    depth = ada_w.shape[0]
    mod = _adaln(c, ada_w, ada_b)
    x2 = x.reshape(bsz * seq, d)
    for layer in range(depth):
        sh1, sc1, g1, sh2, sc2, g2 = [mod[layer, :, i * d:(i + 1) * d] for i in range(6)]
        j = layer // 2
        if layer % 2 == 0:
            x2 = _ssd_na_layer(x2, (sh1, sc1, g1), norm_g[layer], ab_w_in[j], ab_conv_w[j], ab_conv_b[j],
                               ab_dt_bias[j], ab_a_log[j], ab_d_skip[j], ab_ssd_norm[j], ab_rpb[j],
                               ab_w_out[j], bsz, seq)
        else:
            x2 = _mlstm_layer(x2, (sh1, sc1, g1), norm_g[layer], ml_w_in[j], ml_gate_b[j], ml_head_norm[j],
                              ml_w_out[j], bsz, seq)
        x2 = _mlp(x2, norm_g[layer, 2], sh2, sc2, mlp_w1[layer].astype(BF16), mlp_w2[layer].astype(BF16),
                  g2, norm_g[layer, 3], seq, tm=min(1024, seq), tf=1024, name="mlp%d" % layer)
    return x2.reshape(bsz, seq, d)
```

```python
import functools

import numpy as np
import jax
import jax.numpy as jnp
from jax import lax
from jax.experimental import pallas as pl
from jax.experimental.pallas import tpu as pltpu

F32 = jnp.float32
BF16 = jnp.bfloat16

NORM_EPS = 1e-6
GRID_W = 64

SSD_HEAD_DIM = 64
SSD_HEADS = 16
SSD_GROUPS = 2
SSD_STATE = 128
SSD_CONV = 5
SSD_CHUNK = 128
SSD_WIDTH = SSD_HEADS * SSD_HEAD_DIM
SSD_BC = 2 * SSD_GROUPS * SSD_STATE
SSD_CONV_CH = SSD_WIDTH + SSD_BC

NA_HEAD_DIM = 64
NA_HEADS = 16
NA_KH = 8
NA_KW = 16
NA_WIDTH = NA_HEADS * NA_HEAD_DIM

ML_HEADS = 8
ML_DV = 256
ML_DK = 128
ML_CHUNK = 256
ML_WIDTH = ML_HEADS * ML_DV
ML_QK = ML_HEADS * ML_DK

LANES = 128
SUBLANES = 8
HALO = 16
NEG_BIG = -1e30
VMEM_LIMIT = 56 * 1024 * 1024


def _cparams(sem):
    return pltpu.CompilerParams(dimension_semantics=sem, vmem_limit_bytes=VMEM_LIMIT)


def _dot(a, b):
    return jnp.dot(a, b, preferred_element_type=F32)


def _dot_nt(a, b):
    return lax.dot_general(a, b, (((1,), (1,)), ((), ())), preferred_element_type=F32)


def _split3(x):
    hi = x.astype(BF16)
    r1 = x - hi.astype(F32)
    mid = r1.astype(BF16)
    lo = (r1 - mid.astype(F32)).astype(BF16)
    return hi, mid, lo


def _dot_exact_lhs01(sel, x):
    hi, mid, lo = _split3(x)
    return _dot(sel, hi) + _dot(sel, mid) + _dot(sel, lo)


def _iota(shape, dim):
    return lax.broadcasted_iota(jnp.int32, shape, dim)


def _sigmoid(x):
    return 0.5 * (jnp.tanh(0.5 * x) + 1.0)


def _softplus(x):
    return jnp.maximum(x, 0.0) + jnp.log(1.0 + jnp.exp(-jnp.abs(x)))


def _log_sigmoid(x):
    return jnp.minimum(x, 0.0) - jnp.log(1.0 + jnp.exp(-jnp.abs(x)))


def _rms(x):
    return x * lax.rsqrt(jnp.mean(x * x, axis=-1, keepdims=True) + NORM_EPS)


def _mod_kernel(c_ref, w_ref, b_ref, o_ref):
    c = c_ref[...]
    cond = c * _sigmoid(c)
    o_ref[0] = jnp.dot(cond, w_ref[0], preferred_element_type=F32,
                       precision=lax.Precision.HIGHEST) + b_ref[0]


def _adaln(c, ada_w, ada_b):
    depth, d, d6 = ada_w.shape
    bsz = c.shape[0]
    tn = 1024
    return pl.pallas_call(
        _mod_kernel,
        grid=(depth, d6 // tn),
        in_specs=[pl.BlockSpec((bsz, d), lambda l, j: (0, 0)),
                  pl.BlockSpec((1, d, tn), lambda l, j: (l, 0, j)),
                  pl.BlockSpec((1, 1, tn), lambda l, j: (l, 0, j))],
        out_specs=pl.BlockSpec((1, bsz, tn), lambda l, j: (l, 0, j)),
        out_shape=jax.ShapeDtypeStruct((depth, bsz, d6), F32),
        compiler_params=_cparams(("arbitrary", "arbitrary")),
        name="adaln",
    )(c, ada_w, ada_b.reshape(depth, 1, d6))


def _proj_kernel(x_ref, g_ref, sh_ref, sc_ref, w_ref, ws_ref, o_ref, os_ref, h_scr, *, small_transposed):
    @pl.when(pl.program_id(1) == 0)
    def _():
        h = _rms(x_ref[...]) * g_ref[...] * (1.0 + sc_ref[0]) + sh_ref[0]
        hb = h.astype(BF16)
        h_scr[...] = hb
        if small_transposed:
            os_ref[...] = _dot_nt(ws_ref[...], hb)
        else:
            os_ref[...] = _dot(hb, ws_ref[...])

    o_ref[...] = _dot(h_scr[...], w_ref[...]).astype(o_ref.dtype)


def _project(x2, g, shift, scale, w, ws, small_transposed, seq, tm, tn, name):
    n, d = x2.shape
    wn = w.shape[1]
    tiles_per_batch = seq // tm
    bsz = shift.shape[0]
    if small_transposed:
        r = ws.shape[0]
        small_spec, small_shape = pl.BlockSpec((r, tm), lambda i, j: (0, i)), (r, n)
    else:
        small_spec, small_shape = pl.BlockSpec((tm, LANES), lambda i, j: (i, 0)), (n, LANES)
    return pl.pallas_call(
        functools.partial(_proj_kernel, small_transposed=small_transposed),
        grid=(n // tm, wn // tn),
        in_specs=[pl.BlockSpec((tm, d), lambda i, j: (i, 0)),
                  pl.BlockSpec((1, d), lambda i, j: (0, 0)),
                  pl.BlockSpec((1, 1, d), lambda i, j: (i // tiles_per_batch, 0, 0)),
                  pl.BlockSpec((1, 1, d), lambda i, j: (i // tiles_per_batch, 0, 0)),
                  pl.BlockSpec((d, tn), lambda i, j: (0, j)),
                  pl.BlockSpec(ws.shape, lambda i, j: (0, 0))],
        out_specs=[pl.BlockSpec((tm, tn), lambda i, j: (i, j)), small_spec],
        out_shape=[jax.ShapeDtypeStruct((n, wn), BF16), jax.ShapeDtypeStruct(small_shape, F32)],
        scratch_shapes=[pltpu.VMEM((tm, d), BF16)],
        compiler_params=_cparams(("arbitrary", "arbitrary")),
        name=name,
    )(x2, g.reshape(1, d), shift.reshape(bsz, 1, d), scale.reshape(bsz, 1, d), w, ws)


def _out_kernel(ya_ref, yb_ref, wa_ref, wb_ref, x_ref, gate_ref, g_ref, o_ref):
    acc = _dot(ya_ref[...], wa_ref[...]) + _dot(yb_ref[...], wb_ref[...])
    o_ref[...] = x_ref[...] + gate_ref[0] * (_rms(acc) * g_ref[...])


def _out_project(ya, ia, yb, ib, w, x2, gate, g, seq, tm, name):
    n, d = x2.shape
    kh = w.shape[0] // 2
    tiles_per_batch = seq // tm
    bsz = gate.shape[0]
    return pl.pallas_call(
        _out_kernel,
        grid=(n // tm,),
        in_specs=[pl.BlockSpec((tm, kh), lambda i: (i, ia)),
                  pl.BlockSpec((tm, kh), lambda i: (i, ib)),
                  pl.BlockSpec((kh, d), lambda i: (0, 0)),
                  pl.BlockSpec((kh, d), lambda i: (1, 0)),
                  pl.BlockSpec((tm, d), lambda i: (i, 0)),
                  pl.BlockSpec((1, 1, d), lambda i: (i // tiles_per_batch, 0, 0)),
                  pl.BlockSpec((1, d), lambda i: (0, 0))],
        out_specs=pl.BlockSpec((tm, d), lambda i: (i, 0)),
        out_shape=jax.ShapeDtypeStruct((n, d), F32),
        compiler_params=_cparams(("arbitrary",)),
        name=name,
    )(ya, yb, w, w, x2, gate.reshape(bsz, 1, d), g.reshape(1, d))


def _mlp_kernel(x_ref, g_ref, sh_ref, sc_ref, w1_ref, w2_ref, gate_ref, g2_ref, o_ref, h_scr, acc_scr):
    f = pl.program_id(1)

    @pl.when(f == 0)
    def _():
        h = _rms(x_ref[...]) * g_ref[...] * (1.0 + sc_ref[0]) + sh_ref[0]
        h_scr[...] = h.astype(BF16)
        acc_scr[...] = jnp.zeros_like(acc_scr)

    a = jnp.maximum(_dot(h_scr[...], w1_ref[...]), 0.0)
    acc_scr[...] += _dot((a * a).astype(BF16), w2_ref[...])

    @pl.when(f == pl.num_programs(1) - 1)
    def _():
        o_ref[...] = x_ref[...] + gate_ref[0] * (_rms(acc_scr[...]) * g2_ref[...])


def _mlp(x2, g, shift, scale, w1, w2, gate, g2, seq, tm, tf, name):
    n, d = x2.shape
    dff = w1.shape[1]
    tiles_per_batch = seq // tm
    bsz = gate.shape[0]
    vec = lambda a: a.reshape(bsz, 1, d)
    bspec = pl.BlockSpec((1, 1, d), lambda i, f: (i // tiles_per_batch, 0, 0))
    return pl.pallas_call(
        _mlp_kernel,
        grid=(n // tm, dff // tf),
        in_specs=[pl.BlockSpec((tm, d), lambda i, f: (i, 0)),
                  pl.BlockSpec((1, d), lambda i, f: (0, 0)),
                  bspec, bspec,
                  pl.BlockSpec((d, tf), lambda i, f: (0, f)),
                  pl.BlockSpec((tf, d), lambda i, f: (f, 0)),
                  bspec,
                  pl.BlockSpec((1, d), lambda i, f: (0, 0))],
        out_specs=pl.BlockSpec((tm, d), lambda i, f: (i, 0)),
        out_shape=jax.ShapeDtypeStruct((n, d), F32),
        scratch_shapes=[pltpu.VMEM((tm, d), BF16), pltpu.VMEM((tm, d), F32)],
        compiler_params=_cparams(("arbitrary", "arbitrary")),
        name=name,
    )(x2, g.reshape(1, d), vec(shift), vec(scale), w1, w2, vec(gate), g2.reshape(1, d))


def _conv_kernel(main_ref, prev_ref, next_ref, w_ref, b_ref, o_ref, scr, *, blocks_per_seq):
    t = main_ref.shape[0]
    i = pl.program_id(0) % blocks_per_seq
    keep_prev = (i > 0).astype(F32)
    keep_next = (i < blocks_per_seq - 1).astype(F32)
    scr[0:HALO, :] = prev_ref[...].astype(F32) * keep_prev
    scr[HALO:HALO + t, :] = main_ref[...].astype(F32)
    scr[HALO + t:2 * HALO + t, :] = next_ref[...].astype(F32) * keep_next
    pad = SSD_CONV // 2
    acc = b_ref[...] + w_ref[0:1, :] * scr[HALO - pad:HALO - pad + t, :]
    for k in range(1, SSD_CONV):
        acc = acc + w_ref[k:k + 1, :] * scr[HALO - pad + k:HALO - pad + k + t, :]
    o_ref[...] = (acc * _sigmoid(acc)).astype(o_ref.dtype)


def _conv_silu(main, col0, conv_w, conv_b, seq, tr, tc):
    n = main.shape[0]
    ch = conv_w.shape[1]
    cb0 = col0 // tc
    rpb = tr // HALO
    last_halo = n // HALO - 1
    return pl.pallas_call(
        functools.partial(_conv_kernel, blocks_per_seq=seq // tr),
        grid=(n // tr, ch // tc),
        in_specs=[pl.BlockSpec((tr, tc), lambda i, j: (i, cb0 + j)),
                  pl.BlockSpec((HALO, tc), lambda i, j: (jnp.maximum(i * rpb - 1, 0), cb0 + j)),
                  pl.BlockSpec((HALO, tc), lambda i, j: (jnp.minimum((i + 1) * rpb, last_halo), cb0 + j)),
                  pl.BlockSpec((SSD_CONV, tc), lambda i, j: (0, j)),
                  pl.BlockSpec((1, tc), lambda i, j: (0, j))],
        out_specs=pl.BlockSpec((tr, tc), lambda i, j: (i, j)),
        out_shape=jax.ShapeDtypeStruct((n, ch), BF16),
        scratch_shapes=[pltpu.VMEM((tr + 2 * HALO, tc), F32)],
        compiler_params=_cparams(("arbitrary", "arbitrary")),
        name="conv_silu",
    )(main, main, main, conv_w, conv_b.reshape(1, ch))


def _ssd_kernel(xs_ref, bc_ref, dt_ref, dtb_ref, alog_ref, *rest, reverse):
    if reverse:
        z_ref, yf_ref, dskip_ref, ng_ref, o_ref, st_scr = rest
    else:
        o_ref, st_scr = rest
    t = SSD_CHUNK
    hd = SSD_HEAD_DIM
    hpg = SSD_HEADS // SSD_GROUPS

    @pl.when(pl.program_id(1) == 0)
    def _():
        st_scr[...] = jnp.zeros_like(st_scr)

    xs = xs_ref[...].astype(F32)
    bc = bc_ref[...]

    dt = _softplus(dt_ref[...] + dtb_ref[...])
    adt = dt * (-jnp.exp(alog_ref[...]))
    row = _iota((t, t), 0)
    col = _iota((t, t), 1)
    tri = (col >= row) if reverse else (col <= row)
    cs = _dot_exact_lhs01(tri.astype(BF16), adt)
    cs_t = cs.T
    dt_t = dt.T
    base = SSD_HEADS if reverse else 0
    last = 0 if reverse else t - 1

    first_half = _iota((t, LANES), 1) < hd
    ys = []
    for g in range(SSD_GROUPS):
        bg = bc[:, g * SSD_STATE:(g + 1) * SSD_STATE]
        cg = bc[:, (SSD_GROUPS + g) * SSD_STATE:(SSD_GROUPS + g + 1) * SSD_STATE]
        cb = _dot_nt(cg, bg)
        bg_t = bg.astype(F32).T
        s_prev = st_scr[g]
        y_off = _dot(cg, s_prev.astype(BF16))
        s_new = []
        etot = []
        for pr in range(hpg // 2):
            pair = g * (hpg // 2) + pr
            xpair = xs[:, pair * LANES:(pair + 1) * LANES]
            cs_cols = []
            acc = None
            for sub in range(2):
                j = base + 2 * pair + sub
                cs_col = jnp.broadcast_to(cs[:, j:j + 1], (t, t))
                cs_cols.append(cs_col)
                dts = dt_t[j:j + 1, :]
                lmat = jnp.exp(jnp.where(tri, cs_col - cs_t[j:j + 1, :], NEG_BIG))
                m = (cb * lmat * dts).astype(BF16)
                keep = first_half if sub == 0 else jnp.logical_not(first_half)
                part = _dot(m, jnp.where(keep, xpair, 0.0).astype(BF16))
                acc = part if acc is None else acc + part
            cs_pair = jnp.where(first_half, cs_cols[0], cs_cols[1])
            tot_pair = cs_pair[last:last + 1, :]
            xd = (xpair * jnp.exp(tot_pair - cs_pair)).astype(BF16)
            sp = None
            for sub in range(2):
                j = base + 2 * pair + sub
                keep = first_half if sub == 0 else jnp.logical_not(first_half)
                part = _dot((bg_t * dt_t[j:j + 1, :]).astype(BF16), jnp.where(keep, xd, jnp.zeros((), BF16)))
                sp = part if sp is None else sp + part
            ys.append(acc + y_off[:, pr * LANES:(pr + 1) * LANES] * jnp.exp(cs_pair))
            s_new.append(sp)
            etot.append(jnp.exp(tot_pair))
        st_scr[g] = s_prev * jnp.concatenate(etot, axis=1) + jnp.concatenate(s_new, axis=1)
    y = jnp.concatenate(ys, axis=1)

    if reverse:
        y = y + yf_ref[...] + dskip_ref[...] * xs
        z = z_ref[...].astype(F32)
        y = y * (z * _sigmoid(z))
        o_ref[...] = (_rms(y) * ng_ref[...]).astype(o_ref.dtype)
    else:
        o_ref[...] = y


def _ssd_pass(conv, dtf, dtb, alog, bsz, seq, reverse, extra=None):
    n = conv.shape[0]
    t = SSD_CHUNK
    nc = seq // t
    bc_blk = SSD_WIDTH // SSD_BC

    def rows(b, c):
        return b * nc + (nc - 1 - c if reverse else c)

    in_specs = [pl.BlockSpec((t, SSD_WIDTH), lambda b, c: (rows(b, c), 0)),
                pl.BlockSpec((t, SSD_BC), lambda b, c: (rows(b, c), bc_blk)),
                pl.BlockSpec((t, LANES), lambda b, c: (rows(b, c), 0)),
                pl.BlockSpec((1, LANES), lambda b, c: (0, 0)),
                pl.BlockSpec((1, LANES), lambda b, c: (0, 0))]
    args = [conv, conv, dtf, dtb, alog]
    if reverse:
        main, yf, dskip, ng = extra
        in_specs += [pl.BlockSpec((t, SSD_WIDTH), lambda b, c: (rows(b, c), 0)),
                     pl.BlockSpec((t, SSD_WIDTH), lambda b, c: (rows(b, c), 0)),
                     pl.BlockSpec((1, SSD_WIDTH), lambda b, c: (0, 0)),
                     pl.BlockSpec((1, SSD_WIDTH), lambda b, c: (0, 0))]
        args += [main, yf, dskip, ng]
        out_dtype = BF16
    else:
        out_dtype = F32
    return pl.pallas_call(
        functools.partial(_ssd_kernel, reverse=reverse),
        grid=(bsz, nc),
        in_specs=in_specs,
        out_specs=pl.BlockSpec((t, SSD_WIDTH), lambda b, c: (rows(b, c), 0)),
        out_shape=jax.ShapeDtypeStruct((n, SSD_WIDTH), out_dtype),
        scratch_shapes=[pltpu.VMEM((SSD_GROUPS, SSD_STATE, SSD_WIDTH // SSD_GROUPS), F32)],
        compiler_params=_cparams(("arbitrary", "arbitrary")),
        name="ssd_bwd" if reverse else "ssd_fwd",
    )(*args)


def _na_kernel(q_ref, k_ref, v_ref, bias_ref, o_ref, s_scr, p_scr, l_scr, *, nrows):
    w = GRID_W
    band = NA_KH * w
    first = _iota((w, LANES), 1) < NA_HEAD_DIM
    zero = jnp.zeros((), BF16)

    def band_start(r):
        return jnp.clip(r - NA_KH // 2, 0, nrows - NA_KH)

    def scores(r, slot):
        rs = band_start(r)
        q = q_ref[pl.ds(pl.multiple_of(r * w, w), w), :] * jnp.asarray(NA_HEAD_DIM ** -0.5, BF16)
        qs = jnp.concatenate([jnp.where(first, q, zero), jnp.where(first, zero, q)], axis=0)
        kb = k_ref[pl.ds(pl.multiple_of(rs * w, w), band), :]
        j0 = NA_KH - 1 - (r - rs)
        bias = jnp.concatenate([bias_ref[0, j0 + 2 * a] for a in range(NA_KH // 2)], axis=1)
        s_scr[slot] = _dot_nt(qs, kb) + bias

    def softmax(slot):
        s = s_scr[slot]
        p = jnp.exp(s - jnp.max(s, axis=-1, keepdims=True))
        l_scr[slot] = jnp.broadcast_to(jnp.sum(p, axis=-1, keepdims=True), (2 * w, LANES))
        p_scr[slot] = p.astype(BF16)

    def values(r, slot):
        rs = band_start(r)
        vb = v_ref[pl.ds(pl.multiple_of(rs * w, w), band), :]
        o = _dot(p_scr[slot], vb) / l_scr[slot]
        o_ref[pl.ds(pl.multiple_of(r * w, w), w), :] = jnp.where(first, o[0:w], o[w:2 * w]).astype(o_ref.dtype)

    s_scr[...] = jnp.zeros_like(s_scr)
    p_scr[...] = jnp.zeros_like(p_scr)
    l_scr[...] = jnp.ones_like(l_scr)

    def body(i2, carry):
        for slot in range(2):
            i = 2 * i2 + slot
            values(jnp.maximum(i - 2, 0), slot)
            softmax(1 - slot)
            scores(jnp.minimum(i, nrows - 1), slot)
        return carry

    lax.fori_loop(0, nrows // 2 + 1, body, 0)


def _na_bias(rpb):
    w = GRID_W
    h, nro, nrel = rpb.shape
    c = np.arange(w)
    kc = np.arange(w)
    col_start = np.clip(c - NA_KW // 2, 0, w - NA_KW)
    valid = (kc[None, :] >= col_start[:, None]) & (kc[None, :] < col_start[:, None] + NA_KW)
    co = kc[None, :] - c[:, None] + NA_KW - 1
    onehot = ((np.arange(nrel)[:, None, None] == co[None]) & valid[None]).astype(np.float32)
    rows = rpb.astype(F32).reshape(h // 2, 2, nro, nrel).transpose(0, 2, 1, 3)
    toe = jnp.dot(rows.reshape(h * nro, nrel), onehot.reshape(nrel, w * w),
                  precision=lax.Precision.HIGHEST).reshape(h // 2, nro, 2, w, w)
    toe = toe + np.where(valid, 0.0, NEG_BIG).astype(np.float32)
    pairs = jnp.concatenate([toe[:, :nro - 1], toe[:, 1:]], axis=-1)
    return pairs.reshape(h // 2, nro - 1, 2 * w, 2 * w)


def _na(main, bias, bsz, seq, q_blk):
    n = main.shape[0]
    nrows = seq // GRID_W
    assert nrows >= NA_KH
    pairs = NA_HEADS // 2
    return pl.pallas_call(
        functools.partial(_na_kernel, nrows=nrows),
        grid=(bsz, pairs),
        in_specs=[pl.BlockSpec((seq, LANES), lambda b, p: (b, q_blk + p)),
                  pl.BlockSpec((seq, LANES), lambda b, p: (b, q_blk + pairs + p)),
                  pl.BlockSpec((seq, LANES), lambda b, p: (b, q_blk + 2 * pairs + p)),
                  pl.BlockSpec((1,) + bias.shape[1:], lambda b, p: (p, 0, 0, 0))],
        out_specs=pl.BlockSpec((seq, LANES), lambda b, p: (b, p)),
        out_shape=jax.ShapeDtypeStruct((n, NA_WIDTH), BF16),
        scratch_shapes=[pltpu.VMEM((2, 2 * GRID_W, NA_KH * GRID_W), F32),
                        pltpu.VMEM((2, 2 * GRID_W, NA_KH * GRID_W), BF16),
                        pltpu.VMEM((2, 2 * GRID_W, LANES), F32)],
        compiler_params=_cparams(("arbitrary", "arbitrary")),
        name="nbr_attn",
    )(main, main, main, bias)


def _mlstm_kernel(q_ref, k_ref, v_ref, og_ref, gt_ref, gbt_ref, hg_ref, o_ref,
                  rows_scr, cp_scr, c_st, n_st, m_st, *, nchunks):
    t = ML_CHUNK
    head = pl.program_id(1)
    scale = ML_DK ** -0.5
    row = _iota((t, t), 0)
    col = _iota((t, t), 1)
    tris = (col <= row, col >= row)
    eye = col == row
    lane_t = _iota((SUBLANES, t), 1)
    sub1 = _iota((4 * ML_HEADS, 1), 0)

    def gate_row(gr, kind):
        sel = jnp.sum(jnp.where(sub1 == kind * ML_HEADS + head, gr, 0.0), axis=0, keepdims=True)
        return jnp.broadcast_to(sel, (SUBLANES, t))

    def cumsum_row(x, reverse):
        sh = 1
        while sh < t:
            if reverse:
                x = x + jnp.where(lane_t < t - sh, pltpu.roll(x, t - sh, axis=1), 0.0)
            else:
                x = x + jnp.where(lane_t >= sh, pltpu.roll(x, sh, axis=1), 0.0)
            sh *= 2
        return x

    c_st[...] = jnp.zeros_like(c_st)
    n_st[...] = jnp.zeros_like(n_st)
    m_st[...] = jnp.zeros_like(m_st)

    def scan_step(i, carry):
        for d in range(2):
            c = nchunks - 1 - i if d else i
            r0 = pl.multiple_of(c * t, t)
            gr = gt_ref[:, pl.ds(r0, t)] + gbt_ref[...]
            ig = gate_row(gr, 2 * d)
            b = cumsum_row(_log_sigmoid(gate_row(gr, 2 * d + 1)), bool(d))
            g_tot = b[:, 0:1] if d else b[:, t - 1:t]
            a = g_tot - b + ig
            m_loc = jnp.max(a, axis=-1, keepdims=True)
            w = jnp.exp(a - m_loc)[0:1, :]
            kw_t = k_ref[pl.ds(r0, t), :].astype(F32).T * w
            s_loc = _dot(kw_t.astype(BF16), v_ref[pl.ds(r0, t), :])
            n_loc = jnp.sum(kw_t, axis=-1, keepdims=True)
            c_prev = c_st[d]
            n_prev = n_st[d]
            m_prev = m_st[d]
            cp_scr[d, c] = jnp.concatenate([c_prev, n_prev], axis=1).astype(BF16)
            rows_scr[d, c, 0] = b
            rows_scr[d, c, 1] = ig - b
            rows_scr[d, c, 2] = jnp.concatenate([m_prev] * (t // LANES), axis=1)
            gt8 = jnp.broadcast_to(g_tot, (SUBLANES, LANES))
            ml8 = jnp.broadcast_to(m_loc, (SUBLANES, LANES))
            m_new = jnp.maximum(gt8 + m_prev, ml8)
            s_old = jnp.exp(gt8 + m_prev - m_new)[0:1, :]
            s_new = jnp.exp(ml8 - m_new)[0:1, :]
            c_st[d] = (jnp.concatenate([s_old, s_old], axis=1) * c_prev
                       + jnp.concatenate([s_new, s_new], axis=1) * s_loc)
            n_st[d] = s_old * n_prev + s_new * n_loc
            m_st[d] = m_new
        return carry

    lax.fori_loop(0, nchunks, scan_step, 0, unroll=2)

    def out_step(c, carry):
        r0 = pl.multiple_of(c * t, t)
        q = q_ref[pl.ds(r0, t), :]
        v = v_ref[pl.ds(r0, t), :]
        qk = _dot_nt(q, k_ref[pl.ds(r0, t), :]) * scale
        hsum = None
        for d in range(2):
            b_row = rows_scr[d, c, 0][0:1, :]
            rs_row = rows_scr[d, c, 1][0:1, :]
            m_prev = rows_scr[d, c, 2][0:1, 0:1]
            b_col = jnp.sum(jnp.where(eye, b_row, 0.0), axis=-1, keepdims=True)
            dm = jnp.where(tris[d], b_col + rs_row, NEG_BIG)
            m_inter = b_col + m_prev
            m_t = jnp.maximum(m_inter, jnp.max(dm, axis=-1, keepdims=True))
            sc = qk * jnp.exp(dm - m_t)
            inter = jnp.exp(m_inter - m_t) * scale
            ext = _dot(q, cp_scr[d, c])
            num = _dot(sc.astype(BF16), v) + inter * ext[:, 0:ML_DV]
            den = jnp.sum(sc, axis=-1, keepdims=True) + inter * ext[:, ML_DV:ML_DV + 1]
            hd = num / jnp.maximum(jnp.abs(den), jnp.exp(-m_t))
            hsum = hd if hsum is None else hsum + hd
        og = og_ref[pl.ds(r0, t), :].astype(F32)
        o_ref[pl.ds(r0, t), :] = (_sigmoid(og) * (_rms(hsum) * hg_ref[0])).astype(o_ref.dtype)
        return carry

    lax.fori_loop(0, nchunks, out_step, 0)


def _mlstm(main, gates_t, gate_bt, head_g, bsz, seq):
    n = main.shape[0]
    nc = seq // ML_CHUNK
    nh = ML_HEADS
    r = gates_t.shape[0]
    return pl.pallas_call(
        functools.partial(_mlstm_kernel, nchunks=nc),
        grid=(bsz, nh),
        in_specs=[pl.BlockSpec((seq, ML_DK), lambda b, h: (b, h)),
                  pl.BlockSpec((seq, ML_DK), lambda b, h: (b, nh + h)),
                  pl.BlockSpec((seq, ML_DV), lambda b, h: (b, nh + h)),
                  pl.BlockSpec((seq, ML_DV), lambda b, h: (b, 2 * nh + h)),
                  pl.BlockSpec((r, seq), lambda b, h: (0, b)),
                  pl.BlockSpec((r, 1), lambda b, h: (0, 0)),
                  pl.BlockSpec((1, 1, ML_DV), lambda b, h: (h, 0, 0))],
        out_specs=pl.BlockSpec((seq, ML_DV), lambda b, h: (b, h)),
        out_shape=jax.ShapeDtypeStruct((n, ML_WIDTH), BF16),
        scratch_shapes=[pltpu.VMEM((2, nc, 3, SUBLANES, ML_CHUNK), F32),
                        pltpu.VMEM((2, nc, ML_DK, ML_DV + LANES), BF16),
                        pltpu.VMEM((2, ML_DK, ML_DV), F32),
                        pltpu.VMEM((2, ML_DK, LANES), F32),
                        pltpu.VMEM((2, SUBLANES, LANES), F32)],
        compiler_params=_cparams(("arbitrary", "arbitrary")),
        name="mlstm",
    )(main, main, main, main, gates_t, gate_bt, head_g.reshape(nh, 1, ML_DV))


def _pad_cols(a, width):
    return jnp.pad(a, ((0, 0), (0, width - a.shape[1])))


def _ssd_na_layer(x2, mods, norm_g, w_in, conv_w, conv_b, dt_bias, a_log, d_skip, ssd_norm, rpb, w_out,
                  bsz, seq):
    sh1, sc1, g1 = mods
    s1 = SSD_WIDTH
    s2 = s1 + SSD_CONV_CH
    s3 = s2 + 2 * SSD_HEADS
    w_main = jnp.concatenate([w_in[:, :s2], w_in[:, s3:]], axis=1).astype(BF16)
    w_dt = _pad_cols(w_in[:, s2:s3], LANES).astype(BF16)
    main, dtf = _project(x2, norm_g[0], sh1, sc1, w_main, w_dt, False, seq, tm=min(1024, seq), tn=1408,
                         name="proj_ssd_na")

    conv = _conv_silu(main, s1, conv_w, conv_b, seq, tr=min(512, seq), tc=512)
    dtb = _pad_cols(dt_bias.reshape(1, 2 * SSD_HEADS), LANES)
    alog = _pad_cols(a_log.reshape(1, 2 * SSD_HEADS), LANES)
    dskip = jnp.repeat(d_skip, SSD_HEAD_DIM)[None, :]
    y_f = _ssd_pass(conv, dtf, dtb, alog, bsz, seq, reverse=False)
    y_ssd = _ssd_pass(conv, dtf, dtb, alog, bsz, seq, reverse=True,
                      extra=(main, y_f, dskip, ssd_norm[None, :]))

    q_blk = (s1 + SSD_CONV_CH) // LANES
    y_na = _na(main, _na_bias(rpb), bsz, seq, q_blk)
    return _out_project(y_ssd, 0, y_na, 0, w_out.astype(BF16), x2, g1, norm_g[1], seq, tm=min(512, seq),
                        name="out_ssd_na")


def _mlstm_layer(x2, mods, norm_g, w_in, gate_b, head_g, w_out, bsz, seq):
    sh1, sc1, g1 = mods
    wm = 2 * ML_QK + 2 * ML_WIDTH
    w_main = w_in[:, :wm].astype(BF16)
    main, gates_t = _project(x2, norm_g[0], sh1, sc1, w_main, w_in[:, wm:].T.astype(BF16), True, seq,
                             tm=min(1024, seq), tn=1536, name="proj_mlstm")
    y = _mlstm(main, gates_t, gate_b.reshape(4 * ML_HEADS, 1).astype(F32), head_g, bsz, seq)
    return _out_project(y, 0, y, 1, w_out.astype(BF16), x2, g1, norm_g[1], seq, tm=min(512, seq),
                        name="out_mlstm")


def kernel(x, c, ada_w, ada_b, norm_g, mlp_w1, mlp_w2, ab_w_in, ab_conv_w, ab_conv_b, ab_dt_bias, ab_a_log,
           ab_d_skip, ab_ssd_norm, ab_rpb, ab_w_out, ml_w_in, ml_gate_b, ml_head_norm, ml_w_out):
    bsz, seq, d = x.shape
    depth = ada_w.shape[0]
    mod = _adaln(c, ada_w, ada_b)
    x2 = x.reshape(bsz * seq, d)
    for layer in range(depth):
        sh1, sc1, g1, sh2, sc2, g2 = [mod[layer, :, i * d:(i + 1) * d] for i in range(6)]
        j = layer // 2
        if layer % 2 == 0:
            x2 = _ssd_na_layer(x2, (sh1, sc1, g1), norm_g[layer], ab_w_in[j], ab_conv_w[j], ab_conv_b[j],
                               ab_dt_bias[j], ab_a_log[j], ab_d_skip[j], ab_ssd_norm[j], ab_rpb[j],
                               ab_w_out[j], bsz, seq)
        else:
            x2 = _mlstm_layer(x2, (sh1, sc1, g1), norm_g[layer], ml_w_in[j], ml_gate_b[j], ml_head_norm[j],
                              ml_w_out[j], bsz, seq)
        x2 = _mlp(x2, norm_g[layer, 2], sh2, sc2, mlp_w1[layer].astype(BF16), mlp_w2[layer].astype(BF16),
                  g2, norm_g[layer, 3], seq, tm=min(1024, seq), tf=1024, name="mlp%d" % layer)
    return x2.reshape(bsz, seq, d)
```

```python
import functools

import numpy as np
import jax
import jax.numpy as jnp
from jax import lax
from jax.experimental import pallas as pl
from jax.experimental.pallas import tpu as pltpu

F32 = jnp.float32
BF16 = jnp.bfloat16

NORM_EPS = 1e-6
GRID_W = 64

SSD_HEAD_DIM = 64
SSD_HEADS = 16
SSD_GROUPS = 2
SSD_STATE = 128
SSD_CONV = 5
SSD_CHUNK = 128
SSD_WIDTH = SSD_HEADS * SSD_HEAD_DIM
SSD_BC = 2 * SSD_GROUPS * SSD_STATE
SSD_CONV_CH = SSD_WIDTH + SSD_BC

NA_HEAD_DIM = 64
NA_HEADS = 16
NA_KH = 8
NA_KW = 16
NA_WIDTH = NA_HEADS * NA_HEAD_DIM

ML_HEADS = 8
ML_DV = 256
ML_DK = 128
ML_CHUNK = 256
ML_WIDTH = ML_HEADS * ML_DV
ML_QK = ML_HEADS * ML_DK

LANES = 128
SUBLANES = 8
HALO = 16
NEG_BIG = -1e30
VMEM_LIMIT = 56 * 1024 * 1024


def _cparams(sem):
    return pltpu.CompilerParams(dimension_semantics=sem, vmem_limit_bytes=VMEM_LIMIT)


def _dot(a, b):
    return jnp.dot(a, b, preferred_element_type=F32)


def _dot_nt(a, b):
    return lax.dot_general(a, b, (((1,), (1,)), ((), ())), preferred_element_type=F32)


def _split3(x):
    hi = x.astype(BF16)
    r1 = x - hi.astype(F32)
    mid = r1.astype(BF16)
    lo = (r1 - mid.astype(F32)).astype(BF16)
    return hi, mid, lo


def _dot_exact_lhs01(sel, x):
    hi, mid, lo = _split3(x)
    return _dot(sel, hi) + _dot(sel, mid) + _dot(sel, lo)


def _iota(shape, dim):
    return lax.broadcasted_iota(jnp.int32, shape, dim)


def _sigmoid(x):
    return 0.5 * (jnp.tanh(0.5 * x) + 1.0)


def _softplus(x):
    return jnp.maximum(x, 0.0) + jnp.log(1.0 + jnp.exp(-jnp.abs(x)))


def _log_sigmoid(x):
    return jnp.minimum(x, 0.0) - jnp.log(1.0 + jnp.exp(-jnp.abs(x)))


def _rms(x):
    return x * lax.rsqrt(jnp.mean(x * x, axis=-1, keepdims=True) + NORM_EPS)


def _mod_kernel(c_ref, w_ref, b_ref, o_ref):
    c = c_ref[...]
    cond = c * _sigmoid(c)
    o_ref[0] = jnp.dot(cond, w_ref[0], preferred_element_type=F32,
                       precision=lax.Precision.HIGHEST) + b_ref[0]


def _adaln(c, ada_w, ada_b):
    depth, d, d6 = ada_w.shape
    bsz = c.shape[0]
    tn = 1024
    return pl.pallas_call(
        _mod_kernel,
        grid=(depth, d6 // tn),
        in_specs=[pl.BlockSpec((bsz, d), lambda l, j: (0, 0)),
                  pl.BlockSpec((1, d, tn), lambda l, j: (l, 0, j)),
                  pl.BlockSpec((1, 1, tn), lambda l, j: (l, 0, j))],
        out_specs=pl.BlockSpec((1, bsz, tn), lambda l, j: (l, 0, j)),
        out_shape=jax.ShapeDtypeStruct((depth, bsz, d6), F32),
        compiler_params=_cparams(("arbitrary", "arbitrary")),
        name="adaln",
    )(c, ada_w, ada_b.reshape(depth, 1, d6))


def _proj_kernel(x_ref, g_ref, sh_ref, sc_ref, w_ref, ws_ref, o_ref, os_ref, *, small_transposed, tn):
    hb = (_rms(x_ref[...]) * g_ref[...] * (1.0 + sc_ref[0]) + sh_ref[0]).astype(BF16)
    if small_transposed:
        os_ref[...] = _dot_nt(ws_ref[...], hb)
    else:
        os_ref[...] = _dot(hb, ws_ref[...])
    for j in range(w_ref.shape[1] // tn):
        o_ref[:, j * tn:(j + 1) * tn] = _dot(hb, w_ref[:, j * tn:(j + 1) * tn]).astype(o_ref.dtype)


def _project(x2, g, shift, scale, w, ws, small_transposed, seq, tm, tn, name):
    n, d = x2.shape
    wn = w.shape[1]
    tiles_per_batch = seq // tm
    bsz = shift.shape[0]
    if small_transposed:
        r = ws.shape[0]
        small_spec, small_shape = pl.BlockSpec((r, tm), lambda i: (0, i)), (r, n)
    else:
        small_spec, small_shape = pl.BlockSpec((tm, LANES), lambda i: (i, 0)), (n, LANES)
    resident = pl.Buffered(1)
    return pl.pallas_call(
        functools.partial(_proj_kernel, small_transposed=small_transposed, tn=tn),
        grid=(n // tm,),
        in_specs=[pl.BlockSpec((tm, d), lambda i: (i, 0)),
                  pl.BlockSpec((1, d), lambda i: (0, 0)),
                  pl.BlockSpec((1, 1, d), lambda i: (i // tiles_per_batch, 0, 0)),
                  pl.BlockSpec((1, 1, d), lambda i: (i // tiles_per_batch, 0, 0)),
                  pl.BlockSpec((d, wn), lambda i: (0, 0), pipeline_mode=resident),
                  pl.BlockSpec(ws.shape, lambda i: (0, 0), pipeline_mode=resident)],
        out_specs=[pl.BlockSpec((tm, wn), lambda i: (i, 0)), small_spec],
        out_shape=[jax.ShapeDtypeStruct((n, wn), BF16), jax.ShapeDtypeStruct(small_shape, F32)],
        compiler_params=_cparams(("arbitrary",)),
        name=name,
    )(x2, g.reshape(1, d), shift.reshape(bsz, 1, d), scale.reshape(bsz, 1, d), w, ws)


def _out_kernel(ya_ref, yb_ref, wa_ref, wb_ref, x_ref, gate_ref, g_ref, o_ref):
    acc = _dot(ya_ref[...], wa_ref[...]) + _dot(yb_ref[...], wb_ref[...])
    o_ref[...] = x_ref[...] + gate_ref[0] * (_rms(acc) * g_ref[...])


def _out_project(ya, ia, yb, ib, w, x2, gate, g, seq, tm, name):
    n, d = x2.shape
    kh = w.shape[0] // 2
    tiles_per_batch = seq // tm
    bsz = gate.shape[0]
    return pl.pallas_call(
        _out_kernel,
        grid=(n // tm,),
        in_specs=[pl.BlockSpec((tm, kh), lambda i: (i, ia)),
                  pl.BlockSpec((tm, kh), lambda i: (i, ib)),
                  pl.BlockSpec((kh, d), lambda i: (0, 0)),
                  pl.BlockSpec((kh, d), lambda i: (1, 0)),
                  pl.BlockSpec((tm, d), lambda i: (i, 0)),
                  pl.BlockSpec((1, 1, d), lambda i: (i // tiles_per_batch, 0, 0)),
                  pl.BlockSpec((1, d), lambda i: (0, 0))],
        out_specs=pl.BlockSpec((tm, d), lambda i: (i, 0)),
        out_shape=jax.ShapeDtypeStruct((n, d), F32),
        compiler_params=_cparams(("arbitrary",)),
        name=name,
    )(ya, yb, w, w, x2, gate.reshape(bsz, 1, d), g.reshape(1, d))


def _mlp_kernel(x_ref, g_ref, sh_ref, sc_ref, w1_ref, w2_ref, gate_ref, g2_ref, o_ref, *, tf):
    x = x_ref[...]
    h = (_rms(x) * g_ref[...] * (1.0 + sc_ref[0]) + sh_ref[0]).astype(BF16)
    u = None
    for f in range(w1_ref.shape[1] // tf):
        a = jnp.maximum(_dot(h, w1_ref[:, f * tf:(f + 1) * tf]), 0.0)
        part = _dot((a * a).astype(BF16), w2_ref[f * tf:(f + 1) * tf, :])
        u = part if u is None else u + part
    o_ref[...] = x + gate_ref[0] * (_rms(u) * g2_ref[...])


def _mlp(x2, g, shift, scale, w1, w2, gate, g2, seq, tm, tf, name):
    n, d = x2.shape
    dff = w1.shape[1]
    tiles_per_batch = seq // tm
    bsz = gate.shape[0]
    vec = lambda a: a.reshape(bsz, 1, d)
    bspec = pl.BlockSpec((1, 1, d), lambda i: (i // tiles_per_batch, 0, 0))
    resident = pl.Buffered(1)
    return pl.pallas_call(
        functools.partial(_mlp_kernel, tf=tf),
        grid=(n // tm,),
        in_specs=[pl.BlockSpec((tm, d), lambda i: (i, 0)),
                  pl.BlockSpec((1, d), lambda i: (0, 0)),
                  bspec, bspec,
                  pl.BlockSpec((d, dff), lambda i: (0, 0), pipeline_mode=resident),
                  pl.BlockSpec((dff, d), lambda i: (0, 0), pipeline_mode=resident),
                  bspec,
                  pl.BlockSpec((1, d), lambda i: (0, 0))],
        out_specs=pl.BlockSpec((tm, d), lambda i: (i, 0)),
        out_shape=jax.ShapeDtypeStruct((n, d), F32),
        compiler_params=_cparams(("arbitrary",)),
        name=name,
    )(x2, g.reshape(1, d), vec(shift), vec(scale), w1, w2, vec(gate), g2.reshape(1, d))


def _conv_kernel(main_ref, prev_ref, next_ref, w_ref, b_ref, o_ref, scr, *, blocks_per_seq):
    t = main_ref.shape[0]
    i = pl.program_id(0) % blocks_per_seq
    keep_prev = (i > 0).astype(F32)
    keep_next = (i < blocks_per_seq - 1).astype(F32)
    scr[0:HALO, :] = prev_ref[...].astype(F32) * keep_prev
    scr[HALO:HALO + t, :] = main_ref[...].astype(F32)
    scr[HALO + t:2 * HALO + t, :] = next_ref[...].astype(F32) * keep_next
    pad = SSD_CONV // 2
    acc = b_ref[...] + w_ref[0:1, :] * scr[HALO - pad:HALO - pad + t, :]
    for k in range(1, SSD_CONV):
        acc = acc + w_ref[k:k + 1, :] * scr[HALO - pad + k:HALO - pad + k + t, :]
    o_ref[...] = (acc * _sigmoid(acc)).astype(o_ref.dtype)


def _conv_silu(main, col0, conv_w, conv_b, seq, tr, tc):
    n = main.shape[0]
    ch = conv_w.shape[1]
    cb0 = col0 // tc
    rpb = tr // HALO
    last_halo = n // HALO - 1
    return pl.pallas_call(
        functools.partial(_conv_kernel, blocks_per_seq=seq // tr),
        grid=(n // tr, ch // tc),
        in_specs=[pl.BlockSpec((tr, tc), lambda i, j: (i, cb0 + j)),
                  pl.BlockSpec((HALO, tc), lambda i, j: (jnp.maximum(i * rpb - 1, 0), cb0 + j)),
                  pl.BlockSpec((HALO, tc), lambda i, j: (jnp.minimum((i + 1) * rpb, last_halo), cb0 + j)),
                  pl.BlockSpec((SSD_CONV, tc), lambda i, j: (0, j)),
                  pl.BlockSpec((1, tc), lambda i, j: (0, j))],
        out_specs=pl.BlockSpec((tr, tc), lambda i, j: (i, j)),
        out_shape=jax.ShapeDtypeStruct((n, ch), BF16),
        scratch_shapes=[pltpu.VMEM((tr + 2 * HALO, tc), F32)],
        compiler_params=_cparams(("arbitrary", "arbitrary")),
        name="conv_silu",
    )(main, main, main, conv_w, conv_b.reshape(1, ch))


def _ssd_kernel(xs_ref, bc_ref, dt_ref, dtb_ref, alog_ref, *rest, reverse):
    if reverse:
        z_ref, yf_ref, dskip_ref, ng_ref, o_ref, st_scr = rest
    else:
        o_ref, st_scr = rest
    t = SSD_CHUNK
    hd = SSD_HEAD_DIM
    hpg = SSD_HEADS // SSD_GROUPS

    @pl.when(pl.program_id(1) == 0)
    def _():
        st_scr[...] = jnp.zeros_like(st_scr)

    xs = xs_ref[...].astype(F32)
    bc = bc_ref[...]

    dt = _softplus(dt_ref[...] + dtb_ref[...])
    adt = dt * (-jnp.exp(alog_ref[...]))
    row = _iota((t, t), 0)
    col = _iota((t, t), 1)
    tri = (col >= row) if reverse else (col <= row)
    cs = _dot_exact_lhs01(tri.astype(BF16), adt)
    cs_t = cs.T
    dt_t = dt.T
    base = SSD_HEADS if reverse else 0
    last = 0 if reverse else t - 1

    first_half = _iota((t, LANES), 1) < hd
    ys = []
    for g in range(SSD_GROUPS):
        bg = bc[:, g * SSD_STATE:(g + 1) * SSD_STATE]
        cg = bc[:, (SSD_GROUPS + g) * SSD_STATE:(SSD_GROUPS + g + 1) * SSD_STATE]
        cb = _dot_nt(cg, bg)
        bg_t = bg.astype(F32).T
        s_prev = st_scr[g]
        y_off = _dot(cg, s_prev.astype(BF16))
        s_new = []
        etot = []
        for pr in range(hpg // 2):
            pair = g * (hpg // 2) + pr
            xpair = xs[:, pair * LANES:(pair + 1) * LANES]
            cs_cols = []
            acc = None
            for sub in range(2):
                j = base + 2 * pair + sub
                cs_col = jnp.broadcast_to(cs[:, j:j + 1], (t, t))
                cs_cols.append(cs_col)
                dts = dt_t[j:j + 1, :]
                lmat = jnp.exp(jnp.where(tri, cs_col - cs_t[j:j + 1, :], NEG_BIG))
                m = (cb * lmat * dts).astype(BF16)
                keep = first_half if sub == 0 else jnp.logical_not(first_half)
                part = _dot(m, jnp.where(keep, xpair, 0.0).astype(BF16))
                acc = part if acc is None else acc + part
            cs_pair = jnp.where(first_half, cs_cols[0], cs_cols[1])
            tot_pair = cs_pair[last:last + 1, :]
            xd = (xpair * jnp.exp(tot_pair - cs_pair)).astype(BF16)
            sp = None
            for sub in range(2):
                j = base + 2 * pair + sub
                keep = first_half if sub == 0 else jnp.logical_not(first_half)
                part = _dot((bg_t * dt_t[j:j + 1, :]).astype(BF16), jnp.where(keep, xd, jnp.zeros((), BF16)))
                sp = part if sp is None else sp + part
            ys.append(acc + y_off[:, pr * LANES:(pr + 1) * LANES] * jnp.exp(cs_pair))
            s_new.append(sp)
            etot.append(jnp.exp(tot_pair))
        st_scr[g] = s_prev * jnp.concatenate(etot, axis=1) + jnp.concatenate(s_new, axis=1)
    y = jnp.concatenate(ys, axis=1)

    if reverse:
        y = y + yf_ref[...] + dskip_ref[...] * xs
        z = z_ref[...].astype(F32)
        y = y * (z * _sigmoid(z))
        o_ref[...] = (_rms(y) * ng_ref[...]).astype(o_ref.dtype)
    else:
        o_ref[...] = y


def _ssd_pass(conv, dtf, dtb, alog, bsz, seq, reverse, extra=None):
    n = conv.shape[0]
    t = SSD_CHUNK
    nc = seq // t
    bc_blk = SSD_WIDTH // SSD_BC

    def rows(b, c):
        return b * nc + (nc - 1 - c if reverse else c)

    in_specs = [pl.BlockSpec((t, SSD_WIDTH), lambda b, c: (rows(b, c), 0)),
                pl.BlockSpec((t, SSD_BC), lambda b, c: (rows(b, c), bc_blk)),
                pl.BlockSpec((t, LANES), lambda b, c: (rows(b, c), 0)),
                pl.BlockSpec((1, LANES), lambda b, c: (0, 0)),
                pl.BlockSpec((1, LANES), lambda b, c: (0, 0))]
    args = [conv, conv, dtf, dtb, alog]
    if reverse:
        main, yf, dskip, ng = extra
        in_specs += [pl.BlockSpec((t, SSD_WIDTH), lambda b, c: (rows(b, c), 0)),
                     pl.BlockSpec((t, SSD_WIDTH), lambda b, c: (rows(b, c), 0)),
                     pl.BlockSpec((1, SSD_WIDTH), lambda b, c: (0, 0)),
                     pl.BlockSpec((1, SSD_WIDTH), lambda b, c: (0, 0))]
        args += [main, yf, dskip, ng]
        out_dtype = BF16
    else:
        out_dtype = F32
    return pl.pallas_call(
        functools.partial(_ssd_kernel, reverse=reverse),
        grid=(bsz, nc),
        in_specs=in_specs,
        out_specs=pl.BlockSpec((t, SSD_WIDTH), lambda b, c: (rows(b, c), 0)),
        out_shape=jax.ShapeDtypeStruct((n, SSD_WIDTH), out_dtype),
        scratch_shapes=[pltpu.VMEM((SSD_GROUPS, SSD_STATE, SSD_WIDTH // SSD_GROUPS), F32)],
        compiler_params=_cparams(("arbitrary", "arbitrary")),
        name="ssd_bwd" if reverse else "ssd_fwd",
    )(*args)


def _na_kernel(q_ref, k_ref, v_ref, bias_ref, o_ref, s_scr, p_scr, l_scr, *, nrows):
    w = GRID_W
    band = NA_KH * w
    first = _iota((w, LANES), 1) < NA_HEAD_DIM
    zero = jnp.zeros((), BF16)

    def band_start(r):
        return jnp.clip(r - NA_KH // 2, 0, nrows - NA_KH)

    def scores(r, slot):
        rs = band_start(r)
        q = q_ref[pl.ds(pl.multiple_of(r * w, w), w), :] * jnp.asarray(NA_HEAD_DIM ** -0.5, BF16)
        qs = jnp.concatenate([jnp.where(first, q, zero), jnp.where(first, zero, q)], axis=0)
        kb = k_ref[pl.ds(pl.multiple_of(rs * w, w), band), :]
        j0 = NA_KH - 1 - (r - rs)
        bias = jnp.concatenate([bias_ref[0, j0 + 2 * a] for a in range(NA_KH // 2)], axis=1)
        s_scr[slot] = _dot_nt(qs, kb) + bias

    def softmax(slot):
        s = s_scr[slot]
        p = jnp.exp(s - jnp.max(s, axis=-1, keepdims=True))
        l_scr[slot] = jnp.broadcast_to(jnp.sum(p, axis=-1, keepdims=True), (2 * w, LANES))
        p_scr[slot] = p.astype(BF16)

    def values(r, slot):
        rs = band_start(r)
        vb = v_ref[pl.ds(pl.multiple_of(rs * w, w), band), :]
        o = _dot(p_scr[slot], vb) / l_scr[slot]
        o_ref[pl.ds(pl.multiple_of(r * w, w), w), :] = jnp.where(first, o[0:w], o[w:2 * w]).astype(o_ref.dtype)

    s_scr[...] = jnp.zeros_like(s_scr)
    p_scr[...] = jnp.zeros_like(p_scr)
    l_scr[...] = jnp.ones_like(l_scr)

    def body(i2, carry):
        for slot in range(2):
            i = 2 * i2 + slot
            values(jnp.maximum(i - 2, 0), slot)
            softmax(1 - slot)
            scores(jnp.minimum(i, nrows - 1), slot)
        return carry

    lax.fori_loop(0, nrows // 2 + 1, body, 0)


def _na_bias(rpb):
    w = GRID_W
    h, nro, nrel = rpb.shape
    c = np.arange(w)
    kc = np.arange(w)
    col_start = np.clip(c - NA_KW // 2, 0, w - NA_KW)
    valid = (kc[None, :] >= col_start[:, None]) & (kc[None, :] < col_start[:, None] + NA_KW)
    co = kc[None, :] - c[:, None] + NA_KW - 1
    onehot = ((np.arange(nrel)[:, None, None] == co[None]) & valid[None]).astype(np.float32)
    rows = rpb.astype(F32).reshape(h // 2, 2, nro, nrel).transpose(0, 2, 1, 3)
    toe = jnp.dot(rows.reshape(h * nro, nrel), onehot.reshape(nrel, w * w),
                  precision=lax.Precision.HIGHEST).reshape(h // 2, nro, 2, w, w)
    toe = toe + np.where(valid, 0.0, NEG_BIG).astype(np.float32)
    pairs = jnp.concatenate([toe[:, :nro - 1], toe[:, 1:]], axis=-1)
    return pairs.reshape(h // 2, nro - 1, 2 * w, 2 * w)


def _na(main, bias, bsz, seq, q_blk):
    n = main.shape[0]
    nrows = seq // GRID_W
    assert nrows >= NA_KH
    pairs = NA_HEADS // 2
    return pl.pallas_call(
        functools.partial(_na_kernel, nrows=nrows),
        grid=(bsz, pairs),
        in_specs=[pl.BlockSpec((seq, LANES), lambda b, p: (b, q_blk + p)),
                  pl.BlockSpec((seq, LANES), lambda b, p: (b, q_blk + pairs + p)),
                  pl.BlockSpec((seq, LANES), lambda b, p: (b, q_blk + 2 * pairs + p)),
                  pl.BlockSpec((1,) + bias.shape[1:], lambda b, p: (p, 0, 0, 0))],
        out_specs=pl.BlockSpec((seq, LANES), lambda b, p: (b, p)),
        out_shape=jax.ShapeDtypeStruct((n, NA_WIDTH), BF16),
        scratch_shapes=[pltpu.VMEM((2, 2 * GRID_W, NA_KH * GRID_W), F32),
                        pltpu.VMEM((2, 2 * GRID_W, NA_KH * GRID_W), BF16),
                        pltpu.VMEM((2, 2 * GRID_W, LANES), F32)],
        compiler_params=_cparams(("arbitrary", "arbitrary")),
        name="nbr_attn",
    )(main, main, main, bias)


def _mlstm_kernel(q_ref, k_ref, v_ref, og_ref, gt_ref, gbt_ref, hg_ref, o_ref,
                  rows_scr, cp_scr, c_st, n_st, m_st, *, nchunks):
    t = ML_CHUNK
    head = pl.program_id(1)
    scale = ML_DK ** -0.5
    row = _iota((t, t), 0)
    col = _iota((t, t), 1)
    tris = (col <= row, col >= row)
    eye = col == row
    lane_t = _iota((SUBLANES, t), 1)
    sub1 = _iota((4 * ML_HEADS, 1), 0)

    def gate_row(gr, kind):
        sel = jnp.sum(jnp.where(sub1 == kind * ML_HEADS + head, gr, 0.0), axis=0, keepdims=True)
        return jnp.broadcast_to(sel, (SUBLANES, t))

    def cumsum_row(x, reverse):
        sh = 1
        while sh < t:
            if reverse:
                x = x + jnp.where(lane_t < t - sh, pltpu.roll(x, t - sh, axis=1), 0.0)
            else:
                x = x + jnp.where(lane_t >= sh, pltpu.roll(x, sh, axis=1), 0.0)
            sh *= 2
        return x

    c_st[...] = jnp.zeros_like(c_st)
    n_st[...] = jnp.zeros_like(n_st)
    m_st[...] = jnp.zeros_like(m_st)

    def scan_step(i, carry):
        for d in range(2):
            c = nchunks - 1 - i if d else i
            r0 = pl.multiple_of(c * t, t)
            gr = gt_ref[:, pl.ds(r0, t)] + gbt_ref[...]
            ig = gate_row(gr, 2 * d)
            b = cumsum_row(_log_sigmoid(gate_row(gr, 2 * d + 1)), bool(d))
            g_tot = b[:, 0:1] if d else b[:, t - 1:t]
            a = g_tot - b + ig
            m_loc = jnp.max(a, axis=-1, keepdims=True)
            w = jnp.exp(a - m_loc)[0:1, :]
            kw_t = k_ref[pl.ds(r0, t), :].astype(F32).T * w
            s_loc = _dot(kw_t.astype(BF16), v_ref[pl.ds(r0, t), :])
            n_loc = jnp.sum(kw_t, axis=-1, keepdims=True)
            c_prev = c_st[d]
            n_prev = n_st[d]
            m_prev = m_st[d]
            cp_scr[d, c] = jnp.concatenate([c_prev, n_prev], axis=1).astype(BF16)
            rows_scr[d, c, 0] = b
            rows_scr[d, c, 1] = ig - b
            rows_scr[d, c, 2] = jnp.concatenate([m_prev] * (t // LANES), axis=1)
            gt8 = jnp.broadcast_to(g_tot, (SUBLANES, LANES))
            ml8 = jnp.broadcast_to(m_loc, (SUBLANES, LANES))
            m_new = jnp.maximum(gt8 + m_prev, ml8)
            s_old = jnp.exp(gt8 + m_prev - m_new)[0:1, :]
            s_new = jnp.exp(ml8 - m_new)[0:1, :]
            c_st[d] = (jnp.concatenate([s_old, s_old], axis=1) * c_prev
                       + jnp.concatenate([s_new, s_new], axis=1) * s_loc)
            n_st[d] = s_old * n_prev + s_new * n_loc
            m_st[d] = m_new
        return carry

    lax.fori_loop(0, nchunks, scan_step, 0, unroll=2)

    def out_step(c, carry):
        r0 = pl.multiple_of(c * t, t)
        q = q_ref[pl.ds(r0, t), :]
        v = v_ref[pl.ds(r0, t), :]
        qk = _dot_nt(q, k_ref[pl.ds(r0, t), :]) * scale
        hsum = None
        for d in range(2):
            b_row = rows_scr[d, c, 0][0:1, :]
            rs_row = rows_scr[d, c, 1][0:1, :]
            m_prev = rows_scr[d, c, 2][0:1, 0:1]
            b_col = jnp.sum(jnp.where(eye, b_row, 0.0), axis=-1, keepdims=True)
            dm = jnp.where(tris[d], b_col + rs_row, NEG_BIG)
            m_inter = b_col + m_prev
            m_t = jnp.maximum(m_inter, jnp.max(dm, axis=-1, keepdims=True))
            sc = qk * jnp.exp(dm - m_t)
            inter = jnp.exp(m_inter - m_t) * scale
            ext = _dot(q, cp_scr[d, c])
            num = _dot(sc.astype(BF16), v) + inter * ext[:, 0:ML_DV]
            den = jnp.sum(sc, axis=-1, keepdims=True) + inter * ext[:, ML_DV:ML_DV + 1]
            hd = num / jnp.maximum(jnp.abs(den), jnp.exp(-m_t))
            hsum = hd if hsum is None else hsum + hd
        og = og_ref[pl.ds(r0, t), :].astype(F32)
        o_ref[pl.ds(r0, t), :] = (_sigmoid(og) * (_rms(hsum) * hg_ref[0])).astype(o_ref.dtype)
        return carry

    lax.fori_loop(0, nchunks, out_step, 0)


def _mlstm(main, gates_t, gate_bt, head_g, bsz, seq):
    n = main.shape[0]
    nc = seq // ML_CHUNK
    nh = ML_HEADS
    r = gates_t.shape[0]
    return pl.pallas_call(
        functools.partial(_mlstm_kernel, nchunks=nc),
        grid=(bsz, nh),
        in_specs=[pl.BlockSpec((seq, ML_DK), lambda b, h: (b, h)),
                  pl.BlockSpec((seq, ML_DK), lambda b, h: (b, nh + h)),
                  pl.BlockSpec((seq, ML_DV), lambda b, h: (b, nh + h)),
                  pl.BlockSpec((seq, ML_DV), lambda b, h: (b, 2 * nh + h)),
                  pl.BlockSpec((r, seq), lambda b, h: (0, b)),
                  pl.BlockSpec((r, 1), lambda b, h: (0, 0)),
                  pl.BlockSpec((1, 1, ML_DV), lambda b, h: (h, 0, 0))],
        out_specs=pl.BlockSpec((seq, ML_DV), lambda b, h: (b, h)),
        out_shape=jax.ShapeDtypeStruct((n, ML_WIDTH), BF16),
        scratch_shapes=[pltpu.VMEM((2, nc, 3, SUBLANES, ML_CHUNK), F32),
                        pltpu.VMEM((2, nc, ML_DK, ML_DV + LANES), BF16),
                        pltpu.VMEM((2, ML_DK, ML_DV), F32),
                        pltpu.VMEM((2, ML_DK, LANES), F32),
                        pltpu.VMEM((2, SUBLANES, LANES), F32)],
        compiler_params=_cparams(("arbitrary", "arbitrary")),
        name="mlstm",
    )(main, main, main, main, gates_t, gate_bt, head_g.reshape(nh, 1, ML_DV))


def _pad_cols(a, width):
    return jnp.pad(a, ((0, 0), (0, width - a.shape[1])))


def _ssd_na_layer(x2, mods, norm_g, w_in, conv_w, conv_b, dt_bias, a_log, d_skip, ssd_norm, rpb, w_out,
                  bsz, seq):
    sh1, sc1, g1 = mods
    s1 = SSD_WIDTH
    s2 = s1 + SSD_CONV_CH
    s3 = s2 + 2 * SSD_HEADS
    w_main = jnp.concatenate([w_in[:, :s2], w_in[:, s3:]], axis=1).astype(BF16)
    w_dt = _pad_cols(w_in[:, s2:s3], LANES).astype(BF16)
    main, dtf = _project(x2, norm_g[0], sh1, sc1, w_main, w_dt, False, seq, tm=min(512, seq), tn=1408,
                         name="proj_ssd_na")

    conv = _conv_silu(main, s1, conv_w, conv_b, seq, tr=min(512, seq), tc=512)
    dtb = _pad_cols(dt_bias.reshape(1, 2 * SSD_HEADS), LANES)
    alog = _pad_cols(a_log.reshape(1, 2 * SSD_HEADS), LANES)
    dskip = jnp.repeat(d_skip, SSD_HEAD_DIM)[None, :]
    y_f = _ssd_pass(conv, dtf, dtb, alog, bsz, seq, reverse=False)
    y_ssd = _ssd_pass(conv, dtf, dtb, alog, bsz, seq, reverse=True,
                      extra=(main, y_f, dskip, ssd_norm[None, :]))

    q_blk = (s1 + SSD_CONV_CH) // LANES
    y_na = _na(main, _na_bias(rpb), bsz, seq, q_blk)
    return _out_project(y_ssd, 0, y_na, 0, w_out.astype(BF16), x2, g1, norm_g[1], seq, tm=min(512, seq),
                        name="out_ssd_na")


def _mlstm_layer(x2, mods, norm_g, w_in, gate_b, head_g, w_out, bsz, seq):
    sh1, sc1, g1 = mods
    wm = 2 * ML_QK + 2 * ML_WIDTH
    w_main = w_in[:, :wm].astype(BF16)
    main, gates_t = _project(x2, norm_g[0], sh1, sc1, w_main, w_in[:, wm:].T.astype(BF16), True, seq,
                             tm=min(512, seq), tn=1536, name="proj_mlstm")
    y = _mlstm(main, gates_t, gate_b.reshape(4 * ML_HEADS, 1).astype(F32), head_g, bsz, seq)
    return _out_project(y, 0, y, 1, w_out.astype(BF16), x2, g1, norm_g[1], seq, tm=min(512, seq),
                        name="out_mlstm")


def kernel(x, c, ada_w, ada_b, norm_g, mlp_w1, mlp_w2, ab_w_in, ab_conv_w, ab_conv_b, ab_dt_bias, ab_a_log,
           ab_d_skip, ab_ssd_norm, ab_rpb, ab_w_out, ml_w_in, ml_gate_b, ml_head_norm, ml_w_out):
    bsz, seq, d = x.shape
    depth = ada_w.shape[0]
    mod = _adaln(c, ada_w, ada_b)
    x2 = x.reshape(bsz * seq, d)
    for layer in range(depth):
        sh1, sc1, g1, sh2, sc2, g2 = [mod[layer, :, i * d:(i + 1) * d] for i in range(6)]
        j = layer // 2
        if layer % 2 == 0:
            x2 = _ssd_na_layer(x2, (sh1, sc1, g1), norm_g[layer], ab_w_in[j], ab_conv_w[j], ab_conv_b[j],
                               ab_dt_bias[j], ab_a_log[j], ab_d_skip[j], ab_ssd_norm[j], ab_rpb[j],
                               ab_w_out[j], bsz, seq)
        else:
            x2 = _mlstm_layer(x2, (sh1, sc1, g1), norm_g[layer], ml_w_in[j], ml_gate_b[j], ml_head_norm[j],
                              ml_w_out[j], bsz, seq)
        x2 = _mlp(x2, norm_g[layer, 2], sh2, sc2, mlp_w1[layer].astype(BF16), mlp_w2[layer].astype(BF16),
                  g2, norm_g[layer, 3], seq, tm=min(1024, seq), tf=1024, name="mlp%d" % layer)
    return x2.reshape(bsz, seq, d)
```

```python
import functools

import numpy as np
import jax
import jax.numpy as jnp
from jax import lax
from jax.experimental import pallas as pl
from jax.experimental.pallas import tpu as pltpu

F32 = jnp.float32
BF16 = jnp.bfloat16

NORM_EPS = 1e-6
GRID_W = 64

SSD_HEAD_DIM = 64
SSD_HEADS = 16
SSD_GROUPS = 2
SSD_STATE = 128
SSD_CONV = 5
SSD_CHUNK = 128
SSD_WIDTH = SSD_HEADS * SSD_HEAD_DIM
SSD_BC = 2 * SSD_GROUPS * SSD_STATE
SSD_CONV_CH = SSD_WIDTH + SSD_BC

NA_HEAD_DIM = 64
NA_HEADS = 16
NA_KH = 8
NA_KW = 16
NA_WIDTH = NA_HEADS * NA_HEAD_DIM

ML_HEADS = 8
ML_DV = 256
ML_DK = 128
ML_CHUNK = 256
ML_WIDTH = ML_HEADS * ML_DV
ML_QK = ML_HEADS * ML_DK

LANES = 128
SUBLANES = 8
HALO = 16
NEG_BIG = -1e30
VMEM_LIMIT = 56 * 1024 * 1024


def _cparams(sem):
    return pltpu.CompilerParams(dimension_semantics=sem, vmem_limit_bytes=VMEM_LIMIT)


def _dot(a, b):
    return jnp.dot(a, b, preferred_element_type=F32)


def _dot_nt(a, b):
    return lax.dot_general(a, b, (((1,), (1,)), ((), ())), preferred_element_type=F32)


def _split3(x):
    hi = x.astype(BF16)
    r1 = x - hi.astype(F32)
    mid = r1.astype(BF16)
    lo = (r1 - mid.astype(F32)).astype(BF16)
    return hi, mid, lo


def _dot_exact_lhs01(sel, x):
    hi, mid, lo = _split3(x)
    return _dot(sel, hi) + _dot(sel, mid) + _dot(sel, lo)


def _iota(shape, dim):
    return lax.broadcasted_iota(jnp.int32, shape, dim)


def _sigmoid(x):
    return 0.5 * (jnp.tanh(0.5 * x) + 1.0)


def _softplus(x):
    return jnp.maximum(x, 0.0) + jnp.log(1.0 + jnp.exp(-jnp.abs(x)))


def _log_sigmoid(x):
    return jnp.minimum(x, 0.0) - jnp.log(1.0 + jnp.exp(-jnp.abs(x)))


def _rms(x):
    return x * lax.rsqrt(jnp.mean(x * x, axis=-1, keepdims=True) + NORM_EPS)


def _mod_kernel(c_ref, w_ref, b_ref, o_ref):
    c = c_ref[...]
    cond = c * _sigmoid(c)
    o_ref[0] = jnp.dot(cond, w_ref[0], preferred_element_type=F32,
                       precision=lax.Precision.HIGHEST) + b_ref[0]


def _adaln(c, ada_w, ada_b):
    depth, d, d6 = ada_w.shape
    bsz = c.shape[0]
    tn = 1024
    return pl.pallas_call(
        _mod_kernel,
        grid=(depth, d6 // tn),
        in_specs=[pl.BlockSpec((bsz, d), lambda l, j: (0, 0)),
                  pl.BlockSpec((1, d, tn), lambda l, j: (l, 0, j)),
                  pl.BlockSpec((1, 1, tn), lambda l, j: (l, 0, j))],
        out_specs=pl.BlockSpec((1, bsz, tn), lambda l, j: (l, 0, j)),
        out_shape=jax.ShapeDtypeStruct((depth, bsz, d6), F32),
        compiler_params=_cparams(("arbitrary", "arbitrary")),
        name="adaln",
    )(c, ada_w, ada_b.reshape(depth, 1, d6))


def _proj_kernel(x_ref, g_ref, sh_ref, sc_ref, w_ref, ws_ref, o_ref, os_ref, *, small_transposed, tn):
    hb = (_rms(x_ref[...]) * g_ref[...] * (1.0 + sc_ref[0]) + sh_ref[0]).astype(BF16)
    if small_transposed:
        os_ref[...] = _dot_nt(ws_ref[...], hb)
    else:
        os_ref[...] = _dot(hb, ws_ref[...])
    for j in range(w_ref.shape[1] // tn):
        o_ref[:, j * tn:(j + 1) * tn] = _dot(hb, w_ref[:, j * tn:(j + 1) * tn]).astype(o_ref.dtype)


def _project(x2, g, shift, scale, w, ws, small_transposed, seq, tm, tn, name):
    n, d = x2.shape
    wn = w.shape[1]
    tiles_per_batch = seq // tm
    bsz = shift.shape[0]
    if small_transposed:
        r = ws.shape[0]
        small_spec, small_shape = pl.BlockSpec((r, tm), lambda i: (0, i)), (r, n)
    else:
        small_spec, small_shape = pl.BlockSpec((tm, LANES), lambda i: (i, 0)), (n, LANES)
    resident = pl.Buffered(1)
    return pl.pallas_call(
        functools.partial(_proj_kernel, small_transposed=small_transposed, tn=tn),
        grid=(n // tm,),
        in_specs=[pl.BlockSpec((tm, d), lambda i: (i, 0)),
                  pl.BlockSpec((1, d), lambda i: (0, 0)),
                  pl.BlockSpec((1, 1, d), lambda i: (i // tiles_per_batch, 0, 0)),
                  pl.BlockSpec((1, 1, d), lambda i: (i // tiles_per_batch, 0, 0)),
                  pl.BlockSpec((d, wn), lambda i: (0, 0), pipeline_mode=resident),
                  pl.BlockSpec(ws.shape, lambda i: (0, 0), pipeline_mode=resident)],
        out_specs=[pl.BlockSpec((tm, wn), lambda i: (i, 0)), small_spec],
        out_shape=[jax.ShapeDtypeStruct((n, wn), BF16), jax.ShapeDtypeStruct(small_shape, F32)],
        compiler_params=_cparams(("arbitrary",)),
        name=name,
    )(x2, g.reshape(1, d), shift.reshape(bsz, 1, d), scale.reshape(bsz, 1, d), w, ws)


def _out_kernel(ya_ref, yb_ref, wa_ref, wb_ref, x_ref, gate_ref, g_ref, o_ref):
    acc = _dot(ya_ref[...], wa_ref[...]) + _dot(yb_ref[...], wb_ref[...])
    o_ref[...] = x_ref[...] + gate_ref[0] * (_rms(acc) * g_ref[...])


def _out_project(ya, ia, yb, ib, w, x2, gate, g, seq, tm, name):
    n, d = x2.shape
    kh = w.shape[0] // 2
    tiles_per_batch = seq // tm
    bsz = gate.shape[0]
    return pl.pallas_call(
        _out_kernel,
        grid=(n // tm,),
        in_specs=[pl.BlockSpec((tm, kh), lambda i: (i, ia)),
                  pl.BlockSpec((tm, kh), lambda i: (i, ib)),
                  pl.BlockSpec((kh, d), lambda i: (0, 0)),
                  pl.BlockSpec((kh, d), lambda i: (1, 0)),
                  pl.BlockSpec((tm, d), lambda i: (i, 0)),
                  pl.BlockSpec((1, 1, d), lambda i: (i // tiles_per_batch, 0, 0)),
                  pl.BlockSpec((1, d), lambda i: (0, 0))],
        out_specs=pl.BlockSpec((tm, d), lambda i: (i, 0)),
        out_shape=jax.ShapeDtypeStruct((n, d), F32),
        compiler_params=_cparams(("arbitrary",)),
        name=name,
    )(ya, yb, w, w, x2, gate.reshape(bsz, 1, d), g.reshape(1, d))


def _mlp_kernel(x_ref, g_ref, sh_ref, sc_ref, w1_ref, w2_ref, gate_ref, g2_ref, o_ref, *, tf):
    x = x_ref[...]
    h = (_rms(x) * g_ref[...] * (1.0 + sc_ref[0]) + sh_ref[0]).astype(BF16)
    u = None
    for f in range(w1_ref.shape[1] // tf):
        a = jnp.maximum(_dot(h, w1_ref[:, f * tf:(f + 1) * tf]), 0.0)
        part = _dot((a * a).astype(BF16), w2_ref[f * tf:(f + 1) * tf, :])
        u = part if u is None else u + part
    o_ref[...] = x + gate_ref[0] * (_rms(u) * g2_ref[...])


def _mlp(x2, g, shift, scale, w1, w2, gate, g2, seq, tm, tf, name):
    n, d = x2.shape
    dff = w1.shape[1]
    tiles_per_batch = seq // tm
    bsz = gate.shape[0]
    vec = lambda a: a.reshape(bsz, 1, d)
    bspec = pl.BlockSpec((1, 1, d), lambda i: (i // tiles_per_batch, 0, 0))
    resident = pl.Buffered(1)
    return pl.pallas_call(
        functools.partial(_mlp_kernel, tf=tf),
        grid=(n // tm,),
        in_specs=[pl.BlockSpec((tm, d), lambda i: (i, 0)),
                  pl.BlockSpec((1, d), lambda i: (0, 0)),
                  bspec, bspec,
                  pl.BlockSpec((d, dff), lambda i: (0, 0), pipeline_mode=resident),
                  pl.BlockSpec((dff, d), lambda i: (0, 0), pipeline_mode=resident),
                  bspec,
                  pl.BlockSpec((1, d), lambda i: (0, 0))],
        out_specs=pl.BlockSpec((tm, d), lambda i: (i, 0)),
        out_shape=jax.ShapeDtypeStruct((n, d), F32),
        compiler_params=_cparams(("arbitrary",)),
        name=name,
    )(x2, g.reshape(1, d), vec(shift), vec(scale), w1, w2, vec(gate), g2.reshape(1, d))


def _conv_kernel(main_ref, prev_ref, next_ref, w_ref, b_ref, o_ref, scr, *, blocks_per_seq):
    t = main_ref.shape[0]
    i = pl.program_id(0) % blocks_per_seq
    keep_prev = (i > 0).astype(F32)
    keep_next = (i < blocks_per_seq - 1).astype(F32)
    scr[0:HALO, :] = prev_ref[...].astype(F32) * keep_prev
    scr[HALO:HALO + t, :] = main_ref[...].astype(F32)
    scr[HALO + t:2 * HALO + t, :] = next_ref[...].astype(F32) * keep_next
    pad = SSD_CONV // 2
    acc = b_ref[...] + w_ref[0:1, :] * scr[HALO - pad:HALO - pad + t, :]
    for k in range(1, SSD_CONV):
        acc = acc + w_ref[k:k + 1, :] * scr[HALO - pad + k:HALO - pad + k + t, :]
    o_ref[...] = (acc * _sigmoid(acc)).astype(o_ref.dtype)


def _conv_silu(main, col0, conv_w, conv_b, seq, tr, tc):
    n = main.shape[0]
    ch = conv_w.shape[1]
    cb0 = col0 // tc
    rpb = tr // HALO
    last_halo = n // HALO - 1
    return pl.pallas_call(
        functools.partial(_conv_kernel, blocks_per_seq=seq // tr),
        grid=(n // tr, ch // tc),
        in_specs=[pl.BlockSpec((tr, tc), lambda i, j: (i, cb0 + j)),
                  pl.BlockSpec((HALO, tc), lambda i, j: (jnp.maximum(i * rpb - 1, 0), cb0 + j)),
                  pl.BlockSpec((HALO, tc), lambda i, j: (jnp.minimum((i + 1) * rpb, last_halo), cb0 + j)),
                  pl.BlockSpec((SSD_CONV, tc), lambda i, j: (0, j)),
                  pl.BlockSpec((1, tc), lambda i, j: (0, j))],
        out_specs=pl.BlockSpec((tr, tc), lambda i, j: (i, j)),
        out_shape=jax.ShapeDtypeStruct((n, ch), BF16),
        scratch_shapes=[pltpu.VMEM((tr + 2 * HALO, tc), F32)],
        compiler_params=_cparams(("arbitrary", "arbitrary")),
        name="conv_silu",
    )(main, main, main, conv_w, conv_b.reshape(1, ch))


def _ssd_kernel(xs_ref, bc_ref, dt_ref, dtb_ref, alog_ref, *rest, reverse):
    if reverse:
        z_ref, yf_ref, dskip_ref, ng_ref, o_ref, st_scr = rest
    else:
        o_ref, st_scr = rest
    t = SSD_CHUNK
    hd = SSD_HEAD_DIM
    hpg = SSD_HEADS // SSD_GROUPS

    @pl.when(pl.program_id(1) == 0)
    def _():
        st_scr[...] = jnp.zeros_like(st_scr)

    xs = xs_ref[...].astype(F32)
    bc = bc_ref[...]

    dt = _softplus(dt_ref[...] + dtb_ref[...])
    adt = dt * (-jnp.exp(alog_ref[...]))
    row = _iota((t, t), 0)
    col = _iota((t, t), 1)
    tri = (col >= row) if reverse else (col <= row)
    cs = _dot_exact_lhs01(tri.astype(BF16), adt)
    cs_t = cs.T
    dt_t = dt.T
    base = SSD_HEADS if reverse else 0
    last = 0 if reverse else t - 1

    first_half = _iota((t, LANES), 1) < hd
    ys = []
    for g in range(SSD_GROUPS):
        bg = bc[:, g * SSD_STATE:(g + 1) * SSD_STATE]
        cg = bc[:, (SSD_GROUPS + g) * SSD_STATE:(SSD_GROUPS + g + 1) * SSD_STATE]
        cb = _dot_nt(cg, bg)
        bg_t = bg.astype(F32).T
        s_prev = st_scr[g]
        y_off = _dot(cg, s_prev.astype(BF16))
        s_new = []
        etot = []
        for pr in range(hpg // 2):
            pair = g * (hpg // 2) + pr
            xpair = xs[:, pair * LANES:(pair + 1) * LANES]
            cs_cols = []
            acc = None
            for sub in range(2):
                j = base + 2 * pair + sub
                cs_col = jnp.broadcast_to(cs[:, j:j + 1], (t, t))
                cs_cols.append(cs_col)
                dts = dt_t[j:j + 1, :]
                lmat = jnp.exp(jnp.where(tri, cs_col - cs_t[j:j + 1, :], NEG_BIG))
                m = (cb * lmat * dts).astype(BF16)
                keep = first_half if sub == 0 else jnp.logical_not(first_half)
                part = _dot(m, jnp.where(keep, xpair, 0.0).astype(BF16))
                acc = part if acc is None else acc + part
            cs_pair = jnp.where(first_half, cs_cols[0], cs_cols[1])
            tot_pair = cs_pair[last:last + 1, :]
            xd = (xpair * jnp.exp(tot_pair - cs_pair)).astype(BF16)
            sp = None
            for sub in range(2):
                j = base + 2 * pair + sub
                keep = first_half if sub == 0 else jnp.logical_not(first_half)
                part = _dot((bg_t * dt_t[j:j + 1, :]).astype(BF16), jnp.where(keep, xd, jnp.zeros((), BF16)))
                sp = part if sp is None else sp + part
            ys.append(acc + y_off[:, pr * LANES:(pr + 1) * LANES] * jnp.exp(cs_pair))
            s_new.append(sp)
            etot.append(jnp.exp(tot_pair))
        st_scr[g] = s_prev * jnp.concatenate(etot, axis=1) + jnp.concatenate(s_new, axis=1)
    y = jnp.concatenate(ys, axis=1)

    if reverse:
        y = y + yf_ref[...] + dskip_ref[...] * xs
        z = z_ref[...].astype(F32)
        y = y * (z * _sigmoid(z))
        o_ref[...] = (_rms(y) * ng_ref[...]).astype(o_ref.dtype)
    else:
        o_ref[...] = y


def _ssd_pass(conv, dtf, dtb, alog, bsz, seq, reverse, extra=None):
    n = conv.shape[0]
    t = SSD_CHUNK
    nc = seq // t
    bc_blk = SSD_WIDTH // SSD_BC

    def rows(b, c):
        return b * nc + (nc - 1 - c if reverse else c)

    in_specs = [pl.BlockSpec((t, SSD_WIDTH), lambda b, c: (rows(b, c), 0)),
                pl.BlockSpec((t, SSD_BC), lambda b, c: (rows(b, c), bc_blk)),
                pl.BlockSpec((t, LANES), lambda b, c: (rows(b, c), 0)),
                pl.BlockSpec((1, LANES), lambda b, c: (0, 0)),
                pl.BlockSpec((1, LANES), lambda b, c: (0, 0))]
    args = [conv, conv, dtf, dtb, alog]
    if reverse:
        main, yf, dskip, ng = extra
        in_specs += [pl.BlockSpec((t, SSD_WIDTH), lambda b, c: (rows(b, c), 0)),
                     pl.BlockSpec((t, SSD_WIDTH), lambda b, c: (rows(b, c), 0)),
                     pl.BlockSpec((1, SSD_WIDTH), lambda b, c: (0, 0)),
                     pl.BlockSpec((1, SSD_WIDTH), lambda b, c: (0, 0))]
        args += [main, yf, dskip, ng]
        out_dtype = BF16
    else:
        out_dtype = F32
    return pl.pallas_call(
        functools.partial(_ssd_kernel, reverse=reverse),
        grid=(bsz, nc),
        in_specs=in_specs,
        out_specs=pl.BlockSpec((t, SSD_WIDTH), lambda b, c: (rows(b, c), 0)),
        out_shape=jax.ShapeDtypeStruct((n, SSD_WIDTH), out_dtype),
        scratch_shapes=[pltpu.VMEM((SSD_GROUPS, SSD_STATE, SSD_WIDTH // SSD_GROUPS), F32)],
        compiler_params=_cparams(("arbitrary", "arbitrary")),
        name="ssd_bwd" if reverse else "ssd_fwd",
    )(*args)


def _na_kernel(q_ref, k_ref, v_ref, bias_ref, o_ref, s_scr, p_scr, l_scr, *, nrows):
    w = GRID_W
    band = NA_KH * w
    first = _iota((w, LANES), 1) < NA_HEAD_DIM
    zero = jnp.zeros((), BF16)

    def band_start(r):
        return jnp.clip(r - NA_KH // 2, 0, nrows - NA_KH)

    def scores(r, slot):
        rs = band_start(r)
        q = q_ref[pl.ds(pl.multiple_of(r * w, w), w), :] * jnp.asarray(NA_HEAD_DIM ** -0.5, BF16)
        qs = jnp.concatenate([jnp.where(first, q, zero), jnp.where(first, zero, q)], axis=0)
        kb = k_ref[pl.ds(pl.multiple_of(rs * w, w), band), :]
        j0 = NA_KH - 1 - (r - rs)
        bias = jnp.concatenate([bias_ref[0, j0 + 2 * a] for a in range(NA_KH // 2)], axis=1)
        s_scr[slot] = _dot_nt(qs, kb) + bias

    def softmax(slot):
        s = s_scr[slot]
        p = jnp.exp(s - jnp.max(s, axis=-1, keepdims=True))
        l_scr[slot] = jnp.broadcast_to(jnp.sum(p, axis=-1, keepdims=True), (2 * w, LANES))
        p_scr[slot] = p.astype(BF16)

    def values(r, slot):
        rs = band_start(r)
        vb = v_ref[pl.ds(pl.multiple_of(rs * w, w), band), :]
        o = _dot(p_scr[slot], vb) / l_scr[slot]
        o_ref[pl.ds(pl.multiple_of(r * w, w), w), :] = jnp.where(first, o[0:w], o[w:2 * w]).astype(o_ref.dtype)

    s_scr[...] = jnp.zeros_like(s_scr)
    p_scr[...] = jnp.zeros_like(p_scr)
    l_scr[...] = jnp.ones_like(l_scr)

    def body(i2, carry):
        for slot in range(2):
            i = 2 * i2 + slot
            values(jnp.maximum(i - 2, 0), slot)
            softmax(1 - slot)
            scores(jnp.minimum(i, nrows - 1), slot)
        return carry

    lax.fori_loop(0, nrows // 2 + 1, body, 0)


def _na_bias(rpb):
    w = GRID_W
    h, nro, nrel = rpb.shape
    c = np.arange(w)
    kc = np.arange(w)
    col_start = np.clip(c - NA_KW // 2, 0, w - NA_KW)
    valid = (kc[None, :] >= col_start[:, None]) & (kc[None, :] < col_start[:, None] + NA_KW)
    co = kc[None, :] - c[:, None] + NA_KW - 1
    onehot = ((np.arange(nrel)[:, None, None] == co[None]) & valid[None]).astype(np.float32)
    rows = rpb.astype(F32).reshape(h // 2, 2, nro, nrel).transpose(0, 2, 1, 3)
    toe = jnp.dot(rows.reshape(h * nro, nrel), onehot.reshape(nrel, w * w),
                  precision=lax.Precision.HIGHEST).reshape(h // 2, nro, 2, w, w)
    toe = toe + np.where(valid, 0.0, NEG_BIG).astype(np.float32)
    pairs = jnp.concatenate([toe[:, :nro - 1], toe[:, 1:]], axis=-1)
    return pairs.reshape(h // 2, nro - 1, 2 * w, 2 * w)


def _na(main, bias, bsz, seq, q_blk):
    n = main.shape[0]
    nrows = seq // GRID_W
    assert nrows >= NA_KH
    pairs = NA_HEADS // 2
    return pl.pallas_call(
        functools.partial(_na_kernel, nrows=nrows),
        grid=(bsz, pairs),
        in_specs=[pl.BlockSpec((seq, LANES), lambda b, p: (b, q_blk + p)),
                  pl.BlockSpec((seq, LANES), lambda b, p: (b, q_blk + pairs + p)),
                  pl.BlockSpec((seq, LANES), lambda b, p: (b, q_blk + 2 * pairs + p)),
                  pl.BlockSpec((1,) + bias.shape[1:], lambda b, p: (p, 0, 0, 0))],
        out_specs=pl.BlockSpec((seq, LANES), lambda b, p: (b, p)),
        out_shape=jax.ShapeDtypeStruct((n, NA_WIDTH), BF16),
        scratch_shapes=[pltpu.VMEM((2, 2 * GRID_W, NA_KH * GRID_W), F32),
                        pltpu.VMEM((2, 2 * GRID_W, NA_KH * GRID_W), BF16),
                        pltpu.VMEM((2, 2 * GRID_W, LANES), F32)],
        compiler_params=_cparams(("arbitrary", "arbitrary")),
        name="nbr_attn",
    )(main, main, main, bias)


ML_TAB = 16


def _mlstm_kernel(q_ref, k_ref, v_ref, og_ref, gt_ref, gbt_ref, hg_ref, o_ref,
                  rall_scr, ball_scr, a_scr, w_scr, sc_scr, cp_scr, c_st, n_st, *, nchunks):
    t = ML_CHUNK
    nc = nchunks
    head = pl.program_id(1)
    scale = ML_DK ** -0.5
    row = _iota((t, t), 0)
    col = _iota((t, t), 1)
    tris = (col <= row, col >= row)

    @pl.when((pl.program_id(0) == 0) & (head == 0))
    def _():
        rall_scr[...] = jnp.zeros_like(rall_scr)
        ball_scr[...] = jnp.zeros_like(ball_scr)
        for d in range(2):
            rall_scr[ML_TAB * d + 9:ML_TAB * d + 12, :] = jnp.ones((3, nc * t), F32)
            ball_scr[d, ML_TAB * d:ML_TAB * d + 3, :] = jnp.full((3, nc * t), -1.0, F32)

    lane = _iota((nc, t), 1)
    subc = _iota((nc, LANES), 0)
    sub32 = _iota((4 * ML_HEADS, 1), 0)

    def head_rows(kind):
        idx = kind * ML_HEADS + head
        bias = jnp.sum(jnp.where(sub32 == idx, gbt_ref[...], 0.0), axis=0, keepdims=True)
        return gt_ref[0, idx] + bias

    def scan_lanes(x, reverse, op, fill):
        sh = 1
        while sh < t:
            if reverse:
                x = op(x, jnp.where(lane < t - sh, pltpu.roll(x, t - sh, axis=1), fill))
            else:
                x = op(x, jnp.where(lane >= sh, pltpu.roll(x, sh, axis=1), fill))
            sh *= 2
        return x

    for d in range(2):
        ig = head_rows(2 * d)
        b = scan_lanes(_log_sigmoid(head_rows(2 * d + 1)), bool(d), jnp.add, 0.0)
        g_col = b[:, 0:1] if d else b[:, t - 1:t]
        a = g_col - b + ig
        m_col = jnp.max(a, axis=-1, keepdims=True)
        w = jnp.exp(a - m_col)
        rs = ig - b
        run_max = scan_lanes(rs, bool(d), jnp.maximum, NEG_BIG)
        g128 = jnp.broadcast_to(g_col, (nc, LANES))
        l128 = jnp.broadcast_to(m_col, (nc, LANES))
        m = jnp.zeros((1, LANES), F32)
        m_prev = jnp.zeros((nc, LANES), F32)
        s_old = jnp.zeros((nc, LANES), F32)
        s_new = jnp.zeros((nc, LANES), F32)
        for step in range(nc):
            cc = nc - 1 - step if d else step
            g_c = g128[cc:cc + 1, :]
            l_c = l128[cc:cc + 1, :]
            m_next = jnp.maximum(g_c + m, l_c)
            pick = subc == cc
            m_prev = jnp.where(pick, m, m_prev)
            s_old = jnp.where(pick, jnp.exp(g_c + m - m_next), s_old)
            s_new = jnp.where(pick, jnp.exp(l_c - m_next), s_new)
            m = m_next
        mp = m_prev[:, 0:1]
        u = jnp.maximum(mp, run_max)
        inter = jnp.exp(mp - u) * scale
        floor = jnp.exp(-(b + u))
        base = ML_TAB * d
        a_rows = [p.astype(F32) for x in (u, inter, floor) for p in _split3(x)]
        b_rows = [p.astype(F32) for p in _split3(rs)]
        for cc in range(nc):
            ls = slice(cc * t, (cc + 1) * t)
            for j, x in enumerate(a_rows):
                rall_scr[base + j:base + j + 1, ls] = x[cc:cc + 1, :]
            for j, x in enumerate(b_rows):
                ball_scr[d, base + 9 + j:base + 10 + j, ls] = x[cc:cc + 1, :]
            w_scr[d, :, ls] = jnp.broadcast_to(w[cc:cc + 1, :], (SUBLANES, t))
            sc_scr[d, cc, 0] = jnp.broadcast_to(s_old[cc:cc + 1, :], (SUBLANES, LANES))
            sc_scr[d, cc, 1] = jnp.broadcast_to(s_new[cc:cc + 1, :], (SUBLANES, LANES))

    def build_cols(c, carry):
        r0 = pl.multiple_of(c * t, t)
        rows = jnp.concatenate([rall_scr[:, pl.ds(r0, t)], jnp.zeros((LANES - 2 * ML_TAB, t), F32)], axis=0)
        a_scr[pl.ds(r0, t), :] = rows.T.astype(BF16)
        return carry

    lax.fori_loop(0, nc, build_cols, 0)

    c_st[...] = jnp.zeros_like(c_st)
    n_st[...] = jnp.zeros_like(n_st)

    def scan_step(i, carry):
        for d in range(2):
            c = nc - 1 - i if d else i
            r0 = pl.multiple_of(c * t, t)
            kw_t = k_ref[pl.ds(r0, t), :].astype(F32).T * w_scr[d, 0:1, pl.ds(r0, t)]
            s_loc = _dot(kw_t.astype(BF16), v_ref[pl.ds(r0, t), :])
            n_loc = jnp.sum(kw_t, axis=-1, keepdims=True)
            c_prev = c_st[d]
            n_prev = n_st[d]
            cp_scr[d, c] = jnp.concatenate([c_prev, n_prev], axis=1).astype(BF16)
            s_old = sc_scr[d, c, 0][0:1, :]
            s_new = sc_scr[d, c, 1][0:1, :]
            c_st[d] = (jnp.concatenate([s_old, s_old], axis=1) * c_prev
                       + jnp.concatenate([s_new, s_new], axis=1) * s_loc)
            n_st[d] = s_old * n_prev + s_new * n_loc
        return carry

    lax.fori_loop(0, nc, scan_step, 0, unroll=2)

    kk = _iota((LANES, 2 * LANES), 0)
    ll = _iota((LANES, 2 * LANES), 1)
    pad_rows = jnp.zeros((LANES - 2 * ML_TAB, t), BF16)

    def out_step(c, carry):
        r0 = pl.multiple_of(c * t, t)
        q = q_ref[pl.ds(r0, t), :]
        v = v_ref[pl.ds(r0, t), :]
        qk = _dot_nt(q, k_ref[pl.ds(r0, t), :]) * scale
        cols = a_scr[pl.ds(r0, t), :]
        hsum = None
        for d in range(2):
            base = ML_TAB * d
            b_arg = jnp.concatenate([ball_scr[d, :, pl.ds(r0, t)].astype(BF16), pad_rows], axis=0)
            arg = _dot(cols, b_arg)
            sc = qk * jnp.exp(jnp.where(tris[d], arg, NEG_BIG))
            pick = (((ll < LANES) & (kk >= base + 3) & (kk < base + 6))
                    | ((ll >= LANES) & (kk >= base + 6) & (kk < base + 9))).astype(BF16)
            rep = _dot(cols, pick)
            inter = rep[:, 0:LANES]
            ext = _dot(q, cp_scr[d, c])
            den = jnp.sum(sc, axis=-1, keepdims=True) + inter * ext[:, ML_DV:ML_DV + LANES]
            r = 1.0 / jnp.maximum(jnp.abs(den), rep[:, LANES:2 * LANES])
            num = _dot(sc.astype(BF16), v) + jnp.concatenate([inter, inter], axis=1) * ext[:, 0:ML_DV]
            hd = num * jnp.concatenate([r, r], axis=1)
            hsum = hd if hsum is None else hsum + hd
        og = og_ref[pl.ds(r0, t), :].astype(F32)
        o_ref[pl.ds(r0, t), :] = (_sigmoid(og) * (_rms(hsum) * hg_ref[0])).astype(o_ref.dtype)
        return carry

    lax.fori_loop(0, nc, out_step, 0)


def _mlstm(main, gates_t, gate_bt, head_g, bsz, seq):
    n = main.shape[0]
    t = ML_CHUNK
    nc = seq // t
    nh = ML_HEADS
    r = gates_t.shape[0]
    gates_c = gates_t.reshape(r, bsz, nc, t).transpose(1, 0, 2, 3)
    return pl.pallas_call(
        functools.partial(_mlstm_kernel, nchunks=nc),
        grid=(bsz, nh),
        in_specs=[pl.BlockSpec((seq, ML_DK), lambda b, h: (b, h)),
                  pl.BlockSpec((seq, ML_DK), lambda b, h: (b, nh + h)),
                  pl.BlockSpec((seq, ML_DV), lambda b, h: (b, nh + h)),
                  pl.BlockSpec((seq, ML_DV), lambda b, h: (b, 2 * nh + h)),
                  pl.BlockSpec((1, r, nc, t), lambda b, h: (b, 0, 0, 0)),
                  pl.BlockSpec((r, 1), lambda b, h: (0, 0)),
                  pl.BlockSpec((1, 1, ML_DV), lambda b, h: (h, 0, 0))],
        out_specs=pl.BlockSpec((seq, ML_DV), lambda b, h: (b, h)),
        out_shape=jax.ShapeDtypeStruct((n, ML_WIDTH), BF16),
        scratch_shapes=[pltpu.VMEM((2 * ML_TAB, seq), F32),
                        pltpu.VMEM((2, 2 * ML_TAB, seq), F32),
                        pltpu.VMEM((seq, LANES), BF16),
                        pltpu.VMEM((2, SUBLANES, seq), F32),
                        pltpu.VMEM((2, nc, 2, SUBLANES, LANES), F32),
                        pltpu.VMEM((2, nc, ML_DK, ML_DV + LANES), BF16),
                        pltpu.VMEM((2, ML_DK, ML_DV), F32),
                        pltpu.VMEM((2, ML_DK, LANES), F32)],
        compiler_params=_cparams(("arbitrary", "arbitrary")),
        name="mlstm",
    )(main, main, main, main, gates_c, gate_bt, head_g.reshape(nh, 1, ML_DV))


def _pad_cols(a, width):
    return jnp.pad(a, ((0, 0), (0, width - a.shape[1])))


def _ssd_na_layer(x2, mods, norm_g, w_in, conv_w, conv_b, dt_bias, a_log, d_skip, ssd_norm, rpb, w_out,
                  bsz, seq):
    sh1, sc1, g1 = mods
    s1 = SSD_WIDTH
    s2 = s1 + SSD_CONV_CH
    s3 = s2 + 2 * SSD_HEADS
    w_main = jnp.concatenate([w_in[:, :s2], w_in[:, s3:]], axis=1).astype(BF16)
    w_dt = _pad_cols(w_in[:, s2:s3], LANES).astype(BF16)
    main, dtf = _project(x2, norm_g[0], sh1, sc1, w_main, w_dt, False, seq, tm=min(512, seq), tn=1408,
                         name="proj_ssd_na")

    conv = _conv_silu(main, s1, conv_w, conv_b, seq, tr=min(512, seq), tc=512)
    dtb = _pad_cols(dt_bias.reshape(1, 2 * SSD_HEADS), LANES)
    alog = _pad_cols(a_log.reshape(1, 2 * SSD_HEADS), LANES)
    dskip = jnp.repeat(d_skip, SSD_HEAD_DIM)[None, :]
    y_f = _ssd_pass(conv, dtf, dtb, alog, bsz, seq, reverse=False)
    y_ssd = _ssd_pass(conv, dtf, dtb, alog, bsz, seq, reverse=True,
                      extra=(main, y_f, dskip, ssd_norm[None, :]))

    q_blk = (s1 + SSD_CONV_CH) // LANES
    y_na = _na(main, _na_bias(rpb), bsz, seq, q_blk)
    return _out_project(y_ssd, 0, y_na, 0, w_out.astype(BF16), x2, g1, norm_g[1], seq, tm=min(512, seq),
                        name="out_ssd_na")


def _mlstm_layer(x2, mods, norm_g, w_in, gate_b, head_g, w_out, bsz, seq):
    sh1, sc1, g1 = mods
    wm = 2 * ML_QK + 2 * ML_WIDTH
    w_main = w_in[:, :wm].astype(BF16)
    main, gates_t = _project(x2, norm_g[0], sh1, sc1, w_main, w_in[:, wm:].T.astype(BF16), True, seq,
                             tm=min(512, seq), tn=1536, name="proj_mlstm")
    y = _mlstm(main, gates_t, gate_b.reshape(4 * ML_HEADS, 1).astype(F32), head_g, bsz, seq)
    return _out_project(y, 0, y, 1, w_out.astype(BF16), x2, g1, norm_g[1], seq, tm=min(512, seq),
                        name="out_mlstm")


def kernel(x, c, ada_w, ada_b, norm_g, mlp_w1, mlp_w2, ab_w_in, ab_conv_w, ab_conv_b, ab_dt_bias, ab_a_log,
           ab_d_skip, ab_ssd_norm, ab_rpb, ab_w_out, ml_w_in, ml_gate_b, ml_head_norm, ml_w_out):
    bsz, seq, d = x.shape
    depth = ada_w.shape[0]
    mod = _adaln(c, ada_w, ada_b)
    x2 = x.reshape(bsz * seq, d)
    for layer in range(depth):
        sh1, sc1, g1, sh2, sc2, g2 = [mod[layer, :, i * d:(i + 1) * d] for i in range(6)]
        j = layer // 2
        if layer % 2 == 0:
            x2 = _ssd_na_layer(x2, (sh1, sc1, g1), norm_g[layer], ab_w_in[j], ab_conv_w[j], ab_conv_b[j],
                               ab_dt_bias[j], ab_a_log[j], ab_d_skip[j], ab_ssd_norm[j], ab_rpb[j],
                               ab_w_out[j], bsz, seq)
        else:
            x2 = _mlstm_layer(x2, (sh1, sc1, g1), norm_g[layer], ml_w_in[j], ml_gate_b[j], ml_head_norm[j],
                              ml_w_out[j], bsz, seq)
        x2 = _mlp(x2, norm_g[layer, 2], sh2, sc2, mlp_w1[layer].astype(BF16), mlp_w2[layer].astype(BF16),
                  g2, norm_g[layer, 3], seq, tm=min(1024, seq), tf=1024, name="mlp%d" % layer)
    return x2.reshape(bsz, seq, d)
```

```python
import functools

import numpy as np
import jax
import jax.numpy as jnp
from jax import lax
from jax.experimental import pallas as pl
from jax.experimental.pallas import tpu as pltpu

F32 = jnp.float32
BF16 = jnp.bfloat16

NORM_EPS = 1e-6
GRID_W = 64

SSD_HEAD_DIM = 64
SSD_HEADS = 16
SSD_GROUPS = 2
SSD_STATE = 128
SSD_CONV = 5
SSD_CHUNK = 128
SSD_WIDTH = SSD_HEADS * SSD_HEAD_DIM
SSD_BC = 2 * SSD_GROUPS * SSD_STATE
SSD_CONV_CH = SSD_WIDTH + SSD_BC

NA_HEAD_DIM = 64
NA_HEADS = 16
NA_KH = 8
NA_KW = 16
NA_WIDTH = NA_HEADS * NA_HEAD_DIM

ML_HEADS = 8
ML_DV = 256
ML_DK = 128
ML_CHUNK = 256
ML_WIDTH = ML_HEADS * ML_DV
ML_QK = ML_HEADS * ML_DK

LANES = 128
SUBLANES = 8
HALO = 16
NEG_BIG = -1e30
VMEM_LIMIT = 56 * 1024 * 1024


def _cparams(sem):
    return pltpu.CompilerParams(dimension_semantics=sem, vmem_limit_bytes=VMEM_LIMIT)


def _dot(a, b):
    return jnp.dot(a, b, preferred_element_type=F32)


def _dot_nt(a, b):
    return lax.dot_general(a, b, (((1,), (1,)), ((), ())), preferred_element_type=F32)


def _split3(x):
    hi = x.astype(BF16)
    r1 = x - hi.astype(F32)
    mid = r1.astype(BF16)
    lo = (r1 - mid.astype(F32)).astype(BF16)
    return hi, mid, lo


def _dot_exact_lhs01(sel, x):
    hi, mid, lo = _split3(x)
    return _dot(sel, hi) + _dot(sel, mid) + _dot(sel, lo)


def _iota(shape, dim):
    return lax.broadcasted_iota(jnp.int32, shape, dim)


def _sigmoid(x):
    return 0.5 * (jnp.tanh(0.5 * x) + 1.0)


def _softplus(x):
    return jnp.maximum(x, 0.0) + jnp.log(1.0 + jnp.exp(-jnp.abs(x)))


def _log_sigmoid(x):
    return jnp.minimum(x, 0.0) - jnp.log(1.0 + jnp.exp(-jnp.abs(x)))


def _rms(x):
    return x * lax.rsqrt(jnp.mean(x * x, axis=-1, keepdims=True) + NORM_EPS)


def _mod_kernel(c_ref, w_ref, b_ref, o_ref):
    c = c_ref[...]
    cond = c * _sigmoid(c)
    o_ref[0] = jnp.dot(cond, w_ref[0], preferred_element_type=F32,
                       precision=lax.Precision.HIGHEST) + b_ref[0]


def _adaln(c, ada_w, ada_b):
    depth, d, d6 = ada_w.shape
    bsz = c.shape[0]
    tn = 1024
    return pl.pallas_call(
        _mod_kernel,
        grid=(depth, d6 // tn),
        in_specs=[pl.BlockSpec((bsz, d), lambda l, j: (0, 0)),
                  pl.BlockSpec((1, d, tn), lambda l, j: (l, 0, j)),
                  pl.BlockSpec((1, 1, tn), lambda l, j: (l, 0, j))],
        out_specs=pl.BlockSpec((1, bsz, tn), lambda l, j: (l, 0, j)),
        out_shape=jax.ShapeDtypeStruct((depth, bsz, d6), F32),
        compiler_params=_cparams(("arbitrary", "arbitrary")),
        name="adaln",
    )(c, ada_w, ada_b.reshape(depth, 1, d6))


def _proj_kernel(x_ref, g_ref, sh_ref, sc_ref, w_ref, ws_ref, o_ref, os_ref, *, small_transposed, tn):
    hb = (_rms(x_ref[...]) * g_ref[...] * (1.0 + sc_ref[0]) + sh_ref[0]).astype(BF16)
    if small_transposed:
        os_ref[...] = _dot_nt(ws_ref[...], hb)
    else:
        os_ref[...] = _dot(hb, ws_ref[...])
    for j in range(w_ref.shape[1] // tn):
        o_ref[:, j * tn:(j + 1) * tn] = _dot(hb, w_ref[:, j * tn:(j + 1) * tn]).astype(o_ref.dtype)


def _project(x2, g, shift, scale, w, ws, small_transposed, seq, tm, tn, name):
    n, d = x2.shape
    wn = w.shape[1]
    tiles_per_batch = seq // tm
    bsz = shift.shape[0]
    if small_transposed:
        r = ws.shape[0]
        small_spec, small_shape = pl.BlockSpec((r, tm), lambda i: (0, i)), (r, n)
    else:
        small_spec, small_shape = pl.BlockSpec((tm, LANES), lambda i: (i, 0)), (n, LANES)
    resident = pl.Buffered(1)
    return pl.pallas_call(
        functools.partial(_proj_kernel, small_transposed=small_transposed, tn=tn),
        grid=(n // tm,),
        in_specs=[pl.BlockSpec((tm, d), lambda i: (i, 0)),
                  pl.BlockSpec((1, d), lambda i: (0, 0)),
                  pl.BlockSpec((1, 1, d), lambda i: (i // tiles_per_batch, 0, 0)),
                  pl.BlockSpec((1, 1, d), lambda i: (i // tiles_per_batch, 0, 0)),
                  pl.BlockSpec((d, wn), lambda i: (0, 0), pipeline_mode=resident),
                  pl.BlockSpec(ws.shape, lambda i: (0, 0), pipeline_mode=resident)],
        out_specs=[pl.BlockSpec((tm, wn), lambda i: (i, 0)), small_spec],
        out_shape=[jax.ShapeDtypeStruct((n, wn), BF16), jax.ShapeDtypeStruct(small_shape, F32)],
        compiler_params=_cparams(("arbitrary",)),
        name=name,
    )(x2, g.reshape(1, d), shift.reshape(bsz, 1, d), scale.reshape(bsz, 1, d), w, ws)


def _tail_kernel(ya_ref, yb_ref, wa_ref, wb_ref, x_ref, mod_ref, ng_ref, w1_ref, w2_ref, o_ref, *, tf):
    mixed = _dot(ya_ref[...], wa_ref[...]) + _dot(yb_ref[...], wb_ref[...])
    x = x_ref[...] + mod_ref[0, 0:1, :] * (_rms(mixed) * ng_ref[0:1, :])
    h = (_rms(x) * ng_ref[1:2, :] * (1.0 + mod_ref[0, 2:3, :]) + mod_ref[0, 1:2, :]).astype(BF16)
    u = None
    for f in range(w1_ref.shape[1] // tf):
        a = jnp.maximum(_dot(h, w1_ref[:, f * tf:(f + 1) * tf]), 0.0)
        part = _dot((a * a).astype(BF16), w2_ref[f * tf:(f + 1) * tf, :])
        u = part if u is None else u + part
    o_ref[...] = x + mod_ref[0, 3:4, :] * (_rms(u) * ng_ref[2:3, :])


def _tail(ya, ia, yb, ib, w_out, x2, mods, gains, w1, w2, seq, tm, tf, name):
    n, d = x2.shape
    kh = w_out.shape[0] // 2
    dff = w1.shape[1]
    tiles_per_batch = seq // tm
    resident = pl.Buffered(1)
    return pl.pallas_call(
        functools.partial(_tail_kernel, tf=tf),
        grid=(n // tm,),
        in_specs=[pl.BlockSpec((tm, kh), lambda i: (i, ia)),
                  pl.BlockSpec((tm, kh), lambda i: (i, ib)),
                  pl.BlockSpec((kh, d), lambda i: (0, 0), pipeline_mode=resident),
                  pl.BlockSpec((kh, d), lambda i: (1, 0), pipeline_mode=resident),
                  pl.BlockSpec((tm, d), lambda i: (i, 0)),
                  pl.BlockSpec((1, 4, d), lambda i: (i // tiles_per_batch, 0, 0)),
                  pl.BlockSpec((3, d), lambda i: (0, 0)),
                  pl.BlockSpec((d, dff), lambda i: (0, 0), pipeline_mode=resident),
                  pl.BlockSpec((dff, d), lambda i: (0, 0), pipeline_mode=resident)],
        out_specs=pl.BlockSpec((tm, d), lambda i: (i, 0)),
        out_shape=jax.ShapeDtypeStruct((n, d), F32),
        compiler_params=_cparams(("arbitrary",)),
        name=name,
    )(ya, yb, w_out, w_out, x2, mods, gains, w1, w2)


def _conv_kernel(main_ref, prev_ref, next_ref, w_ref, b_ref, o_ref, scr, *, blocks_per_seq):
    t = main_ref.shape[0]
    i = pl.program_id(0) % blocks_per_seq
    keep_prev = (i > 0).astype(F32)
    keep_next = (i < blocks_per_seq - 1).astype(F32)
    scr[0:HALO, :] = prev_ref[...].astype(F32) * keep_prev
    scr[HALO:HALO + t, :] = main_ref[...].astype(F32)
    scr[HALO + t:2 * HALO + t, :] = next_ref[...].astype(F32) * keep_next
    pad = SSD_CONV // 2
    acc = b_ref[...] + w_ref[0:1, :] * scr[HALO - pad:HALO - pad + t, :]
    for k in range(1, SSD_CONV):
        acc = acc + w_ref[k:k + 1, :] * scr[HALO - pad + k:HALO - pad + k + t, :]
    o_ref[...] = (acc * _sigmoid(acc)).astype(o_ref.dtype)


def _conv_silu(main, col0, conv_w, conv_b, seq, tr, tc):
    n = main.shape[0]
    ch = conv_w.shape[1]
    cb0 = col0 // tc
    rpb = tr // HALO
    last_halo = n // HALO - 1
    return pl.pallas_call(
        functools.partial(_conv_kernel, blocks_per_seq=seq // tr),
        grid=(n // tr, ch // tc),
        in_specs=[pl.BlockSpec((tr, tc), lambda i, j: (i, cb0 + j)),
                  pl.BlockSpec((HALO, tc), lambda i, j: (jnp.maximum(i * rpb - 1, 0), cb0 + j)),
                  pl.BlockSpec((HALO, tc), lambda i, j: (jnp.minimum((i + 1) * rpb, last_halo), cb0 + j)),
                  pl.BlockSpec((SSD_CONV, tc), lambda i, j: (0, j)),
                  pl.BlockSpec((1, tc), lambda i, j: (0, j))],
        out_specs=pl.BlockSpec((tr, tc), lambda i, j: (i, j)),
        out_shape=jax.ShapeDtypeStruct((n, ch), BF16),
        scratch_shapes=[pltpu.VMEM((tr + 2 * HALO, tc), F32)],
        compiler_params=_cparams(("arbitrary", "arbitrary")),
        name="conv_silu",
    )(main, main, main, conv_w, conv_b.reshape(1, ch))


def _ssd_kernel(xs_ref, bc_ref, dt_ref, dtb_ref, alog_ref, *rest, reverse):
    if reverse:
        z_ref, yf_ref, dskip_ref, ng_ref, o_ref, st_scr = rest
    else:
        o_ref, st_scr = rest
    t = SSD_CHUNK
    hd = SSD_HEAD_DIM
    hpg = SSD_HEADS // SSD_GROUPS

    @pl.when(pl.program_id(1) == 0)
    def _():
        st_scr[...] = jnp.zeros_like(st_scr)

    xs = xs_ref[...].astype(F32)
    bc = bc_ref[...]

    dt = _softplus(dt_ref[...] + dtb_ref[...])
    adt = dt * (-jnp.exp(alog_ref[...]))
    row = _iota((t, t), 0)
    col = _iota((t, t), 1)
    tri = (col >= row) if reverse else (col <= row)
    cs = _dot_exact_lhs01(tri.astype(BF16), adt)
    cs_t = cs.T
    dt_t = dt.T
    base = SSD_HEADS if reverse else 0
    last = 0 if reverse else t - 1

    first_half = _iota((t, LANES), 1) < hd
    ys = []
    for g in range(SSD_GROUPS):
        bg = bc[:, g * SSD_STATE:(g + 1) * SSD_STATE]
        cg = bc[:, (SSD_GROUPS + g) * SSD_STATE:(SSD_GROUPS + g + 1) * SSD_STATE]
        cb = _dot_nt(cg, bg)
        bg_t = bg.astype(F32).T
        s_prev = st_scr[g]
        y_off = _dot(cg, s_prev.astype(BF16))
        s_new = []
        etot = []
        for pr in range(hpg // 2):
            pair = g * (hpg // 2) + pr
            xpair = xs[:, pair * LANES:(pair + 1) * LANES]
            cs_cols = []
            acc = None
            for sub in range(2):
                j = base + 2 * pair + sub
                cs_col = jnp.broadcast_to(cs[:, j:j + 1], (t, t))
                cs_cols.append(cs_col)
                dts = dt_t[j:j + 1, :]
                lmat = jnp.exp(jnp.where(tri, cs_col - cs_t[j:j + 1, :], NEG_BIG))
                m = (cb * lmat * dts).astype(BF16)
                keep = first_half if sub == 0 else jnp.logical_not(first_half)
                part = _dot(m, jnp.where(keep, xpair, 0.0).astype(BF16))
                acc = part if acc is None else acc + part
            cs_pair = jnp.where(first_half, cs_cols[0], cs_cols[1])
            tot_pair = cs_pair[last:last + 1, :]
            xd = (xpair * jnp.exp(tot_pair - cs_pair)).astype(BF16)
            sp = None
            for sub in range(2):
                j = base + 2 * pair + sub
                keep = first_half if sub == 0 else jnp.logical_not(first_half)
                part = _dot((bg_t * dt_t[j:j + 1, :]).astype(BF16), jnp.where(keep, xd, jnp.zeros((), BF16)))
                sp = part if sp is None else sp + part
            ys.append(acc + y_off[:, pr * LANES:(pr + 1) * LANES] * jnp.exp(cs_pair))
            s_new.append(sp)
            etot.append(jnp.exp(tot_pair))
        st_scr[g] = s_prev * jnp.concatenate(etot, axis=1) + jnp.concatenate(s_new, axis=1)
    y = jnp.concatenate(ys, axis=1)

    if reverse:
        y = y + yf_ref[...] + dskip_ref[...] * xs
        z = z_ref[...].astype(F32)
        y = y * (z * _sigmoid(z))
        o_ref[...] = (_rms(y) * ng_ref[...]).astype(o_ref.dtype)
    else:
        o_ref[...] = y


def _ssd_pass(conv, dtf, dtb, alog, bsz, seq, reverse, extra=None):
    n = conv.shape[0]
    t = SSD_CHUNK
    nc = seq // t
    bc_blk = SSD_WIDTH // SSD_BC

    def rows(b, c):
        return b * nc + (nc - 1 - c if reverse else c)

    in_specs = [pl.BlockSpec((t, SSD_WIDTH), lambda b, c: (rows(b, c), 0)),
                pl.BlockSpec((t, SSD_BC), lambda b, c: (rows(b, c), bc_blk)),
                pl.BlockSpec((t, LANES), lambda b, c: (rows(b, c), 0)),
                pl.BlockSpec((1, LANES), lambda b, c: (0, 0)),
                pl.BlockSpec((1, LANES), lambda b, c: (0, 0))]
    args = [conv, conv, dtf, dtb, alog]
    if reverse:
        main, yf, dskip, ng = extra
        in_specs += [pl.BlockSpec((t, SSD_WIDTH), lambda b, c: (rows(b, c), 0)),
                     pl.BlockSpec((t, SSD_WIDTH), lambda b, c: (rows(b, c), 0)),
                     pl.BlockSpec((1, SSD_WIDTH), lambda b, c: (0, 0)),
                     pl.BlockSpec((1, SSD_WIDTH), lambda b, c: (0, 0))]
        args += [main, yf, dskip, ng]
        out_dtype = BF16
    else:
        out_dtype = F32
    return pl.pallas_call(
        functools.partial(_ssd_kernel, reverse=reverse),
        grid=(bsz, nc),
        in_specs=in_specs,
        out_specs=pl.BlockSpec((t, SSD_WIDTH), lambda b, c: (rows(b, c), 0)),
        out_shape=jax.ShapeDtypeStruct((n, SSD_WIDTH), out_dtype),
        scratch_shapes=[pltpu.VMEM((SSD_GROUPS, SSD_STATE, SSD_WIDTH // SSD_GROUPS), F32)],
        compiler_params=_cparams(("arbitrary", "arbitrary")),
        name="ssd_bwd" if reverse else "ssd_fwd",
    )(*args)


def _na_kernel(q_ref, k_ref, v_ref, bias_ref, o_ref, s_scr, p_scr, l_scr, *, nrows):
    w = GRID_W
    band = NA_KH * w
    first = _iota((w, LANES), 1) < NA_HEAD_DIM
    zero = jnp.zeros((), BF16)

    def band_start(r):
        return jnp.clip(r - NA_KH // 2, 0, nrows - NA_KH)

    def scores(r, slot):
        rs = band_start(r)
        q = q_ref[pl.ds(pl.multiple_of(r * w, w), w), :] * jnp.asarray(NA_HEAD_DIM ** -0.5, BF16)
        qs = jnp.concatenate([jnp.where(first, q, zero), jnp.where(first, zero, q)], axis=0)
        kb = k_ref[pl.ds(pl.multiple_of(rs * w, w), band), :]
        j0 = NA_KH - 1 - (r - rs)
        bias = jnp.concatenate([bias_ref[0, j0 + 2 * a] for a in range(NA_KH // 2)], axis=1)
        s_scr[slot] = _dot_nt(qs, kb) + bias

    def softmax(slot):
        s = s_scr[slot]
        p = jnp.exp(s - jnp.max(s, axis=-1, keepdims=True))
        l_scr[slot] = jnp.broadcast_to(jnp.sum(p, axis=-1, keepdims=True), (2 * w, LANES))
        p_scr[slot] = p.astype(BF16)

    def values(r, slot):
        rs = band_start(r)
        vb = v_ref[pl.ds(pl.multiple_of(rs * w, w), band), :]
        o = _dot(p_scr[slot], vb) / l_scr[slot]
        o_ref[pl.ds(pl.multiple_of(r * w, w), w), :] = jnp.where(first, o[0:w], o[w:2 * w]).astype(o_ref.dtype)

    s_scr[...] = jnp.zeros_like(s_scr)
    p_scr[...] = jnp.zeros_like(p_scr)
    l_scr[...] = jnp.ones_like(l_scr)

    def body(i2, carry):
        for slot in range(2):
            i = 2 * i2 + slot
            values(jnp.maximum(i - 2, 0), slot)
            softmax(1 - slot)
            scores(jnp.minimum(i, nrows - 1), slot)
        return carry

    lax.fori_loop(0, nrows // 2 + 1, body, 0)


def _na_bias(rpb):
    w = GRID_W
    h, nro, nrel = rpb.shape
    c = np.arange(w)
    kc = np.arange(w)
    col_start = np.clip(c - NA_KW // 2, 0, w - NA_KW)
    valid = (kc[None, :] >= col_start[:, None]) & (kc[None, :] < col_start[:, None] + NA_KW)
    co = kc[None, :] - c[:, None] + NA_KW - 1
    onehot = ((np.arange(nrel)[:, None, None] == co[None]) & valid[None]).astype(np.float32)
    rows = rpb.astype(F32).reshape(h // 2, 2, nro, nrel).transpose(0, 2, 1, 3)
    toe = jnp.dot(rows.reshape(h * nro, nrel), onehot.reshape(nrel, w * w),
                  precision=lax.Precision.HIGHEST).reshape(h // 2, nro, 2, w, w)
    toe = toe + np.where(valid, 0.0, NEG_BIG).astype(np.float32)
    pairs = jnp.concatenate([toe[:, :nro - 1], toe[:, 1:]], axis=-1)
    return pairs.reshape(h // 2, nro - 1, 2 * w, 2 * w)


def _na(main, bias, bsz, seq, q_blk):
    n = main.shape[0]
    nrows = seq // GRID_W
    assert nrows >= NA_KH
    pairs = NA_HEADS // 2
    return pl.pallas_call(
        functools.partial(_na_kernel, nrows=nrows),
        grid=(bsz, pairs),
        in_specs=[pl.BlockSpec((seq, LANES), lambda b, p: (b, q_blk + p)),
                  pl.BlockSpec((seq, LANES), lambda b, p: (b, q_blk + pairs + p)),
                  pl.BlockSpec((seq, LANES), lambda b, p: (b, q_blk + 2 * pairs + p)),
                  pl.BlockSpec((1,) + bias.shape[1:], lambda b, p: (p, 0, 0, 0))],
        out_specs=pl.BlockSpec((seq, LANES), lambda b, p: (b, p)),
        out_shape=jax.ShapeDtypeStruct((n, NA_WIDTH), BF16),
        scratch_shapes=[pltpu.VMEM((2, 2 * GRID_W, NA_KH * GRID_W), F32),
                        pltpu.VMEM((2, 2 * GRID_W, NA_KH * GRID_W), BF16),
                        pltpu.VMEM((2, 2 * GRID_W, LANES), F32)],
        compiler_params=_cparams(("arbitrary", "arbitrary")),
        name="nbr_attn",
    )(main, main, main, bias)


ML_TAB = 16


def _mlstm_kernel(q_ref, k_ref, v_ref, og_ref, gt_ref, gbt_ref, hg_ref, o_ref,
                  rall_scr, ball_scr, a_scr, w_scr, sc_scr, cp_scr, c_st, n_st, *, nchunks):
    t = ML_CHUNK
    nc = nchunks
    head = pl.program_id(1)
    scale = ML_DK ** -0.5
    row = _iota((t, t), 0)
    col = _iota((t, t), 1)
    tris = (col <= row, col >= row)

    @pl.when((pl.program_id(0) == 0) & (head == 0))
    def _():
        rall_scr[...] = jnp.zeros_like(rall_scr)
        ball_scr[...] = jnp.zeros_like(ball_scr)
        for d in range(2):
            rall_scr[ML_TAB * d + 9:ML_TAB * d + 12, :] = jnp.ones((3, nc * t), F32)
            ball_scr[d, ML_TAB * d:ML_TAB * d + 3, :] = jnp.full((3, nc * t), -1.0, F32)

    lane = _iota((nc, t), 1)
    subc = _iota((nc, LANES), 0)
    sub32 = _iota((4 * ML_HEADS, 1), 0)

    def head_rows(kind):
        idx = kind * ML_HEADS + head
        bias = jnp.sum(jnp.where(sub32 == idx, gbt_ref[...], 0.0), axis=0, keepdims=True)
        return gt_ref[0, idx] + bias

    def scan_lanes(x, reverse, op, fill):
        sh = 1
        while sh < t:
            if reverse:
                x = op(x, jnp.where(lane < t - sh, pltpu.roll(x, t - sh, axis=1), fill))
            else:
                x = op(x, jnp.where(lane >= sh, pltpu.roll(x, sh, axis=1), fill))
            sh *= 2
        return x

    for d in range(2):
        ig = head_rows(2 * d)
        b = scan_lanes(_log_sigmoid(head_rows(2 * d + 1)), bool(d), jnp.add, 0.0)
        g_col = b[:, 0:1] if d else b[:, t - 1:t]
        a = g_col - b + ig
        m_col = jnp.max(a, axis=-1, keepdims=True)
        w = jnp.exp(a - m_col)
        rs = ig - b
        run_max = scan_lanes(rs, bool(d), jnp.maximum, NEG_BIG)
        g128 = jnp.broadcast_to(g_col, (nc, LANES))
        l128 = jnp.broadcast_to(m_col, (nc, LANES))
        m = jnp.zeros((1, LANES), F32)
        m_prev = jnp.zeros((nc, LANES), F32)
        s_old = jnp.zeros((nc, LANES), F32)
        s_new = jnp.zeros((nc, LANES), F32)
        for step in range(nc):
            cc = nc - 1 - step if d else step
            g_c = g128[cc:cc + 1, :]
            l_c = l128[cc:cc + 1, :]
            m_next = jnp.maximum(g_c + m, l_c)
            pick = subc == cc
            m_prev = jnp.where(pick, m, m_prev)
            s_old = jnp.where(pick, jnp.exp(g_c + m - m_next), s_old)
            s_new = jnp.where(pick, jnp.exp(l_c - m_next), s_new)
            m = m_next
        mp = m_prev[:, 0:1]
        u = jnp.maximum(mp, run_max)
        inter = jnp.exp(mp - u) * scale
        floor = jnp.exp(-(b + u))
        base = ML_TAB * d
        a_rows = [p.astype(F32) for x in (u, inter, floor) for p in _split3(x)]
        b_rows = [p.astype(F32) for p in _split3(rs)]
        for cc in range(nc):
            ls = slice(cc * t, (cc + 1) * t)
            for j, x in enumerate(a_rows):
                rall_scr[base + j:base + j + 1, ls] = x[cc:cc + 1, :]
            for j, x in enumerate(b_rows):
                ball_scr[d, base + 9 + j:base + 10 + j, ls] = x[cc:cc + 1, :]
            w_scr[d, :, ls] = jnp.broadcast_to(w[cc:cc + 1, :], (SUBLANES, t))
            sc_scr[d, cc, 0] = jnp.broadcast_to(s_old[cc:cc + 1, :], (SUBLANES, LANES))
            sc_scr[d, cc, 1] = jnp.broadcast_to(s_new[cc:cc + 1, :], (SUBLANES, LANES))

    def build_cols(c, carry):
        r0 = pl.multiple_of(c * t, t)
        rows = jnp.concatenate([rall_scr[:, pl.ds(r0, t)], jnp.zeros((LANES - 2 * ML_TAB, t), F32)], axis=0)
        a_scr[pl.ds(r0, t), :] = rows.T.astype(BF16)
        return carry

    lax.fori_loop(0, nc, build_cols, 0)

    c_st[...] = jnp.zeros_like(c_st)
    n_st[...] = jnp.zeros_like(n_st)

    def scan_step(i, carry):
        for d in range(2):
            c = nc - 1 - i if d else i
            r0 = pl.multiple_of(c * t, t)
            kw_t = k_ref[pl.ds(r0, t), :].astype(F32).T * w_scr[d, 0:1, pl.ds(r0, t)]
            s_loc = _dot(kw_t.astype(BF16), v_ref[pl.ds(r0, t), :])
            n_loc = jnp.sum(kw_t, axis=-1, keepdims=True)
            c_prev = c_st[d]
            n_prev = n_st[d]
            cp_scr[d, c] = jnp.concatenate([c_prev, n_prev], axis=1).astype(BF16)
            s_old = sc_scr[d, c, 0][0:1, :]
            s_new = sc_scr[d, c, 1][0:1, :]
            c_st[d] = (jnp.concatenate([s_old, s_old], axis=1) * c_prev
                       + jnp.concatenate([s_new, s_new], axis=1) * s_loc)
            n_st[d] = s_old * n_prev + s_new * n_loc
        return carry

    lax.fori_loop(0, nc, scan_step, 0, unroll=2)

    kk = _iota((LANES, 2 * LANES), 0)
    ll = _iota((LANES, 2 * LANES), 1)
    pad_rows = jnp.zeros((LANES - 2 * ML_TAB, t), BF16)

    def out_step(c, carry):
        r0 = pl.multiple_of(c * t, t)
        q = q_ref[pl.ds(r0, t), :]
        v = v_ref[pl.ds(r0, t), :]
        qk = _dot_nt(q, k_ref[pl.ds(r0, t), :]) * scale
        cols = a_scr[pl.ds(r0, t), :]
        hsum = None
        for d in range(2):
            base = ML_TAB * d
            b_arg = jnp.concatenate([ball_scr[d, :, pl.ds(r0, t)].astype(BF16), pad_rows], axis=0)
            arg = _dot(cols, b_arg)
            sc = qk * jnp.exp(jnp.where(tris[d], arg, NEG_BIG))
            pick = (((ll < LANES) & (kk >= base + 3) & (kk < base + 6))
                    | ((ll >= LANES) & (kk >= base + 6) & (kk < base + 9))).astype(BF16)
            rep = _dot(cols, pick)
            inter = rep[:, 0:LANES]
            ext = _dot(q, cp_scr[d, c])
            den = jnp.sum(sc, axis=-1, keepdims=True) + inter * ext[:, ML_DV:ML_DV + LANES]
            r = 1.0 / jnp.maximum(jnp.abs(den), rep[:, LANES:2 * LANES])
            num = _dot(sc.astype(BF16), v) + jnp.concatenate([inter, inter], axis=1) * ext[:, 0:ML_DV]
            hd = num * jnp.concatenate([r, r], axis=1)
            hsum = hd if hsum is None else hsum + hd
        og = og_ref[pl.ds(r0, t), :].astype(F32)
        o_ref[pl.ds(r0, t), :] = (_sigmoid(og) * (_rms(hsum) * hg_ref[0])).astype(o_ref.dtype)
        return carry

    lax.fori_loop(0, nc, out_step, 0)


def _mlstm(main, gates_t, gate_bt, head_g, bsz, seq):
    n = main.shape[0]
    t = ML_CHUNK
    nc = seq // t
    nh = ML_HEADS
    r = gates_t.shape[0]
    gates_c = gates_t.reshape(r, bsz, nc, t).transpose(1, 0, 2, 3)
    return pl.pallas_call(
        functools.partial(_mlstm_kernel, nchunks=nc),
        grid=(bsz, nh),
        in_specs=[pl.BlockSpec((seq, ML_DK), lambda b, h: (b, h)),
                  pl.BlockSpec((seq, ML_DK), lambda b, h: (b, nh + h)),
                  pl.BlockSpec((seq, ML_DV), lambda b, h: (b, nh + h)),
                  pl.BlockSpec((seq, ML_DV), lambda b, h: (b, 2 * nh + h)),
                  pl.BlockSpec((1, r, nc, t), lambda b, h: (b, 0, 0, 0)),
                  pl.BlockSpec((r, 1), lambda b, h: (0, 0)),
                  pl.BlockSpec((1, 1, ML_DV), lambda b, h: (h, 0, 0))],
        out_specs=pl.BlockSpec((seq, ML_DV), lambda b, h: (b, h)),
        out_shape=jax.ShapeDtypeStruct((n, ML_WIDTH), BF16),
        scratch_shapes=[pltpu.VMEM((2 * ML_TAB, seq), F32),
                        pltpu.VMEM((2, 2 * ML_TAB, seq), F32),
                        pltpu.VMEM((seq, LANES), BF16),
                        pltpu.VMEM((2, SUBLANES, seq), F32),
                        pltpu.VMEM((2, nc, 2, SUBLANES, LANES), F32),
                        pltpu.VMEM((2, nc, ML_DK, ML_DV + LANES), BF16),
                        pltpu.VMEM((2, ML_DK, ML_DV), F32),
                        pltpu.VMEM((2, ML_DK, LANES), F32)],
        compiler_params=_cparams(("arbitrary", "arbitrary")),
        name="mlstm",
    )(main, main, main, main, gates_c, gate_bt, head_g.reshape(nh, 1, ML_DV))


def _pad_cols(a, width):
    return jnp.pad(a, ((0, 0), (0, width - a.shape[1])))


def _ssd_na_layer(x2, mods, norm_g, w_in, conv_w, conv_b, dt_bias, a_log, d_skip, ssd_norm, rpb, bsz, seq):
    sh1, sc1 = mods
    s1 = SSD_WIDTH
    s2 = s1 + SSD_CONV_CH
    s3 = s2 + 2 * SSD_HEADS
    w_main = jnp.concatenate([w_in[:, :s2], w_in[:, s3:]], axis=1).astype(BF16)
    w_dt = _pad_cols(w_in[:, s2:s3], LANES).astype(BF16)
    main, dtf = _project(x2, norm_g[0], sh1, sc1, w_main, w_dt, False, seq, tm=min(512, seq), tn=1408,
                         name="proj_ssd_na")

    conv = _conv_silu(main, s1, conv_w, conv_b, seq, tr=min(512, seq), tc=512)
    dtb = _pad_cols(dt_bias.reshape(1, 2 * SSD_HEADS), LANES)
    alog = _pad_cols(a_log.reshape(1, 2 * SSD_HEADS), LANES)
    dskip = jnp.repeat(d_skip, SSD_HEAD_DIM)[None, :]
    y_f = _ssd_pass(conv, dtf, dtb, alog, bsz, seq, reverse=False)
    y_ssd = _ssd_pass(conv, dtf, dtb, alog, bsz, seq, reverse=True,
                      extra=(main, y_f, dskip, ssd_norm[None, :]))

    q_blk = (s1 + SSD_CONV_CH) // LANES
    y_na = _na(main, _na_bias(rpb), bsz, seq, q_blk)
    return y_ssd, 0, y_na, 0


def _mlstm_layer(x2, mods, norm_g, w_in, gate_b, head_g, bsz, seq):
    sh1, sc1 = mods
    wm = 2 * ML_QK + 2 * ML_WIDTH
    w_main = w_in[:, :wm].astype(BF16)
    main, gates_t = _project(x2, norm_g[0], sh1, sc1, w_main, w_in[:, wm:].T.astype(BF16), True, seq,
                             tm=min(512, seq), tn=1536, name="proj_mlstm")
    y = _mlstm(main, gates_t, gate_b.reshape(4 * ML_HEADS, 1).astype(F32), head_g, bsz, seq)
    return y, 0, y, 1


def kernel(x, c, ada_w, ada_b, norm_g, mlp_w1, mlp_w2, ab_w_in, ab_conv_w, ab_conv_b, ab_dt_bias, ab_a_log,
           ab_d_skip, ab_ssd_norm, ab_rpb, ab_w_out, ml_w_in, ml_gate_b, ml_head_norm, ml_w_out):
    bsz, seq, d = x.shape
    depth = ada_w.shape[0]
    mod = _adaln(c, ada_w, ada_b)
    x2 = x.reshape(bsz * seq, d)
    for layer in range(depth):
        sh1, sc1, g1, sh2, sc2, g2 = [mod[layer, :, i * d:(i + 1) * d] for i in range(6)]
        j = layer // 2
        if layer % 2 == 0:
            mixed = _ssd_na_layer(x2, (sh1, sc1), norm_g[layer], ab_w_in[j], ab_conv_w[j], ab_conv_b[j],
                                  ab_dt_bias[j], ab_a_log[j], ab_d_skip[j], ab_ssd_norm[j], ab_rpb[j], bsz, seq)
            w_out = ab_w_out[j]
        else:
            mixed = _mlstm_layer(x2, (sh1, sc1), norm_g[layer], ml_w_in[j], ml_gate_b[j], ml_head_norm[j],
                                 bsz, seq)
            w_out = ml_w_out[j]
        x2 = _tail(*mixed, w_out.astype(BF16), x2, jnp.stack([g1, sh2, sc2, g2], axis=1), norm_g[layer, 1:4],
                   mlp_w1[layer].astype(BF16), mlp_w2[layer].astype(BF16), seq, tm=min(512, seq), tf=1024,
                   name="tail%d" % layer)
    return x2.reshape(bsz, seq, d)
```

```python
import functools

import numpy as np
import jax
import jax.numpy as jnp
from jax import lax
from jax.experimental import pallas as pl
from jax.experimental.pallas import tpu as pltpu

F32 = jnp.float32
BF16 = jnp.bfloat16

NORM_EPS = 1e-6
GRID_W = 64

SSD_HEAD_DIM = 64
SSD_HEADS = 16
SSD_GROUPS = 2
SSD_STATE = 128
SSD_CONV = 5
SSD_CHUNK = 128
SSD_WIDTH = SSD_HEADS * SSD_HEAD_DIM
SSD_BC = 2 * SSD_GROUPS * SSD_STATE
SSD_CONV_CH = SSD_WIDTH + SSD_BC

NA_HEAD_DIM = 64
NA_HEADS = 16
NA_KH = 8
NA_KW = 16
NA_WIDTH = NA_HEADS * NA_HEAD_DIM

ML_HEADS = 8
ML_DV = 256
ML_DK = 128
ML_CHUNK = 256
ML_WIDTH = ML_HEADS * ML_DV
ML_QK = ML_HEADS * ML_DK

LANES = 128
SUBLANES = 8
HALO = 16
NEG_BIG = -1e30
VMEM_LIMIT = 56 * 1024 * 1024


def _cparams(sem):
    return pltpu.CompilerParams(dimension_semantics=sem, vmem_limit_bytes=VMEM_LIMIT)


def _dot(a, b):
    return jnp.dot(a, b, preferred_element_type=F32)


def _dot_nt(a, b):
    return lax.dot_general(a, b, (((1,), (1,)), ((), ())), preferred_element_type=F32)


def _split3(x):
    hi = x.astype(BF16)
    r1 = x - hi.astype(F32)
    mid = r1.astype(BF16)
    lo = (r1 - mid.astype(F32)).astype(BF16)
    return hi, mid, lo


def _dot_exact_lhs01(sel, x):
    hi, mid, lo = _split3(x)
    return _dot(sel, hi) + _dot(sel, mid) + _dot(sel, lo)


def _iota(shape, dim):
    return lax.broadcasted_iota(jnp.int32, shape, dim)


def _sigmoid(x):
    return 0.5 * (jnp.tanh(0.5 * x) + 1.0)


def _softplus(x):
    return jnp.maximum(x, 0.0) + jnp.log(1.0 + jnp.exp(-jnp.abs(x)))


def _log_sigmoid(x):
    return jnp.minimum(x, 0.0) - jnp.log(1.0 + jnp.exp(-jnp.abs(x)))


def _rms(x):
    return x * lax.rsqrt(jnp.mean(x * x, axis=-1, keepdims=True) + NORM_EPS)


def _mod_kernel(c_ref, w_ref, b_ref, o_ref):
    c = c_ref[...]
    cond = c * _sigmoid(c)
    o_ref[0] = jnp.dot(cond, w_ref[0], preferred_element_type=F32,
                       precision=lax.Precision.HIGHEST) + b_ref[0]


def _adaln(c, ada_w, ada_b):
    depth, d, d6 = ada_w.shape
    bsz = c.shape[0]
    tn = 1024
    return pl.pallas_call(
        _mod_kernel,
        grid=(depth, d6 // tn),
        in_specs=[pl.BlockSpec((bsz, d), lambda l, j: (0, 0)),
                  pl.BlockSpec((1, d, tn), lambda l, j: (l, 0, j)),
                  pl.BlockSpec((1, 1, tn), lambda l, j: (l, 0, j))],
        out_specs=pl.BlockSpec((1, bsz, tn), lambda l, j: (l, 0, j)),
        out_shape=jax.ShapeDtypeStruct((depth, bsz, d6), F32),
        compiler_params=_cparams(("arbitrary", "arbitrary")),
        name="adaln",
    )(c, ada_w, ada_b.reshape(depth, 1, d6))


def _proj_kernel(x_ref, g_ref, sh_ref, sc_ref, w_ref, ws_ref, o_ref, os_ref, *, small_transposed, tn):
    hb = (_rms(x_ref[...]) * g_ref[...] * (1.0 + sc_ref[0]) + sh_ref[0]).astype(BF16)
    small = _dot(hb, ws_ref[...])
    if small_transposed:
        os_ref[...] = small.T[0:os_ref.shape[0], :]
    else:
        os_ref[...] = small
    for j in range(w_ref.shape[1] // tn):
        o_ref[:, j * tn:(j + 1) * tn] = _dot(hb, w_ref[:, j * tn:(j + 1) * tn]).astype(o_ref.dtype)


def _project(x2, g, shift, scale, w, ws, small_rows, seq, tm, tn, name):
    n, d = x2.shape
    wn = w.shape[1]
    tiles_per_batch = seq // tm
    bsz = shift.shape[0]
    if small_rows:
        small_spec, small_shape = pl.BlockSpec((small_rows, tm), lambda i: (0, i)), (small_rows, n)
    else:
        small_spec, small_shape = pl.BlockSpec((tm, LANES), lambda i: (i, 0)), (n, LANES)
    resident = pl.Buffered(1)
    return pl.pallas_call(
        functools.partial(_proj_kernel, small_transposed=bool(small_rows), tn=tn),
        grid=(n // tm,),
        in_specs=[pl.BlockSpec((tm, d), lambda i: (i, 0)),
                  pl.BlockSpec((1, d), lambda i: (0, 0)),
                  pl.BlockSpec((1, 1, d), lambda i: (i // tiles_per_batch, 0, 0)),
                  pl.BlockSpec((1, 1, d), lambda i: (i // tiles_per_batch, 0, 0)),
                  pl.BlockSpec((d, wn), lambda i: (0, 0), pipeline_mode=resident),
                  pl.BlockSpec(ws.shape, lambda i: (0, 0), pipeline_mode=resident)],
        out_specs=[pl.BlockSpec((tm, wn), lambda i: (i, 0)), small_spec],
        out_shape=[jax.ShapeDtypeStruct((n, wn), BF16), jax.ShapeDtypeStruct(small_shape, F32)],
        compiler_params=_cparams(("arbitrary",)),
        name=name,
    )(x2, g.reshape(1, d), shift.reshape(bsz, 1, d), scale.reshape(bsz, 1, d), w, ws)


def _tail_kernel(ya_ref, yb_ref, wa_ref, wb_ref, x_ref, mod_ref, ng_ref, w1_ref, w2_ref, o_ref, *, tf):
    mixed = _dot(ya_ref[...], wa_ref[...]) + _dot(yb_ref[...], wb_ref[...])
    x = x_ref[...] + mod_ref[0, 0:1, :] * (_rms(mixed) * ng_ref[0:1, :])
    h = (_rms(x) * ng_ref[1:2, :] * (1.0 + mod_ref[0, 2:3, :]) + mod_ref[0, 1:2, :]).astype(BF16)
    u = None
    for f in range(w1_ref.shape[1] // tf):
        a = jnp.maximum(_dot(h, w1_ref[:, f * tf:(f + 1) * tf]), 0.0)
        part = _dot((a * a).astype(BF16), w2_ref[f * tf:(f + 1) * tf, :])
        u = part if u is None else u + part
    o_ref[...] = x + mod_ref[0, 3:4, :] * (_rms(u) * ng_ref[2:3, :])


def _tail(ya, ia, yb, ib, w_out, jo, x2, mods, gains, w1, w2, layer, seq, tm, tf, name):
    n, d = x2.shape
    kh = w_out.shape[1] // 2
    dff = w1.shape[2]
    tiles_per_batch = seq // tm
    resident = pl.Buffered(1)
    return pl.pallas_call(
        functools.partial(_tail_kernel, tf=tf),
        grid=(n // tm,),
        in_specs=[pl.BlockSpec((tm, kh), lambda i: (i, ia)),
                  pl.BlockSpec((tm, kh), lambda i: (i, ib)),
                  pl.BlockSpec((None, kh, d), lambda i: (jo, 0, 0), pipeline_mode=resident),
                  pl.BlockSpec((None, kh, d), lambda i: (jo, 1, 0), pipeline_mode=resident),
                  pl.BlockSpec((tm, d), lambda i: (i, 0)),
                  pl.BlockSpec((1, 4, d), lambda i: (i // tiles_per_batch, 0, 0)),
                  pl.BlockSpec((3, d), lambda i: (0, 0)),
                  pl.BlockSpec((None, d, dff), lambda i: (layer, 0, 0), pipeline_mode=resident),
                  pl.BlockSpec((None, dff, d), lambda i: (layer, 0, 0), pipeline_mode=resident)],
        out_specs=pl.BlockSpec((tm, d), lambda i: (i, 0)),
        out_shape=jax.ShapeDtypeStruct((n, d), F32),
        compiler_params=_cparams(("arbitrary",)),
        name=name,
    )(ya, yb, w_out, w_out, x2, mods, gains, w1, w2)


def _conv_kernel(main_ref, prev_ref, next_ref, w_ref, b_ref, o_ref, scr, *, blocks_per_seq):
    t = main_ref.shape[0]
    i = pl.program_id(0) % blocks_per_seq
    keep_prev = (i > 0).astype(F32)
    keep_next = (i < blocks_per_seq - 1).astype(F32)
    scr[0:HALO, :] = prev_ref[...].astype(F32) * keep_prev
    scr[HALO:HALO + t, :] = main_ref[...].astype(F32)
    scr[HALO + t:2 * HALO + t, :] = next_ref[...].astype(F32) * keep_next
    pad = SSD_CONV // 2
    acc = b_ref[...] + w_ref[0:1, :] * scr[HALO - pad:HALO - pad + t, :]
    for k in range(1, SSD_CONV):
        acc = acc + w_ref[k:k + 1, :] * scr[HALO - pad + k:HALO - pad + k + t, :]
    o_ref[...] = (acc * _sigmoid(acc)).astype(o_ref.dtype)


def _conv_silu(main, col0, conv_w, conv_b, seq, tr, tc):
    n = main.shape[0]
    ch = conv_w.shape[1]
    cb0 = col0 // tc
    rpb = tr // HALO
    last_halo = n // HALO - 1
    return pl.pallas_call(
        functools.partial(_conv_kernel, blocks_per_seq=seq // tr),
        grid=(n // tr, ch // tc),
        in_specs=[pl.BlockSpec((tr, tc), lambda i, j: (i, cb0 + j)),
                  pl.BlockSpec((HALO, tc), lambda i, j: (jnp.maximum(i * rpb - 1, 0), cb0 + j)),
                  pl.BlockSpec((HALO, tc), lambda i, j: (jnp.minimum((i + 1) * rpb, last_halo), cb0 + j)),
                  pl.BlockSpec((SSD_CONV, tc), lambda i, j: (0, j)),
                  pl.BlockSpec((1, tc), lambda i, j: (0, j))],
        out_specs=pl.BlockSpec((tr, tc), lambda i, j: (i, j)),
        out_shape=jax.ShapeDtypeStruct((n, ch), BF16),
        scratch_shapes=[pltpu.VMEM((tr + 2 * HALO, tc), F32)],
        compiler_params=_cparams(("arbitrary", "arbitrary")),
        name="conv_silu",
    )(main, main, main, conv_w, conv_b.reshape(1, ch))


def _ssd_kernel(xs_ref, bc_ref, dt_ref, dtb_ref, alog_ref, *rest, reverse):
    if reverse:
        z_ref, yf_ref, dskip_ref, ng_ref, o_ref, st_scr = rest
    else:
        o_ref, st_scr = rest
    t = SSD_CHUNK
    hd = SSD_HEAD_DIM
    hpg = SSD_HEADS // SSD_GROUPS

    @pl.when(pl.program_id(1) == 0)
    def _():
        st_scr[...] = jnp.zeros_like(st_scr)

    xs = xs_ref[...].astype(F32)
    bc = bc_ref[...]

    dt = _softplus(dt_ref[...] + dtb_ref[...])
    adt = dt * (-jnp.exp(alog_ref[...]))
    row = _iota((t, t), 0)
    col = _iota((t, t), 1)
    tri = (col >= row) if reverse else (col <= row)
    cs = _dot_exact_lhs01(tri.astype(BF16), adt)
    cs_t = cs.T
    dt_t = dt.T
    base = SSD_HEADS if reverse else 0
    last = 0 if reverse else t - 1

    first_half = _iota((t, LANES), 1) < hd
    ys = []
    for g in range(SSD_GROUPS):
        bg = bc[:, g * SSD_STATE:(g + 1) * SSD_STATE]
        cg = bc[:, (SSD_GROUPS + g) * SSD_STATE:(SSD_GROUPS + g + 1) * SSD_STATE]
        cb = _dot_nt(cg, bg)
        bg_t = bg.astype(F32).T
        s_prev = st_scr[g]
        y_off = _dot(cg, s_prev.astype(BF16))
        s_new = []
        etot = []
        for pr in range(hpg // 2):
            pair = g * (hpg // 2) + pr
            xpair = xs[:, pair * LANES:(pair + 1) * LANES]
            cs_cols = []
            acc = None
            for sub in range(2):
                j = base + 2 * pair + sub
                cs_col = jnp.broadcast_to(cs[:, j:j + 1], (t, t))
                cs_cols.append(cs_col)
                dts = dt_t[j:j + 1, :]
                lmat = jnp.exp(jnp.where(tri, cs_col - cs_t[j:j + 1, :], NEG_BIG))
                m = (cb * lmat * dts).astype(BF16)
                keep = first_half if sub == 0 else jnp.logical_not(first_half)
                part = _dot(m, jnp.where(keep, xpair, 0.0).astype(BF16))
                acc = part if acc is None else acc + part
            cs_pair = jnp.where(first_half, cs_cols[0], cs_cols[1])
            tot_pair = cs_pair[last:last + 1, :]
            xd = (xpair * jnp.exp(tot_pair - cs_pair)).astype(BF16)
            sp = None
            for sub in range(2):
                j = base + 2 * pair + sub
                keep = first_half if sub == 0 else jnp.logical_not(first_half)
                part = _dot((bg_t * dt_t[j:j + 1, :]).astype(BF16), jnp.where(keep, xd, jnp.zeros((), BF16)))
                sp = part if sp is None else sp + part
            ys.append(acc + y_off[:, pr * LANES:(pr + 1) * LANES] * jnp.exp(cs_pair))
            s_new.append(sp)
            etot.append(jnp.exp(tot_pair))
        st_scr[g] = s_prev * jnp.concatenate(etot, axis=1) + jnp.concatenate(s_new, axis=1)
    y = jnp.concatenate(ys, axis=1)

    if reverse:
        y = y + yf_ref[...] + dskip_ref[...] * xs
        z = z_ref[...].astype(F32)
        y = y * (z * _sigmoid(z))
        o_ref[...] = (_rms(y) * ng_ref[...]).astype(o_ref.dtype)
    else:
        o_ref[...] = y


def _ssd_pass(conv, dtf, dtb, alog, bsz, seq, reverse, extra=None):
    n = conv.shape[0]
    t = SSD_CHUNK
    nc = seq // t
    bc_blk = SSD_WIDTH // SSD_BC

    def rows(b, c):
        return b * nc + (nc - 1 - c if reverse else c)

    in_specs = [pl.BlockSpec((t, SSD_WIDTH), lambda b, c: (rows(b, c), 0)),
                pl.BlockSpec((t, SSD_BC), lambda b, c: (rows(b, c), bc_blk)),
                pl.BlockSpec((t, LANES), lambda b, c: (rows(b, c), 0)),
                pl.BlockSpec((1, LANES), lambda b, c: (0, 0)),
                pl.BlockSpec((1, LANES), lambda b, c: (0, 0))]
    args = [conv, conv, dtf, dtb, alog]
    if reverse:
        main, yf, dskip, ng = extra
        in_specs += [pl.BlockSpec((t, SSD_WIDTH), lambda b, c: (rows(b, c), 0)),
                     pl.BlockSpec((t, SSD_WIDTH), lambda b, c: (rows(b, c), 0)),
                     pl.BlockSpec((1, SSD_WIDTH), lambda b, c: (0, 0)),
                     pl.BlockSpec((1, SSD_WIDTH), lambda b, c: (0, 0))]
        args += [main, yf, dskip, ng]
        out_dtype = BF16
    else:
        out_dtype = F32
    return pl.pallas_call(
        functools.partial(_ssd_kernel, reverse=reverse),
        grid=(bsz, nc),
        in_specs=in_specs,
        out_specs=pl.BlockSpec((t, SSD_WIDTH), lambda b, c: (rows(b, c), 0)),
        out_shape=jax.ShapeDtypeStruct((n, SSD_WIDTH), out_dtype),
        scratch_shapes=[pltpu.VMEM((SSD_GROUPS, SSD_STATE, SSD_WIDTH // SSD_GROUPS), F32)],
        compiler_params=_cparams(("arbitrary", "arbitrary")),
        name="ssd_bwd" if reverse else "ssd_fwd",
    )(*args)


def _na_kernel(q_ref, k_ref, v_ref, bias_ref, o_ref, s_scr, p_scr, l_scr, *, nrows):
    w = GRID_W
    band = NA_KH * w
    first = _iota((w, LANES), 1) < NA_HEAD_DIM
    zero = jnp.zeros((), BF16)

    def band_start(r):
        return jnp.clip(r - NA_KH // 2, 0, nrows - NA_KH)

    def scores(r, slot):
        rs = band_start(r)
        q = q_ref[pl.ds(pl.multiple_of(r * w, w), w), :] * jnp.asarray(NA_HEAD_DIM ** -0.5, BF16)
        qs = jnp.concatenate([jnp.where(first, q, zero), jnp.where(first, zero, q)], axis=0)
        kb = k_ref[pl.ds(pl.multiple_of(rs * w, w), band), :]
        j0 = NA_KH - 1 - (r - rs)
        bias = jnp.concatenate([bias_ref[0, j0 + 2 * a] for a in range(NA_KH // 2)], axis=1)
        s_scr[slot] = _dot_nt(qs, kb) + bias

    def softmax(slot):
        s = s_scr[slot]
        p = jnp.exp(s - jnp.max(s, axis=-1, keepdims=True))
        l_scr[slot] = jnp.broadcast_to(jnp.sum(p, axis=-1, keepdims=True), (2 * w, LANES))
        p_scr[slot] = p.astype(BF16)

    def values(r, slot):
        rs = band_start(r)
        vb = v_ref[pl.ds(pl.multiple_of(rs * w, w), band), :]
        o = _dot(p_scr[slot], vb) / l_scr[slot]
        o_ref[pl.ds(pl.multiple_of(r * w, w), w), :] = jnp.where(first, o[0:w], o[w:2 * w]).astype(o_ref.dtype)

    s_scr[...] = jnp.zeros_like(s_scr)
    p_scr[...] = jnp.zeros_like(p_scr)
    l_scr[...] = jnp.ones_like(l_scr)

    def body(i2, carry):
        for slot in range(2):
            i = 2 * i2 + slot
            values(jnp.maximum(i - 2, 0), slot)
            softmax(1 - slot)
            scores(jnp.minimum(i, nrows - 1), slot)
        return carry

    lax.fori_loop(0, nrows // 2 + 1, body, 0)


def _na_bias(rpb):
    w = GRID_W
    h, nro, nrel = rpb.shape
    c = np.arange(w)
    kc = np.arange(w)
    col_start = np.clip(c - NA_KW // 2, 0, w - NA_KW)
    valid = (kc[None, :] >= col_start[:, None]) & (kc[None, :] < col_start[:, None] + NA_KW)
    co = kc[None, :] - c[:, None] + NA_KW - 1
    onehot = ((np.arange(nrel)[:, None, None] == co[None]) & valid[None]).astype(np.float32)
    sel = np.zeros((2, nrel, w, 2, w), np.float32)
    sel[0, :, :, 0, :] = onehot
    sel[1, :, :, 1, :] = onehot
    mask = np.broadcast_to(np.where(valid, 0.0, NEG_BIG)[:, None, :], (w, 2, w)).astype(np.float32)
    table = np.concatenate([sel.reshape(2 * nrel, 2 * w * w), mask.reshape(1, 2 * w * w)], axis=0)
    r = rpb.astype(F32).reshape(h // 2, 2, nro, nrel)
    rows = jnp.stack([r[:, :, :nro - 1], r[:, :, 1:]], axis=3).transpose(0, 2, 1, 3, 4)
    rows = rows.reshape(h * (nro - 1), 2 * nrel)
    rows = jnp.concatenate([rows, jnp.ones((rows.shape[0], 1), F32)], axis=1)
    tiles = jnp.dot(rows, table, precision=lax.Precision.HIGHEST)
    return tiles.reshape(h // 2, nro - 1, 2 * w, 2 * w)


def _na(main, bias, bsz, seq, q_blk):
    n = main.shape[0]
    nrows = seq // GRID_W
    assert nrows >= NA_KH
    pairs = NA_HEADS // 2
    return pl.pallas_call(
        functools.partial(_na_kernel, nrows=nrows),
        grid=(bsz, pairs),
        in_specs=[pl.BlockSpec((seq, LANES), lambda b, p: (b, q_blk + p)),
                  pl.BlockSpec((seq, LANES), lambda b, p: (b, q_blk + pairs + p)),
                  pl.BlockSpec((seq, LANES), lambda b, p: (b, q_blk + 2 * pairs + p)),
                  pl.BlockSpec((1,) + bias.shape[1:], lambda b, p: (p, 0, 0, 0))],
        out_specs=pl.BlockSpec((seq, LANES), lambda b, p: (b, p)),
        out_shape=jax.ShapeDtypeStruct((n, NA_WIDTH), BF16),
        scratch_shapes=[pltpu.VMEM((2, 2 * GRID_W, NA_KH * GRID_W), F32),
                        pltpu.VMEM((2, 2 * GRID_W, NA_KH * GRID_W), BF16),
                        pltpu.VMEM((2, 2 * GRID_W, LANES), F32)],
        compiler_params=_cparams(("arbitrary", "arbitrary")),
        name="nbr_attn",
    )(main, main, main, bias)


ML_TAB = 16


def _mlstm_kernel(q_ref, k_ref, v_ref, og_ref, gt_ref, gbt_ref, hg_ref, o_ref,
                  rall_scr, ball_scr, a_scr, w_scr, sc_scr, cp_scr, c_st, n_st, *, nchunks):
    t = ML_CHUNK
    nc = nchunks
    head = pl.program_id(1)
    scale = ML_DK ** -0.5
    row = _iota((t, t), 0)
    col = _iota((t, t), 1)
    tris = (col <= row, col >= row)

    @pl.when((pl.program_id(0) == 0) & (head == 0))
    def _():
        rall_scr[...] = jnp.zeros_like(rall_scr)
        ball_scr[...] = jnp.zeros_like(ball_scr)
        for d in range(2):
            rall_scr[ML_TAB * d + 9:ML_TAB * d + 12, :] = jnp.ones((3, nc * t), F32)
            ball_scr[d, ML_TAB * d:ML_TAB * d + 3, :] = jnp.full((3, nc * t), -1.0, F32)

    lane = _iota((nc, t), 1)
    subc = _iota((nc, LANES), 0)
    sub32 = _iota((4 * ML_HEADS, 1), 0)

    def head_rows(kind):
        idx = kind * ML_HEADS + head
        bias = jnp.sum(jnp.where(sub32 == idx, gbt_ref[...], 0.0), axis=0, keepdims=True)
        return gt_ref[0, idx] + bias

    def scan_lanes(x, reverse, op, fill):
        sh = 1
        while sh < t:
            if reverse:
                x = op(x, jnp.where(lane < t - sh, pltpu.roll(x, t - sh, axis=1), fill))
            else:
                x = op(x, jnp.where(lane >= sh, pltpu.roll(x, sh, axis=1), fill))
            sh *= 2
        return x

    for d in range(2):
        ig = head_rows(2 * d)
        b = scan_lanes(_log_sigmoid(head_rows(2 * d + 1)), bool(d), jnp.add, 0.0)
        g_col = b[:, 0:1] if d else b[:, t - 1:t]
        a = g_col - b + ig
        m_col = jnp.max(a, axis=-1, keepdims=True)
        w = jnp.exp(a - m_col)
        rs = ig - b
        run_max = scan_lanes(rs, bool(d), jnp.maximum, NEG_BIG)
        g128 = jnp.broadcast_to(g_col, (nc, LANES))
        l128 = jnp.broadcast_to(m_col, (nc, LANES))
        m = jnp.zeros((1, LANES), F32)
        m_prev = jnp.zeros((nc, LANES), F32)
        s_old = jnp.zeros((nc, LANES), F32)
        s_new = jnp.zeros((nc, LANES), F32)
        for step in range(nc):
            cc = nc - 1 - step if d else step
            g_c = g128[cc:cc + 1, :]
            l_c = l128[cc:cc + 1, :]
            m_next = jnp.maximum(g_c + m, l_c)
            pick = subc == cc
            m_prev = jnp.where(pick, m, m_prev)
            s_old = jnp.where(pick, jnp.exp(g_c + m - m_next), s_old)
            s_new = jnp.where(pick, jnp.exp(l_c - m_next), s_new)
            m = m_next
        mp = m_prev[:, 0:1]
        u = jnp.maximum(mp, run_max)
        inter = jnp.exp(mp - u) * scale
        floor = jnp.exp(-(b + u))
        base = ML_TAB * d
        a_rows = [p.astype(F32) for x in (u, inter, floor) for p in _split3(x)]
        b_rows = [p.astype(F32) for p in _split3(rs)]
        for cc in range(nc):
            ls = slice(cc * t, (cc + 1) * t)
            for j, x in enumerate(a_rows):
                rall_scr[base + j:base + j + 1, ls] = x[cc:cc + 1, :]
            for j, x in enumerate(b_rows):
                ball_scr[d, base + 9 + j:base + 10 + j, ls] = x[cc:cc + 1, :]
            w_scr[d, :, ls] = jnp.broadcast_to(w[cc:cc + 1, :], (SUBLANES, t))
            sc_scr[d, cc, 0] = jnp.broadcast_to(s_old[cc:cc + 1, :], (SUBLANES, LANES))
            sc_scr[d, cc, 1] = jnp.broadcast_to(s_new[cc:cc + 1, :], (SUBLANES, LANES))

    def build_cols(c, carry):
        r0 = pl.multiple_of(c * t, t)
        rows = jnp.concatenate([rall_scr[:, pl.ds(r0, t)], jnp.zeros((LANES - 2 * ML_TAB, t), F32)], axis=0)
        a_scr[pl.ds(r0, t), :] = rows.T.astype(BF16)
        return carry

    lax.fori_loop(0, nc, build_cols, 0)

    c_st[...] = jnp.zeros_like(c_st)
    n_st[...] = jnp.zeros_like(n_st)

    def scan_step(i, carry):
        for d in range(2):
            c = nc - 1 - i if d else i
            r0 = pl.multiple_of(c * t, t)
            kw_t = k_ref[pl.ds(r0, t), :].astype(F32).T * w_scr[d, 0:1, pl.ds(r0, t)]
            s_loc = _dot(kw_t.astype(BF16), v_ref[pl.ds(r0, t), :])
            n_loc = jnp.sum(kw_t, axis=-1, keepdims=True)
            c_prev = c_st[d]
            n_prev = n_st[d]
            cp_scr[d, c] = jnp.concatenate([c_prev, n_prev], axis=1).astype(BF16)
            s_old = sc_scr[d, c, 0][0:1, :]
            s_new = sc_scr[d, c, 1][0:1, :]
            c_st[d] = (jnp.concatenate([s_old, s_old], axis=1) * c_prev
                       + jnp.concatenate([s_new, s_new], axis=1) * s_loc)
            n_st[d] = s_old * n_prev + s_new * n_loc
        return carry

    lax.fori_loop(0, nc, scan_step, 0, unroll=2)

    kk = _iota((LANES, 2 * LANES), 0)
    ll = _iota((LANES, 2 * LANES), 1)
    pad_rows = jnp.zeros((LANES - 2 * ML_TAB, t), BF16)

    def out_step(c, carry):
        r0 = pl.multiple_of(c * t, t)
        q = q_ref[pl.ds(r0, t), :]
        v = v_ref[pl.ds(r0, t), :]
        qk = _dot_nt(q, k_ref[pl.ds(r0, t), :]) * scale
        cols = a_scr[pl.ds(r0, t), :]
        hsum = None
        for d in range(2):
            base = ML_TAB * d
            b_arg = jnp.concatenate([ball_scr[d, :, pl.ds(r0, t)].astype(BF16), pad_rows], axis=0)
            arg = _dot(cols, b_arg)
            sc = qk * jnp.exp(jnp.where(tris[d], arg, NEG_BIG))
            pick = (((ll < LANES) & (kk >= base + 3) & (kk < base + 6))
                    | ((ll >= LANES) & (kk >= base + 6) & (kk < base + 9))).astype(BF16)
            rep = _dot(cols, pick)
            inter = rep[:, 0:LANES]
            ext = _dot(q, cp_scr[d, c])
            den = jnp.sum(sc, axis=-1, keepdims=True) + inter * ext[:, ML_DV:ML_DV + LANES]
            r = 1.0 / jnp.maximum(jnp.abs(den), rep[:, LANES:2 * LANES])
            num = _dot(sc.astype(BF16), v) + jnp.concatenate([inter, inter], axis=1) * ext[:, 0:ML_DV]
            hd = num * jnp.concatenate([r, r], axis=1)
            hsum = hd if hsum is None else hsum + hd
        og = og_ref[pl.ds(r0, t), :].astype(F32)
        o_ref[pl.ds(r0, t), :] = (_sigmoid(og) * (_rms(hsum) * hg_ref[0])).astype(o_ref.dtype)
        return carry

    lax.fori_loop(0, nc, out_step, 0)


def _mlstm(main, gates_t, gate_bt, head_g, bsz, seq):
    n = main.shape[0]
    t = ML_CHUNK
    nc = seq // t
    nh = ML_HEADS
    r = gates_t.shape[0]
    gates_c = gates_t.reshape(r, bsz, nc, t).transpose(1, 0, 2, 3)
    return pl.pallas_call(
        functools.partial(_mlstm_kernel, nchunks=nc),
        grid=(bsz, nh),
        in_specs=[pl.BlockSpec((seq, ML_DK), lambda b, h: (b, h)),
                  pl.BlockSpec((seq, ML_DK), lambda b, h: (b, nh + h)),
                  pl.BlockSpec((seq, ML_DV), lambda b, h: (b, nh + h)),
                  pl.BlockSpec((seq, ML_DV), lambda b, h: (b, 2 * nh + h)),
                  pl.BlockSpec((1, r, nc, t), lambda b, h: (b, 0, 0, 0)),
                  pl.BlockSpec((r, 1), lambda b, h: (0, 0)),
                  pl.BlockSpec((1, 1, ML_DV), lambda b, h: (h, 0, 0))],
        out_specs=pl.BlockSpec((seq, ML_DV), lambda b, h: (b, h)),
        out_shape=jax.ShapeDtypeStruct((n, ML_WIDTH), BF16),
        scratch_shapes=[pltpu.VMEM((2 * ML_TAB, seq), F32),
                        pltpu.VMEM((2, 2 * ML_TAB, seq), F32),
                        pltpu.VMEM((seq, LANES), BF16),
                        pltpu.VMEM((2, SUBLANES, seq), F32),
                        pltpu.VMEM((2, nc, 2, SUBLANES, LANES), F32),
                        pltpu.VMEM((2, nc, ML_DK, ML_DV + LANES), BF16),
                        pltpu.VMEM((2, ML_DK, ML_DV), F32),
                        pltpu.VMEM((2, ML_DK, LANES), F32)],
        compiler_params=_cparams(("arbitrary", "arbitrary")),
        name="mlstm",
    )(main, main, main, main, gates_c, gate_bt, head_g.reshape(nh, 1, ML_DV))


def _pad_cols(a, width):
    return jnp.pad(a, ((0, 0), (0, width - a.shape[1])))


def _ssd_na_layer(x2, mods, norm_g, w_in, conv_w, conv_b, dt_bias, a_log, d_skip, ssd_norm, rpb, bsz, seq):
    sh1, sc1 = mods
    s1 = SSD_WIDTH
    s2 = s1 + SSD_CONV_CH
    s3 = s2 + 2 * SSD_HEADS
    w_main = jnp.concatenate([w_in[:, :s2], w_in[:, s3:]], axis=1).astype(BF16)
    w_dt = _pad_cols(w_in[:, s2:s3], LANES).astype(BF16)
    main, dtf = _project(x2, norm_g[0], sh1, sc1, w_main, w_dt, 0, seq, tm=min(512, seq), tn=1408,
                         name="proj_ssd_na")

    conv = _conv_silu(main, s1, conv_w, conv_b, seq, tr=min(512, seq), tc=512)
    dtb = _pad_cols(dt_bias.reshape(1, 2 * SSD_HEADS), LANES)
    alog = _pad_cols(a_log.reshape(1, 2 * SSD_HEADS), LANES)
    dskip = jnp.repeat(d_skip, SSD_HEAD_DIM)[None, :]
    y_f = _ssd_pass(conv, dtf, dtb, alog, bsz, seq, reverse=False)
    y_ssd = _ssd_pass(conv, dtf, dtb, alog, bsz, seq, reverse=True,
                      extra=(main, y_f, dskip, ssd_norm[None, :]))

    q_blk = (s1 + SSD_CONV_CH) // LANES
    y_na = _na(main, _na_bias(rpb), bsz, seq, q_blk)
    return y_ssd, 0, y_na, 0


def _mlstm_layer(x2, mods, norm_g, w_in, gate_b, head_g, bsz, seq):
    sh1, sc1 = mods
    wm = 2 * ML_QK + 2 * ML_WIDTH
    w_main = w_in[:, :wm].astype(BF16)
    main, gates_t = _project(x2, norm_g[0], sh1, sc1, w_main, _pad_cols(w_in[:, wm:], LANES).astype(BF16),
                             4 * ML_HEADS, seq, tm=min(512, seq), tn=1536, name="proj_mlstm")
    y = _mlstm(main, gates_t, gate_b.reshape(4 * ML_HEADS, 1).astype(F32), head_g, bsz, seq)
    return y, 0, y, 1


def kernel(x, c, ada_w, ada_b, norm_g, mlp_w1, mlp_w2, ab_w_in, ab_conv_w, ab_conv_b, ab_dt_bias, ab_a_log,
           ab_d_skip, ab_ssd_norm, ab_rpb, ab_w_out, ml_w_in, ml_gate_b, ml_head_norm, ml_w_out):
    bsz, seq, d = x.shape
    depth = ada_w.shape[0]
    mod = _adaln(c, ada_w, ada_b)
    x2 = x.reshape(bsz * seq, d)
    w1_all, w2_all = mlp_w1.astype(BF16), mlp_w2.astype(BF16)
    w_outs = (ab_w_out.astype(BF16), ml_w_out.astype(BF16))
    for layer in range(depth):
        sh1, sc1, g1, sh2, sc2, g2 = [mod[layer, :, i * d:(i + 1) * d] for i in range(6)]
        j = layer // 2
        if layer % 2 == 0:
            mixed = _ssd_na_layer(x2, (sh1, sc1), norm_g[layer], ab_w_in[j], ab_conv_w[j], ab_conv_b[j],
                                  ab_dt_bias[j], ab_a_log[j], ab_d_skip[j], ab_ssd_norm[j], ab_rpb[j], bsz, seq)
        else:
            mixed = _mlstm_layer(x2, (sh1, sc1), norm_g[layer], ml_w_in[j], ml_gate_b[j], ml_head_norm[j],
                                 bsz, seq)
        x2 = _tail(*mixed, w_outs[layer % 2], j, x2, jnp.stack([g1, sh2, sc2, g2], axis=1), norm_g[layer, 1:4],
                   w1_all, w2_all, layer, seq, tm=min(512, seq), tf=1024, name="tail%d" % layer)
    return x2.reshape(bsz, seq, d)
```

```python
import functools

import numpy as np
import jax
import jax.numpy as jnp
from jax import lax
from jax.experimental import pallas as pl
from jax.experimental.pallas import tpu as pltpu

F32 = jnp.float32
BF16 = jnp.bfloat16

NORM_EPS = 1e-6
GRID_W = 64

SSD_HEAD_DIM = 64
SSD_HEADS = 16
SSD_GROUPS = 2
SSD_STATE = 128
SSD_CONV = 5
SSD_CHUNK = 128
SSD_WIDTH = SSD_HEADS * SSD_HEAD_DIM
SSD_BC = 2 * SSD_GROUPS * SSD_STATE
SSD_CONV_CH = SSD_WIDTH + SSD_BC

NA_HEAD_DIM = 64
NA_HEADS = 16
NA_KH = 8
NA_KW = 16
NA_WIDTH = NA_HEADS * NA_HEAD_DIM

ML_HEADS = 8
ML_DV = 256
ML_DK = 128
ML_CHUNK = 256
ML_WIDTH = ML_HEADS * ML_DV
ML_QK = ML_HEADS * ML_DK

LANES = 128
SUBLANES = 8
HALO = 16
NEG_BIG = -1e30
VMEM_LIMIT = 56 * 1024 * 1024


def _cparams(sem):
    return pltpu.CompilerParams(dimension_semantics=sem, vmem_limit_bytes=VMEM_LIMIT)


def _dot(a, b):
    return jnp.dot(a, b, preferred_element_type=F32)


def _dot_nt(a, b):
    return lax.dot_general(a, b, (((1,), (1,)), ((), ())), preferred_element_type=F32)


def _split3(x):
    hi = x.astype(BF16)
    r1 = x - hi.astype(F32)
    mid = r1.astype(BF16)
    lo = (r1 - mid.astype(F32)).astype(BF16)
    return hi, mid, lo


def _dot_exact_lhs01(sel, x):
    hi, mid, lo = _split3(x)
    return _dot(sel, hi) + _dot(sel, mid) + _dot(sel, lo)


def _iota(shape, dim):
    return lax.broadcasted_iota(jnp.int32, shape, dim)


def _sigmoid(x):
    return 0.5 * (jnp.tanh(0.5 * x) + 1.0)


def _softplus(x):
    return jnp.maximum(x, 0.0) + jnp.log(1.0 + jnp.exp(-jnp.abs(x)))


def _log_sigmoid(x):
    return jnp.minimum(x, 0.0) - jnp.log(1.0 + jnp.exp(-jnp.abs(x)))


def _rms(x):
    return x * lax.rsqrt(jnp.mean(x * x, axis=-1, keepdims=True) + NORM_EPS)


def _mod_kernel(c_ref, w_ref, b_ref, o_ref):
    c = c_ref[...]
    cond = c * _sigmoid(c)
    o_ref[0] = _dot(cond.astype(BF16), w_ref[0].astype(BF16)) + b_ref[0]


def _adaln(c, ada_w, ada_b):
    depth, d, d6 = ada_w.shape
    bsz = c.shape[0]
    tn = 1024
    return pl.pallas_call(
        _mod_kernel,
        grid=(depth, d6 // tn),
        in_specs=[pl.BlockSpec((bsz, d), lambda l, j: (0, 0)),
                  pl.BlockSpec((1, d, tn), lambda l, j: (l, 0, j)),
                  pl.BlockSpec((1, 1, tn), lambda l, j: (l, 0, j))],
        out_specs=pl.BlockSpec((1, bsz, tn), lambda l, j: (l, 0, j)),
        out_shape=jax.ShapeDtypeStruct((depth, bsz, d6), F32),
        compiler_params=_cparams(("arbitrary", "arbitrary")),
        name="adaln",
    )(c, ada_w, ada_b.reshape(depth, 1, d6))


def _proj_kernel(x_ref, g_ref, sh_ref, sc_ref, w_ref, ws_ref, o_ref, os_ref, *, small_transposed, tn):
    hb = (_rms(x_ref[...]) * g_ref[...] * (1.0 + sc_ref[0]) + sh_ref[0]).astype(BF16)
    small = _dot(hb, ws_ref[...])
    if small_transposed:
        os_ref[...] = small.T[0:os_ref.shape[0], :]
    else:
        os_ref[...] = small
    for j in range(w_ref.shape[1] // tn):
        o_ref[:, j * tn:(j + 1) * tn] = _dot(hb, w_ref[:, j * tn:(j + 1) * tn]).astype(o_ref.dtype)


def _project(x2, g, shift, scale, w, ws, small_rows, seq, tm, tn, name):
    n, d = x2.shape
    wn = w.shape[1]
    tiles_per_batch = seq // tm
    bsz = shift.shape[0]
    if small_rows:
        small_spec, small_shape = pl.BlockSpec((small_rows, tm), lambda i: (0, i)), (small_rows, n)
    else:
        small_spec, small_shape = pl.BlockSpec((tm, LANES), lambda i: (i, 0)), (n, LANES)
    resident = pl.Buffered(1)
    return pl.pallas_call(
        functools.partial(_proj_kernel, small_transposed=bool(small_rows), tn=tn),
        grid=(n // tm,),
        in_specs=[pl.BlockSpec((tm, d), lambda i: (i, 0)),
                  pl.BlockSpec((1, d), lambda i: (0, 0)),
                  pl.BlockSpec((1, 1, d), lambda i: (i // tiles_per_batch, 0, 0)),
                  pl.BlockSpec((1, 1, d), lambda i: (i // tiles_per_batch, 0, 0)),
                  pl.BlockSpec((d, wn), lambda i: (0, 0), pipeline_mode=resident),
                  pl.BlockSpec(ws.shape, lambda i: (0, 0), pipeline_mode=resident)],
        out_specs=[pl.BlockSpec((tm, wn), lambda i: (i, 0)), small_spec],
        out_shape=[jax.ShapeDtypeStruct((n, wn), BF16), jax.ShapeDtypeStruct(small_shape, F32)],
        compiler_params=_cparams(("arbitrary",)),
        name=name,
    )(x2, g.reshape(1, d), shift.reshape(bsz, 1, d), scale.reshape(bsz, 1, d), w, ws)


def _tail_kernel(ya_ref, yb_ref, wa_ref, wb_ref, x_ref, mod_ref, ng_ref, w1_ref, w2_ref, o_ref, *, tf):
    mixed = _dot(ya_ref[...], wa_ref[...]) + _dot(yb_ref[...], wb_ref[...])
    x = x_ref[...] + mod_ref[0, 0:1, :] * (_rms(mixed) * ng_ref[0:1, :])
    h = (_rms(x) * ng_ref[1:2, :] * (1.0 + mod_ref[0, 2:3, :]) + mod_ref[0, 1:2, :]).astype(BF16)
    u = None
    for f in range(w1_ref.shape[1] // tf):
        a = jnp.maximum(_dot(h, w1_ref[:, f * tf:(f + 1) * tf]), 0.0)
        part = _dot((a * a).astype(BF16), w2_ref[f * tf:(f + 1) * tf, :])
        u = part if u is None else u + part
    o_ref[...] = x + mod_ref[0, 3:4, :] * (_rms(u) * ng_ref[2:3, :])


def _tail(ya, ia, yb, ib, w_out, jo, x2, mods, gains, w1, w2, layer, seq, tm, tf, name):
    n, d = x2.shape
    kh = w_out.shape[1] // 2
    dff = w1.shape[2]
    tiles_per_batch = seq // tm
    resident = pl.Buffered(1)
    return pl.pallas_call(
        functools.partial(_tail_kernel, tf=tf),
        grid=(n // tm,),
        in_specs=[pl.BlockSpec((tm, kh), lambda i: (i, ia)),
                  pl.BlockSpec((tm, kh), lambda i: (i, ib)),
                  pl.BlockSpec((None, kh, d), lambda i: (jo, 0, 0), pipeline_mode=resident),
                  pl.BlockSpec((None, kh, d), lambda i: (jo, 1, 0), pipeline_mode=resident),
                  pl.BlockSpec((tm, d), lambda i: (i, 0)),
                  pl.BlockSpec((1, 4, d), lambda i: (i // tiles_per_batch, 0, 0)),
                  pl.BlockSpec((3, d), lambda i: (0, 0)),
                  pl.BlockSpec((None, d, dff), lambda i: (layer, 0, 0), pipeline_mode=resident),
                  pl.BlockSpec((None, dff, d), lambda i: (layer, 0, 0), pipeline_mode=resident)],
        out_specs=pl.BlockSpec((tm, d), lambda i: (i, 0)),
        out_shape=jax.ShapeDtypeStruct((n, d), F32),
        compiler_params=_cparams(("arbitrary",)),
        name=name,
    )(ya, yb, w_out, w_out, x2, mods, gains, w1, w2)


def _conv_kernel(main_ref, prev_ref, next_ref, w_ref, b_ref, o_ref, scr, *, blocks_per_seq):
    t = main_ref.shape[0]
    i = pl.program_id(0) % blocks_per_seq
    keep_prev = (i > 0).astype(F32)
    keep_next = (i < blocks_per_seq - 1).astype(F32)
    scr[0:HALO, :] = prev_ref[...].astype(F32) * keep_prev
    scr[HALO:HALO + t, :] = main_ref[...].astype(F32)
    scr[HALO + t:2 * HALO + t, :] = next_ref[...].astype(F32) * keep_next
    pad = SSD_CONV // 2
    acc = b_ref[...] + w_ref[0:1, :] * scr[HALO - pad:HALO - pad + t, :]
    for k in range(1, SSD_CONV):
        acc = acc + w_ref[k:k + 1, :] * scr[HALO - pad + k:HALO - pad + k + t, :]
    o_ref[...] = (acc * _sigmoid(acc)).astype(o_ref.dtype)


def _conv_silu(main, col0, conv_w, conv_b, seq, tr, tc):
    n = main.shape[0]
    ch = conv_w.shape[1]
    cb0 = col0 // tc
    rpb = tr // HALO
    last_halo = n // HALO - 1
    return pl.pallas_call(
        functools.partial(_conv_kernel, blocks_per_seq=seq // tr),
        grid=(n // tr, ch // tc),
        in_specs=[pl.BlockSpec((tr, tc), lambda i, j: (i, cb0 + j)),
                  pl.BlockSpec((HALO, tc), lambda i, j: (jnp.maximum(i * rpb - 1, 0), cb0 + j)),
                  pl.BlockSpec((HALO, tc), lambda i, j: (jnp.minimum((i + 1) * rpb, last_halo), cb0 + j)),
                  pl.BlockSpec((SSD_CONV, tc), lambda i, j: (0, j)),
                  pl.BlockSpec((1, tc), lambda i, j: (0, j))],
        out_specs=pl.BlockSpec((tr, tc), lambda i, j: (i, j)),
        out_shape=jax.ShapeDtypeStruct((n, ch), BF16),
        scratch_shapes=[pltpu.VMEM((tr + 2 * HALO, tc), F32)],
        compiler_params=_cparams(("arbitrary", "arbitrary")),
        name="conv_silu",
    )(main, main, main, conv_w, conv_b.reshape(1, ch))


def _ssd_kernel(xs_ref, bc_ref, dt_ref, dtb_ref, alog_ref, *rest, reverse):
    if reverse:
        z_ref, yf_ref, dskip_ref, ng_ref, o_ref, st_scr = rest
    else:
        o_ref, st_scr = rest
    t = SSD_CHUNK
    hd = SSD_HEAD_DIM
    hpg = SSD_HEADS // SSD_GROUPS

    @pl.when(pl.program_id(1) == 0)
    def _():
        st_scr[...] = jnp.zeros_like(st_scr)

    xs = xs_ref[...].astype(F32)
    bc = bc_ref[...]

    dt = _softplus(dt_ref[...] + dtb_ref[...])
    adt = dt * (-jnp.exp(alog_ref[...]))
    row = _iota((t, t), 0)
    col = _iota((t, t), 1)
    tri = (col >= row) if reverse else (col <= row)
    cs = _dot_exact_lhs01(tri.astype(BF16), adt)
    cs_t = cs.T
    dt_t = dt.T
    base = SSD_HEADS if reverse else 0
    last = 0 if reverse else t - 1

    first_half = _iota((t, LANES), 1) < hd
    ys = []
    for g in range(SSD_GROUPS):
        bg = bc[:, g * SSD_STATE:(g + 1) * SSD_STATE]
        cg = bc[:, (SSD_GROUPS + g) * SSD_STATE:(SSD_GROUPS + g + 1) * SSD_STATE]
        cb = _dot_nt(cg, bg)
        bg_t = bg.astype(F32).T
        s_prev = st_scr[g]
        y_off = _dot(cg, s_prev.astype(BF16))
        s_new = []
        etot = []
        for pr in range(hpg // 2):
            pair = g * (hpg // 2) + pr
            xpair = xs[:, pair * LANES:(pair + 1) * LANES]
            cs_cols = []
            acc = None
            for sub in range(2):
                j = base + 2 * pair + sub
                cs_col = jnp.broadcast_to(cs[:, j:j + 1], (t, t))
                cs_cols.append(cs_col)
                dts = dt_t[j:j + 1, :]
                lmat = jnp.exp(jnp.where(tri, cs_col - cs_t[j:j + 1, :], NEG_BIG))
                m = (cb * lmat * dts).astype(BF16)
                keep = first_half if sub == 0 else jnp.logical_not(first_half)
                part = _dot(m, jnp.where(keep, xpair, 0.0).astype(BF16))
                acc = part if acc is None else acc + part
            cs_pair = jnp.where(first_half, cs_cols[0], cs_cols[1])
            tot_pair = cs_pair[last:last + 1, :]
            xd = (xpair * jnp.exp(tot_pair - cs_pair)).astype(BF16)
            sp = None
            for sub in range(2):
                j = base + 2 * pair + sub
                keep = first_half if sub == 0 else jnp.logical_not(first_half)
                part = _dot((bg_t * dt_t[j:j + 1, :]).astype(BF16), jnp.where(keep, xd, jnp.zeros((), BF16)))
                sp = part if sp is None else sp + part
            ys.append(acc + y_off[:, pr * LANES:(pr + 1) * LANES] * jnp.exp(cs_pair))
            s_new.append(sp)
            etot.append(jnp.exp(tot_pair))
        st_scr[g] = s_prev * jnp.concatenate(etot, axis=1) + jnp.concatenate(s_new, axis=1)
    y = jnp.concatenate(ys, axis=1)

    if reverse:
        y = y + yf_ref[...] + dskip_ref[...] * xs
        z = z_ref[...].astype(F32)
        y = y * (z * _sigmoid(z))
        o_ref[...] = (_rms(y) * ng_ref[...]).astype(o_ref.dtype)
    else:
        o_ref[...] = y


def _ssd_pass(conv, dtf, dtb, alog, bsz, seq, reverse, extra=None):
    n = conv.shape[0]
    t = SSD_CHUNK
    nc = seq // t
    bc_blk = SSD_WIDTH // SSD_BC

    def rows(b, c):
        return b * nc + (nc - 1 - c if reverse else c)

    in_specs = [pl.BlockSpec((t, SSD_WIDTH), lambda b, c: (rows(b, c), 0)),
                pl.BlockSpec((t, SSD_BC), lambda b, c: (rows(b, c), bc_blk)),
                pl.BlockSpec((t, LANES), lambda b, c: (rows(b, c), 0)),
                pl.BlockSpec((1, LANES), lambda b, c: (0, 0)),
                pl.BlockSpec((1, LANES), lambda b, c: (0, 0))]
    args = [conv, conv, dtf, dtb, alog]
    if reverse:
        main, yf, dskip, ng = extra
        in_specs += [pl.BlockSpec((t, SSD_WIDTH), lambda b, c: (rows(b, c), 0)),
                     pl.BlockSpec((t, SSD_WIDTH), lambda b, c: (rows(b, c), 0)),
                     pl.BlockSpec((1, SSD_WIDTH), lambda b, c: (0, 0)),
                     pl.BlockSpec((1, SSD_WIDTH), lambda b, c: (0, 0))]
        args += [main, yf, dskip, ng]
        out_dtype = BF16
    else:
        out_dtype = F32
    return pl.pallas_call(
        functools.partial(_ssd_kernel, reverse=reverse),
        grid=(bsz, nc),
        in_specs=in_specs,
        out_specs=pl.BlockSpec((t, SSD_WIDTH), lambda b, c: (rows(b, c), 0)),
        out_shape=jax.ShapeDtypeStruct((n, SSD_WIDTH), out_dtype),
        scratch_shapes=[pltpu.VMEM((SSD_GROUPS, SSD_STATE, SSD_WIDTH // SSD_GROUPS), F32)],
        compiler_params=_cparams(("arbitrary", "arbitrary")),
        name="ssd_bwd" if reverse else "ssd_fwd",
    )(*args)


def _na_kernel(q_ref, k_ref, v_ref, bias_ref, o_ref, s_scr, p_scr, l_scr, *, nrows):
    w = GRID_W
    band = NA_KH * w
    first = _iota((w, LANES), 1) < NA_HEAD_DIM
    zero = jnp.zeros((), BF16)

    def band_start(r):
        return jnp.clip(r - NA_KH // 2, 0, nrows - NA_KH)

    def scores(r, slot):
        rs = band_start(r)
        q = q_ref[pl.ds(pl.multiple_of(r * w, w), w), :] * jnp.asarray(NA_HEAD_DIM ** -0.5, BF16)
        qs = jnp.concatenate([jnp.where(first, q, zero), jnp.where(first, zero, q)], axis=0)
        kb = k_ref[pl.ds(pl.multiple_of(rs * w, w), band), :]
        j0 = NA_KH - 1 - (r - rs)
        bias = jnp.concatenate([bias_ref[0, j0 + 2 * a] for a in range(NA_KH // 2)], axis=1)
        s_scr[slot] = _dot_nt(qs, kb) + bias

    def softmax(slot):
        s = s_scr[slot]
        p = jnp.exp(s - jnp.max(s, axis=-1, keepdims=True))
        l_scr[slot] = jnp.broadcast_to(jnp.sum(p, axis=-1, keepdims=True), (2 * w, LANES))
        p_scr[slot] = p.astype(BF16)

    def values(r, slot):
        rs = band_start(r)
        vb = v_ref[pl.ds(pl.multiple_of(rs * w, w), band), :]
        o = _dot(p_scr[slot], vb) / l_scr[slot]
        o_ref[pl.ds(pl.multiple_of(r * w, w), w), :] = jnp.where(first, o[0:w], o[w:2 * w]).astype(o_ref.dtype)

    s_scr[...] = jnp.zeros_like(s_scr)
    p_scr[...] = jnp.zeros_like(p_scr)
    l_scr[...] = jnp.ones_like(l_scr)

    def body(i2, carry):
        for slot in range(2):
            i = 2 * i2 + slot
            values(jnp.maximum(i - 2, 0), slot)
            softmax(1 - slot)
            scores(jnp.minimum(i, nrows - 1), slot)
        return carry

    lax.fori_loop(0, nrows // 2 + 1, body, 0)


def _na_bias(rpb):
    w = GRID_W
    h, nro, nrel = rpb.shape
    c = np.arange(w)
    kc = np.arange(w)
    col_start = np.clip(c - NA_KW // 2, 0, w - NA_KW)
    valid = (kc[None, :] >= col_start[:, None]) & (kc[None, :] < col_start[:, None] + NA_KW)
    co = kc[None, :] - c[:, None] + NA_KW - 1
    onehot = ((np.arange(nrel)[:, None, None] == co[None]) & valid[None]).astype(np.float32)
    sel = np.zeros((2, nrel, w, 2, w), np.float32)
    sel[0, :, :, 0, :] = onehot
    sel[1, :, :, 1, :] = onehot
    mask = np.broadcast_to(np.where(valid, 0.0, NEG_BIG)[:, None, :], (w, 2, w)).astype(np.float32)
    table = np.concatenate([sel.reshape(2 * nrel, 2 * w * w), mask.reshape(1, 2 * w * w)], axis=0)
    r = rpb.astype(F32).reshape(h // 2, 2, nro, nrel)
    rows = jnp.stack([r[:, :, :nro - 1], r[:, :, 1:]], axis=3).transpose(0, 2, 1, 3, 4)
    rows = rows.reshape(h * (nro - 1), 2 * nrel)
    rows = jnp.concatenate([rows, jnp.ones((rows.shape[0], 1), F32)], axis=1)
    tiles = jnp.dot(rows, table, precision=lax.Precision.HIGHEST)
    return tiles.reshape(h // 2, nro - 1, 2 * w, 2 * w)


def _na(main, bias, bsz, seq, q_blk):
    n = main.shape[0]
    nrows = seq // GRID_W
    assert nrows >= NA_KH
    pairs = NA_HEADS // 2
    return pl.pallas_call(
        functools.partial(_na_kernel, nrows=nrows),
        grid=(bsz, pairs),
        in_specs=[pl.BlockSpec((seq, LANES), lambda b, p: (b, q_blk + p)),
                  pl.BlockSpec((seq, LANES), lambda b, p: (b, q_blk + pairs + p)),
                  pl.BlockSpec((seq, LANES), lambda b, p: (b, q_blk + 2 * pairs + p)),
                  pl.BlockSpec((1,) + bias.shape[1:], lambda b, p: (p, 0, 0, 0))],
        out_specs=pl.BlockSpec((seq, LANES), lambda b, p: (b, p)),
        out_shape=jax.ShapeDtypeStruct((n, NA_WIDTH), BF16),
        scratch_shapes=[pltpu.VMEM((2, 2 * GRID_W, NA_KH * GRID_W), F32),
                        pltpu.VMEM((2, 2 * GRID_W, NA_KH * GRID_W), BF16),
                        pltpu.VMEM((2, 2 * GRID_W, LANES), F32)],
        compiler_params=_cparams(("arbitrary", "arbitrary")),
        name="nbr_attn",
    )(main, main, main, bias)


ML_TAB = 16


def _mlstm_kernel(q_ref, k_ref, v_ref, og_ref, gt_ref, gbt_ref, hg_ref, o_ref,
                  rall_scr, ball_scr, a_scr, w_scr, sc_scr, cp_scr, c_st, n_st, p_scr, psum_scr, *, nchunks):
    t = ML_CHUNK
    nc = nchunks
    head = pl.program_id(1)
    scale = ML_DK ** -0.5
    row = _iota((t, t), 0)
    col = _iota((t, t), 1)
    tris = (col <= row, col >= row)

    @pl.when((pl.program_id(0) == 0) & (head == 0))
    def _():
        rall_scr[...] = jnp.zeros_like(rall_scr)
        ball_scr[...] = jnp.zeros_like(ball_scr)
        for d in range(2):
            rall_scr[ML_TAB * d + 9:ML_TAB * d + 12, :] = jnp.ones((3, nc * t), F32)
            ball_scr[d, ML_TAB * d:ML_TAB * d + 3, :] = jnp.full((3, nc * t), -1.0, F32)

    lane = _iota((nc, t), 1)
    subc = _iota((nc, LANES), 0)
    sub32 = _iota((4 * ML_HEADS, 1), 0)

    def head_rows(kind):
        idx = kind * ML_HEADS + head
        bias = jnp.sum(jnp.where(sub32 == idx, gbt_ref[...], 0.0), axis=0, keepdims=True)
        return gt_ref[0, idx] + bias

    def scan_lanes(x, reverse, op, fill):
        sh = 1
        while sh < t:
            if reverse:
                x = op(x, jnp.where(lane < t - sh, pltpu.roll(x, t - sh, axis=1), fill))
            else:
                x = op(x, jnp.where(lane >= sh, pltpu.roll(x, sh, axis=1), fill))
            sh *= 2
        return x

    for d in range(2):
        ig = head_rows(2 * d)
        b = scan_lanes(_log_sigmoid(head_rows(2 * d + 1)), bool(d), jnp.add, 0.0)
        g_col = b[:, 0:1] if d else b[:, t - 1:t]
        a = g_col - b + ig
        m_col = jnp.max(a, axis=-1, keepdims=True)
        w = jnp.exp(a - m_col)
        rs = ig - b
        run_max = scan_lanes(rs, bool(d), jnp.maximum, NEG_BIG)
        g128 = jnp.broadcast_to(g_col, (nc, LANES))
        l128 = jnp.broadcast_to(m_col, (nc, LANES))
        m = jnp.zeros((1, LANES), F32)
        m_prev = jnp.zeros((nc, LANES), F32)
        s_old = jnp.zeros((nc, LANES), F32)
        s_new = jnp.zeros((nc, LANES), F32)
        for step in range(nc):
            cc = nc - 1 - step if d else step
            g_c = g128[cc:cc + 1, :]
            l_c = l128[cc:cc + 1, :]
            m_next = jnp.maximum(g_c + m, l_c)
            pick = subc == cc
            m_prev = jnp.where(pick, m, m_prev)
            s_old = jnp.where(pick, jnp.exp(g_c + m - m_next), s_old)
            s_new = jnp.where(pick, jnp.exp(l_c - m_next), s_new)
            m = m_next
        mp = m_prev[:, 0:1]
        u = jnp.maximum(mp, run_max)
        inter = jnp.exp(mp - u) * scale
        floor = jnp.exp(-(b + u))
        base = ML_TAB * d
        a_rows = [p.astype(F32) for x in (u, inter, floor) for p in _split3(x)]
        b_rows = [p.astype(F32) for p in _split3(rs)]
        for cc in range(nc):
            ls = slice(cc * t, (cc + 1) * t)
            for j, x in enumerate(a_rows):
                rall_scr[base + j:base + j + 1, ls] = x[cc:cc + 1, :]
            for j, x in enumerate(b_rows):
                ball_scr[d, base + 9 + j:base + 10 + j, ls] = x[cc:cc + 1, :]
            w_scr[d, :, ls] = jnp.broadcast_to(w[cc:cc + 1, :], (SUBLANES, t))
            sc_scr[d, cc, 0] = jnp.broadcast_to(s_old[cc:cc + 1, :], (SUBLANES, LANES))
            sc_scr[d, cc, 1] = jnp.broadcast_to(s_new[cc:cc + 1, :], (SUBLANES, LANES))

    def build_cols(c, carry):
        r0 = pl.multiple_of(c * t, t)
        rows = jnp.concatenate([rall_scr[:, pl.ds(r0, t)], jnp.zeros((LANES - 2 * ML_TAB, t), F32)], axis=0)
        a_scr[pl.ds(r0, t), :] = rows.T.astype(BF16)
        return carry

    lax.fori_loop(0, nc, build_cols, 0)

    c_st[...] = jnp.zeros_like(c_st)
    n_st[...] = jnp.zeros_like(n_st)

    def scan_step(i, carry):
        for d in range(2):
            c = nc - 1 - i if d else i
            r0 = pl.multiple_of(c * t, t)
            kw_t = k_ref[pl.ds(r0, t), :].astype(F32).T * w_scr[d, 0:1, pl.ds(r0, t)]
            s_loc = _dot(kw_t.astype(BF16), v_ref[pl.ds(r0, t), :])
            n_loc = jnp.sum(kw_t, axis=-1, keepdims=True)
            c_prev = c_st[d]
            n_prev = n_st[d]
            cp_scr[d, c] = jnp.concatenate([c_prev, n_prev], axis=1).astype(BF16)
            s_old = sc_scr[d, c, 0][0:1, :]
            s_new = sc_scr[d, c, 1][0:1, :]
            c_st[d] = (jnp.concatenate([s_old, s_old], axis=1) * c_prev
                       + jnp.concatenate([s_new, s_new], axis=1) * s_loc)
            n_st[d] = s_old * n_prev + s_new * n_loc
        return carry

    lax.fori_loop(0, nc, scan_step, 0, unroll=2)

    kk = _iota((LANES, 2 * LANES), 0)
    ll = _iota((LANES, 2 * LANES), 1)
    pad_rows = jnp.zeros((LANES - 2 * ML_TAB, t), BF16)

    def scores(c, slot):
        r0 = pl.multiple_of(c * t, t)
        qk = _dot_nt(q_ref[pl.ds(r0, t), :], k_ref[pl.ds(r0, t), :]) * scale
        cols = a_scr[pl.ds(r0, t), :]
        for d in range(2):
            b_arg = jnp.concatenate([ball_scr[d, :, pl.ds(r0, t)].astype(BF16), pad_rows], axis=0)
            arg = _dot(cols, b_arg)
            sc = qk * jnp.exp(jnp.where(tris[d], arg, NEG_BIG))
            psum_scr[slot, d] = jnp.broadcast_to(jnp.sum(sc, axis=-1, keepdims=True), (t, LANES))
            p_scr[slot, d] = sc.astype(BF16)

    def values(c, slot):
        r0 = pl.multiple_of(c * t, t)
        q = q_ref[pl.ds(r0, t), :]
        v = v_ref[pl.ds(r0, t), :]
        cols = a_scr[pl.ds(r0, t), :]
        hsum = None
        for d in range(2):
            base = ML_TAB * d
            pick = (((ll < LANES) & (kk >= base + 3) & (kk < base + 6))
                    | ((ll >= LANES) & (kk >= base + 6) & (kk < base + 9))).astype(BF16)
            rep = _dot(cols, pick)
            inter = rep[:, 0:LANES]
            ext = _dot(q, cp_scr[d, c])
            den = psum_scr[slot, d] + inter * ext[:, ML_DV:ML_DV + LANES]
            r = 1.0 / jnp.maximum(jnp.abs(den), rep[:, LANES:2 * LANES])
            num = _dot(p_scr[slot, d], v) + jnp.concatenate([inter, inter], axis=1) * ext[:, 0:ML_DV]
            hd = num * jnp.concatenate([r, r], axis=1)
            hsum = hd if hsum is None else hsum + hd
        og = og_ref[pl.ds(r0, t), :].astype(F32)
        o_ref[pl.ds(r0, t), :] = (_sigmoid(og) * (_rms(hsum) * hg_ref[0])).astype(o_ref.dtype)

    p_scr[...] = jnp.zeros_like(p_scr)
    psum_scr[...] = jnp.zeros_like(psum_scr)

    def out_step(i2, carry):
        for slot in range(2):
            i = 2 * i2 + slot
            values(jnp.clip(i - 1, 0, nc - 1), 1 - slot)
            scores(jnp.minimum(i, nc - 1), slot)
        return carry

    lax.fori_loop(0, nc // 2 + 1, out_step, 0)


def _mlstm(main, gates_t, gate_bt, head_g, bsz, seq):
    n = main.shape[0]
    t = ML_CHUNK
    nc = seq // t
    nh = ML_HEADS
    r = gates_t.shape[0]
    gates_c = gates_t.reshape(r, bsz, nc, t).transpose(1, 0, 2, 3)
    return pl.pallas_call(
        functools.partial(_mlstm_kernel, nchunks=nc),
        grid=(bsz, nh),
        in_specs=[pl.BlockSpec((seq, ML_DK), lambda b, h: (b, h)),
                  pl.BlockSpec((seq, ML_DK), lambda b, h: (b, nh + h)),
                  pl.BlockSpec((seq, ML_DV), lambda b, h: (b, nh + h)),
                  pl.BlockSpec((seq, ML_DV), lambda b, h: (b, 2 * nh + h)),
                  pl.BlockSpec((1, r, nc, t), lambda b, h: (b, 0, 0, 0)),
                  pl.BlockSpec((r, 1), lambda b, h: (0, 0)),
                  pl.BlockSpec((1, 1, ML_DV), lambda b, h: (h, 0, 0))],
        out_specs=pl.BlockSpec((seq, ML_DV), lambda b, h: (b, h)),
        out_shape=jax.ShapeDtypeStruct((n, ML_WIDTH), BF16),
        scratch_shapes=[pltpu.VMEM((2 * ML_TAB, seq), F32),
                        pltpu.VMEM((2, 2 * ML_TAB, seq), F32),
                        pltpu.VMEM((seq, LANES), BF16),
                        pltpu.VMEM((2, SUBLANES, seq), F32),
                        pltpu.VMEM((2, nc, 2, SUBLANES, LANES), F32),
                        pltpu.VMEM((2, nc, ML_DK, ML_DV + LANES), BF16),
                        pltpu.VMEM((2, ML_DK, ML_DV), F32),
                        pltpu.VMEM((2, ML_DK, LANES), F32),
                        pltpu.VMEM((2, 2, t, t), BF16),
                        pltpu.VMEM((2, 2, t, LANES), F32)],
        compiler_params=_cparams(("arbitrary", "arbitrary")),
        name="mlstm",
    )(main, main, main, main, gates_c, gate_bt, head_g.reshape(nh, 1, ML_DV))


def _pad_cols(a, width):
    return jnp.pad(a, ((0, 0), (0, width - a.shape[1])))


def _ssd_na_layer(x2, mods, norm_g, w_in, conv_w, conv_b, dt_bias, a_log, d_skip, ssd_norm, rpb, bsz, seq):
    sh1, sc1 = mods
    s1 = SSD_WIDTH
    s2 = s1 + SSD_CONV_CH
    s3 = s2 + 2 * SSD_HEADS
    w_main = jnp.concatenate([w_in[:, :s2], w_in[:, s3:]], axis=1).astype(BF16)
    w_dt = _pad_cols(w_in[:, s2:s3], LANES).astype(BF16)
    main, dtf = _project(x2, norm_g[0], sh1, sc1, w_main, w_dt, 0, seq, tm=min(512, seq), tn=1408,
                         name="proj_ssd_na")

    conv = _conv_silu(main, s1, conv_w, conv_b, seq, tr=min(512, seq), tc=512)
    dtb = _pad_cols(dt_bias.reshape(1, 2 * SSD_HEADS), LANES)
    alog = _pad_cols(a_log.reshape(1, 2 * SSD_HEADS), LANES)
    dskip = jnp.repeat(d_skip, SSD_HEAD_DIM)[None, :]
    y_f = _ssd_pass(conv, dtf, dtb, alog, bsz, seq, reverse=False)
    y_ssd = _ssd_pass(conv, dtf, dtb, alog, bsz, seq, reverse=True,
                      extra=(main, y_f, dskip, ssd_norm[None, :]))

    q_blk = (s1 + SSD_CONV_CH) // LANES
    y_na = _na(main, _na_bias(rpb), bsz, seq, q_blk)
    return y_ssd, 0, y_na, 0


def _mlstm_layer(x2, mods, norm_g, w_in, gate_b, head_g, bsz, seq):
    sh1, sc1 = mods
    wm = 2 * ML_QK + 2 * ML_WIDTH
    w_main = w_in[:, :wm].astype(BF16)
    main, gates_t = _project(x2, norm_g[0], sh1, sc1, w_main, _pad_cols(w_in[:, wm:], LANES).astype(BF16),
                             4 * ML_HEADS, seq, tm=min(512, seq), tn=1536, name="proj_mlstm")
    y = _mlstm(main, gates_t, gate_b.reshape(4 * ML_HEADS, 1).astype(F32), head_g, bsz, seq)
    return y, 0, y, 1


def kernel(x, c, ada_w, ada_b, norm_g, mlp_w1, mlp_w2, ab_w_in, ab_conv_w, ab_conv_b, ab_dt_bias, ab_a_log,
           ab_d_skip, ab_ssd_norm, ab_rpb, ab_w_out, ml_w_in, ml_gate_b, ml_head_norm, ml_w_out):
    bsz, seq, d = x.shape
    depth = ada_w.shape[0]
    mod = _adaln(c, ada_w, ada_b)
    x2 = x.reshape(bsz * seq, d)
    w1_all, w2_all = mlp_w1.astype(BF16), mlp_w2.astype(BF16)
    w_outs = (ab_w_out.astype(BF16), ml_w_out.astype(BF16))
    for layer in range(depth):
        sh1, sc1, g1, sh2, sc2, g2 = [mod[layer, :, i * d:(i + 1) * d] for i in range(6)]
        j = layer // 2
        if layer % 2 == 0:
            mixed = _ssd_na_layer(x2, (sh1, sc1), norm_g[layer], ab_w_in[j], ab_conv_w[j], ab_conv_b[j],
                                  ab_dt_bias[j], ab_a_log[j], ab_d_skip[j], ab_ssd_norm[j], ab_rpb[j], bsz, seq)
        else:
            mixed = _mlstm_layer(x2, (sh1, sc1), norm_g[layer], ml_w_in[j], ml_gate_b[j], ml_head_norm[j],
                                 bsz, seq)
        x2 = _tail(*mixed, w_outs[layer % 2], j, x2, jnp.stack([g1, sh2, sc2, g2], axis=1), norm_g[layer, 1:4],
                   w1_all, w2_all, layer, seq, tm=min(512, seq), tf=1024, name="tail%d" % layer)
    return x2.reshape(bsz, seq, d)
```

```python
import functools

import numpy as np
import jax
import jax.numpy as jnp
from jax import lax
from jax.experimental import pallas as pl
from jax.experimental.pallas import tpu as pltpu

F32 = jnp.float32
BF16 = jnp.bfloat16

NORM_EPS = 1e-6
GRID_W = 64

SSD_HEAD_DIM = 64
SSD_HEADS = 16
SSD_GROUPS = 2
SSD_STATE = 128
SSD_CONV = 5
SSD_CHUNK = 128
SSD_BLOCK_CHUNKS = 4
SSD_WIDTH = SSD_HEADS * SSD_HEAD_DIM
SSD_BC = 2 * SSD_GROUPS * SSD_STATE
SSD_CONV_CH = SSD_WIDTH + SSD_BC

NA_HEAD_DIM = 64
NA_HEADS = 16
NA_KH = 8
NA_KW = 16
NA_WIDTH = NA_HEADS * NA_HEAD_DIM

ML_HEADS = 8
ML_DV = 256
ML_DK = 128
ML_CHUNK = 256
ML_WIDTH = ML_HEADS * ML_DV
ML_QK = ML_HEADS * ML_DK

LANES = 128
SUBLANES = 8
HALO = 16
NEG_BIG = -1e30
VMEM_LIMIT = 56 * 1024 * 1024


def _cparams(sem):
    return pltpu.CompilerParams(dimension_semantics=sem, vmem_limit_bytes=VMEM_LIMIT)


def _dot(a, b):
    return jnp.dot(a, b, preferred_element_type=F32)


def _dot_nt(a, b):
    return lax.dot_general(a, b, (((1,), (1,)), ((), ())), preferred_element_type=F32)


def _split3(x):
    hi = x.astype(BF16)
    r1 = x - hi.astype(F32)
    mid = r1.astype(BF16)
    lo = (r1 - mid.astype(F32)).astype(BF16)
    return hi, mid, lo


def _dot_exact_lhs01(sel, x):
    hi, mid, lo = _split3(x)
    return _dot(sel, hi) + _dot(sel, mid) + _dot(sel, lo)


def _iota(shape, dim):
    return lax.broadcasted_iota(jnp.int32, shape, dim)


def _sigmoid(x):
    return 0.5 * (jnp.tanh(0.5 * x) + 1.0)


def _softplus(x):
    return jnp.maximum(x, 0.0) + jnp.log(1.0 + jnp.exp(-jnp.abs(x)))


def _log_sigmoid(x):
    return jnp.minimum(x, 0.0) - jnp.log(1.0 + jnp.exp(-jnp.abs(x)))


def _rms(x):
    return x * lax.rsqrt(jnp.mean(x * x, axis=-1, keepdims=True) + NORM_EPS)


def _mod_kernel(c_ref, w_ref, b_ref, o_ref):
    c = c_ref[...]
    cond = c * _sigmoid(c)
    o_ref[0] = _dot(cond.astype(BF16), w_ref[0].astype(BF16)) + b_ref[0]


def _adaln(c, ada_w, ada_b):
    depth, d, d6 = ada_w.shape
    bsz = c.shape[0]
    tn = 1024
    return pl.pallas_call(
        _mod_kernel,
        grid=(depth, d6 // tn),
        in_specs=[pl.BlockSpec((bsz, d), lambda l, j: (0, 0)),
                  pl.BlockSpec((1, d, tn), lambda l, j: (l, 0, j)),
                  pl.BlockSpec((1, 1, tn), lambda l, j: (l, 0, j))],
        out_specs=pl.BlockSpec((1, bsz, tn), lambda l, j: (l, 0, j)),
        out_shape=jax.ShapeDtypeStruct((depth, bsz, d6), F32),
        compiler_params=_cparams(("arbitrary", "arbitrary")),
        name="adaln",
    )(c, ada_w, ada_b.reshape(depth, 1, d6))


def _proj_kernel(x_ref, g_ref, sh_ref, sc_ref, w_ref, ws_ref, o_ref, os_ref, *, small_transposed, tn):
    hb = (_rms(x_ref[...]) * g_ref[...] * (1.0 + sc_ref[0]) + sh_ref[0]).astype(BF16)
    small = _dot(hb, ws_ref[...])
    if small_transposed:
        os_ref[...] = small.T[0:os_ref.shape[0], :]
    else:
        os_ref[...] = small
    for j in range(w_ref.shape[1] // tn):
        o_ref[:, j * tn:(j + 1) * tn] = _dot(hb, w_ref[:, j * tn:(j + 1) * tn]).astype(o_ref.dtype)


def _project(x2, g, shift, scale, w, ws, small_rows, seq, tm, tn, name):
    n, d = x2.shape
    wn = w.shape[1]
    tiles_per_batch = seq // tm
    bsz = shift.shape[0]
    if small_rows:
        small_spec, small_shape = pl.BlockSpec((small_rows, tm), lambda i: (0, i)), (small_rows, n)
    else:
        small_spec, small_shape = pl.BlockSpec((tm, LANES), lambda i: (i, 0)), (n, LANES)
    resident = pl.Buffered(1)
    return pl.pallas_call(
        functools.partial(_proj_kernel, small_transposed=bool(small_rows), tn=tn),
        grid=(n // tm,),
        in_specs=[pl.BlockSpec((tm, d), lambda i: (i, 0)),
                  pl.BlockSpec((1, d), lambda i: (0, 0)),
                  pl.BlockSpec((1, 1, d), lambda i: (i // tiles_per_batch, 0, 0)),
                  pl.BlockSpec((1, 1, d), lambda i: (i // tiles_per_batch, 0, 0)),
                  pl.BlockSpec((d, wn), lambda i: (0, 0), pipeline_mode=resident),
                  pl.BlockSpec(ws.shape, lambda i: (0, 0), pipeline_mode=resident)],
        out_specs=[pl.BlockSpec((tm, wn), lambda i: (i, 0)), small_spec],
        out_shape=[jax.ShapeDtypeStruct((n, wn), BF16), jax.ShapeDtypeStruct(small_shape, F32)],
        compiler_params=_cparams(("arbitrary",)),
        name=name,
    )(x2, g.reshape(1, d), shift.reshape(bsz, 1, d), scale.reshape(bsz, 1, d), w, ws)


def _tail_kernel(ya_ref, yb_ref, wa_ref, wb_ref, x_ref, mod_ref, ng_ref, w1_ref, w2_ref, o_ref, *, tf):
    mixed = _dot(ya_ref[...], wa_ref[...]) + _dot(yb_ref[...], wb_ref[...])
    x = x_ref[...] + mod_ref[0, 0:1, :] * (_rms(mixed) * ng_ref[0:1, :])
    h = (_rms(x) * ng_ref[1:2, :] * (1.0 + mod_ref[0, 2:3, :]) + mod_ref[0, 1:2, :]).astype(BF16)
    u = None
    for f in range(w1_ref.shape[1] // tf):
        a = jnp.maximum(_dot(h, w1_ref[:, f * tf:(f + 1) * tf]), 0.0)
        part = _dot((a * a).astype(BF16), w2_ref[f * tf:(f + 1) * tf, :])
        u = part if u is None else u + part
    o_ref[...] = x + mod_ref[0, 3:4, :] * (_rms(u) * ng_ref[2:3, :])


def _tail(ya, ia, yb, ib, w_out, jo, x2, mods, gains, w1, w2, layer, seq, tm, tf, name):
    n, d = x2.shape
    kh = w_out.shape[1] // 2
    dff = w1.shape[2]
    tiles_per_batch = seq // tm
    resident = pl.Buffered(1)
    return pl.pallas_call(
        functools.partial(_tail_kernel, tf=tf),
        grid=(n // tm,),
        in_specs=[pl.BlockSpec((tm, kh), lambda i: (i, ia)),
                  pl.BlockSpec((tm, kh), lambda i: (i, ib)),
                  pl.BlockSpec((None, kh, d), lambda i: (jo, 0, 0), pipeline_mode=resident),
                  pl.BlockSpec((None, kh, d), lambda i: (jo, 1, 0), pipeline_mode=resident),
                  pl.BlockSpec((tm, d), lambda i: (i, 0)),
                  pl.BlockSpec((1, 4, d), lambda i: (i // tiles_per_batch, 0, 0)),
                  pl.BlockSpec((3, d), lambda i: (0, 0)),
                  pl.BlockSpec((None, d, dff), lambda i: (layer, 0, 0), pipeline_mode=resident),
                  pl.BlockSpec((None, dff, d), lambda i: (layer, 0, 0), pipeline_mode=resident)],
        out_specs=pl.BlockSpec((tm, d), lambda i: (i, 0)),
        out_shape=jax.ShapeDtypeStruct((n, d), F32),
        compiler_params=_cparams(("arbitrary",)),
        name=name,
    )(ya, yb, w_out, w_out, x2, mods, gains, w1, w2)


def _conv_kernel(main_ref, prev_ref, next_ref, w_ref, b_ref, o_ref, scr, *, blocks_per_seq):
    t = main_ref.shape[0]
    i = pl.program_id(0) % blocks_per_seq
    keep_prev = (i > 0).astype(F32)
    keep_next = (i < blocks_per_seq - 1).astype(F32)
    scr[0:HALO, :] = prev_ref[...].astype(F32) * keep_prev
    scr[HALO:HALO + t, :] = main_ref[...].astype(F32)
    scr[HALO + t:2 * HALO + t, :] = next_ref[...].astype(F32) * keep_next
    pad = SSD_CONV // 2
    acc = b_ref[...] + w_ref[0:1, :] * scr[HALO - pad:HALO - pad + t, :]
    for k in range(1, SSD_CONV):
        acc = acc + w_ref[k:k + 1, :] * scr[HALO - pad + k:HALO - pad + k + t, :]
    o_ref[...] = (acc * _sigmoid(acc)).astype(o_ref.dtype)


def _conv_silu(main, col0, conv_w, conv_b, seq, tr, tc):
    n = main.shape[0]
    ch = conv_w.shape[1]
    cb0 = col0 // tc
    rpb = tr // HALO
    last_halo = n // HALO - 1
    return pl.pallas_call(
        functools.partial(_conv_kernel, blocks_per_seq=seq // tr),
        grid=(n // tr, ch // tc),
        in_specs=[pl.BlockSpec((tr, tc), lambda i, j: (i, cb0 + j)),
                  pl.BlockSpec((HALO, tc), lambda i, j: (jnp.maximum(i * rpb - 1, 0), cb0 + j)),
                  pl.BlockSpec((HALO, tc), lambda i, j: (jnp.minimum((i + 1) * rpb, last_halo), cb0 + j)),
                  pl.BlockSpec((SSD_CONV, tc), lambda i, j: (0, j)),
                  pl.BlockSpec((1, tc), lambda i, j: (0, j))],
        out_specs=pl.BlockSpec((tr, tc), lambda i, j: (i, j)),
        out_shape=jax.ShapeDtypeStruct((n, ch), BF16),
        scratch_shapes=[pltpu.VMEM((tr + 2 * HALO, tc), F32)],
        compiler_params=_cparams(("arbitrary", "arbitrary")),
        name="conv_silu",
    )(main, main, main, conv_w, conv_b.reshape(1, ch))


def _ssd_kernel(xs_ref, bc_ref, dt_ref, dtb_ref, alog_ref, *rest, reverse):
    t = SSD_CHUNK

    @pl.when(pl.program_id(1) == 0)
    def _():
        rest[-1][...] = jnp.zeros_like(rest[-1])

    subs = range(xs_ref.shape[0] // t)
    for sub in (reversed(subs) if reverse else subs):
        rows = pl.ds(sub * t, t)
        per_row = [r.at[rows] for r in rest[:2]] + list(rest[2:4]) if reverse else []
        _ssd_chunk(xs_ref.at[rows], bc_ref.at[rows], dt_ref.at[rows], dtb_ref, alog_ref,
                   *per_row, rest[-2].at[rows], rest[-1], reverse=reverse)


def _ssd_chunk(xs_ref, bc_ref, dt_ref, dtb_ref, alog_ref, *rest, reverse):
    if reverse:
        z_ref, yf_ref, dskip_ref, ng_ref, o_ref, st_scr = rest
    else:
        o_ref, st_scr = rest
    t = SSD_CHUNK
    hd = SSD_HEAD_DIM
    hpg = SSD_HEADS // SSD_GROUPS

    xs = xs_ref[...].astype(F32)
    bc = bc_ref[...]

    dt = _softplus(dt_ref[...] + dtb_ref[...])
    adt = dt * (-jnp.exp(alog_ref[...]))
    row = _iota((t, t), 0)
    col = _iota((t, t), 1)
    tri = (col >= row) if reverse else (col <= row)
    cs = _dot_exact_lhs01(tri.astype(BF16), adt)
    cs_t = cs.T
    dt_t = dt.T
    base = SSD_HEADS if reverse else 0
    last = 0 if reverse else t - 1

    first_half = _iota((t, LANES), 1) < hd
    ys = []
    for g in range(SSD_GROUPS):
        bg = bc[:, g * SSD_STATE:(g + 1) * SSD_STATE]
        cg = bc[:, (SSD_GROUPS + g) * SSD_STATE:(SSD_GROUPS + g + 1) * SSD_STATE]
        cb = _dot_nt(cg, bg)
        bg_t = bg.astype(F32).T
        s_prev = st_scr[g]
        y_off = _dot(cg, s_prev.astype(BF16))
        s_new = []
        etot = []
        for pr in range(hpg // 2):
            pair = g * (hpg // 2) + pr
            xpair = xs[:, pair * LANES:(pair + 1) * LANES]
            cs_cols = []
            acc = None
            for sub in range(2):
                j = base + 2 * pair + sub
                cs_col = jnp.broadcast_to(cs[:, j:j + 1], (t, t))
                cs_cols.append(cs_col)
                dts = dt_t[j:j + 1, :]
                lmat = jnp.exp(jnp.where(tri, cs_col - cs_t[j:j + 1, :], NEG_BIG))
                m = (cb * lmat * dts).astype(BF16)
                keep = first_half if sub == 0 else jnp.logical_not(first_half)
                part = _dot(m, jnp.where(keep, xpair, 0.0).astype(BF16))
                acc = part if acc is None else acc + part
            cs_pair = jnp.where(first_half, cs_cols[0], cs_cols[1])
            tot_pair = cs_pair[last:last + 1, :]
            xd = (xpair * jnp.exp(tot_pair - cs_pair)).astype(BF16)
            sp = None
            for sub in range(2):
                j = base + 2 * pair + sub
                keep = first_half if sub == 0 else jnp.logical_not(first_half)
                part = _dot((bg_t * dt_t[j:j + 1, :]).astype(BF16), jnp.where(keep, xd, jnp.zeros((), BF16)))
                sp = part if sp is None else sp + part
            ys.append(acc + y_off[:, pr * LANES:(pr + 1) * LANES] * jnp.exp(cs_pair))
            s_new.append(sp)
            etot.append(jnp.exp(tot_pair))
        st_scr[g] = s_prev * jnp.concatenate(etot, axis=1) + jnp.concatenate(s_new, axis=1)
    y = jnp.concatenate(ys, axis=1)

    if reverse:
        y = y + yf_ref[...] + dskip_ref[...] * xs
        z = z_ref[...].astype(F32)
        y = y * (z * _sigmoid(z))
        o_ref[...] = (_rms(y) * ng_ref[...]).astype(o_ref.dtype)
    else:
        o_ref[...] = y


def _ssd_pass(conv, dtf, dtb, alog, bsz, seq, reverse, extra=None):
    n = conv.shape[0]
    t = SSD_BLOCK_CHUNKS * SSD_CHUNK
    nc = seq // t
    bc_blk = SSD_WIDTH // SSD_BC

    def rows(b, c):
        return b * nc + (nc - 1 - c if reverse else c)

    in_specs = [pl.BlockSpec((t, SSD_WIDTH), lambda b, c: (rows(b, c), 0)),
                pl.BlockSpec((t, SSD_BC), lambda b, c: (rows(b, c), bc_blk)),
                pl.BlockSpec((t, LANES), lambda b, c: (rows(b, c), 0)),
                pl.BlockSpec((1, LANES), lambda b, c: (0, 0)),
                pl.BlockSpec((1, LANES), lambda b, c: (0, 0))]
    args = [conv, conv, dtf, dtb, alog]
    if reverse:
        main, yf, dskip, ng = extra
        in_specs += [pl.BlockSpec((t, SSD_WIDTH), lambda b, c: (rows(b, c), 0)),
                     pl.BlockSpec((t, SSD_WIDTH), lambda b, c: (rows(b, c), 0)),
                     pl.BlockSpec((1, SSD_WIDTH), lambda b, c: (0, 0)),
                     pl.BlockSpec((1, SSD_WIDTH), lambda b, c: (0, 0))]
        args += [main, yf, dskip, ng]
        out_dtype = BF16
    else:
        out_dtype = F32
    return pl.pallas_call(
        functools.partial(_ssd_kernel, reverse=reverse),
        grid=(bsz, nc),
        in_specs=in_specs,
        out_specs=pl.BlockSpec((t, SSD_WIDTH), lambda b, c: (rows(b, c), 0)),
        out_shape=jax.ShapeDtypeStruct((n, SSD_WIDTH), out_dtype),
        scratch_shapes=[pltpu.VMEM((SSD_GROUPS, SSD_STATE, SSD_WIDTH // SSD_GROUPS), F32)],
        compiler_params=_cparams(("arbitrary", "arbitrary")),
        name="ssd_bwd" if reverse else "ssd_fwd",
    )(*args)


def _na_kernel(q_ref, k_ref, v_ref, bias_ref, o_ref, s_scr, p_scr, l_scr, *, nrows):
    w = GRID_W
    band = NA_KH * w
    first = _iota((w, LANES), 1) < NA_HEAD_DIM
    zero = jnp.zeros((), BF16)

    def band_start(r):
        return jnp.clip(r - NA_KH // 2, 0, nrows - NA_KH)

    def scores(r, slot):
        rs = band_start(r)
        q = q_ref[pl.ds(pl.multiple_of(r * w, w), w), :] * jnp.asarray(NA_HEAD_DIM ** -0.5, BF16)
        qs = jnp.concatenate([jnp.where(first, q, zero), jnp.where(first, zero, q)], axis=0)
        kb = k_ref[pl.ds(pl.multiple_of(rs * w, w), band), :]
        j0 = NA_KH - 1 - (r - rs)
        bias = jnp.concatenate([bias_ref[0, j0 + 2 * a] for a in range(NA_KH // 2)], axis=1)
        s_scr[slot] = _dot_nt(qs, kb) + bias

    def softmax(slot):
        s = s_scr[slot]
        p = jnp.exp(s - jnp.max(s, axis=-1, keepdims=True))
        l_scr[slot] = jnp.broadcast_to(jnp.sum(p, axis=-1, keepdims=True), (2 * w, LANES))
        p_scr[slot] = p.astype(BF16)

    def values(r, slot):
        rs = band_start(r)
        vb = v_ref[pl.ds(pl.multiple_of(rs * w, w), band), :]
        o = _dot(p_scr[slot], vb) / l_scr[slot]
        o_ref[pl.ds(pl.multiple_of(r * w, w), w), :] = jnp.where(first, o[0:w], o[w:2 * w]).astype(o_ref.dtype)

    s_scr[...] = jnp.zeros_like(s_scr)
    p_scr[...] = jnp.zeros_like(p_scr)
    l_scr[...] = jnp.ones_like(l_scr)

    def body(i2, carry):
        for slot in range(2):
            i = 2 * i2 + slot
            values(jnp.maximum(i - 2, 0), slot)
            softmax(1 - slot)
            scores(jnp.minimum(i, nrows - 1), slot)
        return carry

    lax.fori_loop(0, nrows // 2 + 1, body, 0)


def _na_bias(rpb):
    w = GRID_W
    h, nro, nrel = rpb.shape
    c = np.arange(w)
    kc = np.arange(w)
    col_start = np.clip(c - NA_KW // 2, 0, w - NA_KW)
    valid = (kc[None, :] >= col_start[:, None]) & (kc[None, :] < col_start[:, None] + NA_KW)
    co = kc[None, :] - c[:, None] + NA_KW - 1
    onehot = ((np.arange(nrel)[:, None, None] == co[None]) & valid[None]).astype(np.float32)
    sel = np.zeros((2, nrel, w, 2, w), np.float32)
    sel[0, :, :, 0, :] = onehot
    sel[1, :, :, 1, :] = onehot
    mask = np.broadcast_to(np.where(valid, 0.0, NEG_BIG)[:, None, :], (w, 2, w)).astype(np.float32)
    table = np.concatenate([sel.reshape(2 * nrel, 2 * w * w), mask.reshape(1, 2 * w * w)], axis=0)
    r = rpb.astype(F32).reshape(h // 2, 2, nro, nrel)
    rows = jnp.stack([r[:, :, :nro - 1], r[:, :, 1:]], axis=3).transpose(0, 2, 1, 3, 4)
    rows = rows.reshape(h * (nro - 1), 2 * nrel)
    rows = jnp.concatenate([rows, jnp.ones((rows.shape[0], 1), F32)], axis=1)
    tiles = jnp.dot(rows, table, precision=lax.Precision.HIGHEST)
    return tiles.reshape(h // 2, nro - 1, 2 * w, 2 * w)


def _na(main, bias, bsz, seq, q_blk):
    n = main.shape[0]
    nrows = seq // GRID_W
    assert nrows >= NA_KH
    pairs = NA_HEADS // 2
    return pl.pallas_call(
        functools.partial(_na_kernel, nrows=nrows),
        grid=(bsz, pairs),
        in_specs=[pl.BlockSpec((seq, LANES), lambda b, p: (b, q_blk + p)),
                  pl.BlockSpec((seq, LANES), lambda b, p: (b, q_blk + pairs + p)),
                  pl.BlockSpec((seq, LANES), lambda b, p: (b, q_blk + 2 * pairs + p)),
                  pl.BlockSpec((1,) + bias.shape[1:], lambda b, p: (p, 0, 0, 0))],
        out_specs=pl.BlockSpec((seq, LANES), lambda b, p: (b, p)),
        out_shape=jax.ShapeDtypeStruct((n, NA_WIDTH), BF16),
        scratch_shapes=[pltpu.VMEM((2, 2 * GRID_W, NA_KH * GRID_W), F32),
                        pltpu.VMEM((2, 2 * GRID_W, NA_KH * GRID_W), BF16),
                        pltpu.VMEM((2, 2 * GRID_W, LANES), F32)],
        compiler_params=_cparams(("arbitrary", "arbitrary")),
        name="nbr_attn",
    )(main, main, main, bias)


ML_TAB = 16


def _mlstm_kernel(q_ref, k_ref, v_ref, og_ref, gt_ref, gbt_ref, hg_ref, o_ref,
                  rall_scr, ball_scr, a_scr, w_scr, sc_scr, cp_scr, c_st, n_st, p_scr, psum_scr, *, nchunks):
    t = ML_CHUNK
    nc = nchunks
    head = pl.program_id(1)
    scale = ML_DK ** -0.5
    row = _iota((t, t), 0)
    col = _iota((t, t), 1)
    tris = (col <= row, col >= row)

    @pl.when((pl.program_id(0) == 0) & (head == 0))
    def _():
        rall_scr[...] = jnp.zeros_like(rall_scr)
        ball_scr[...] = jnp.zeros_like(ball_scr)
        a_scr[...] = jnp.zeros_like(a_scr)
        for d in range(2):
            rall_scr[ML_TAB * d + 9:ML_TAB * d + 12, :] = jnp.ones((3, nc * t), F32)
            ball_scr[d, ML_TAB * d:ML_TAB * d + 3, :] = jnp.full((3, nc * t), -1.0, F32)

    lane = _iota((nc, t), 1)
    subc = _iota((nc, LANES), 0)
    sub32 = _iota((4 * ML_HEADS, 1), 0)

    def head_rows(kind):
        idx = kind * ML_HEADS + head
        bias = jnp.sum(jnp.where(sub32 == idx, gbt_ref[...], 0.0), axis=0, keepdims=True)
        return gt_ref[0, idx] + bias

    def scan_lanes(x, reverse, op, fill):
        sh = 1
        while sh < t:
            if reverse:
                x = op(x, jnp.where(lane < t - sh, pltpu.roll(x, t - sh, axis=1), fill))
            else:
                x = op(x, jnp.where(lane >= sh, pltpu.roll(x, sh, axis=1), fill))
            sh *= 2
        return x

    for d in range(2):
        ig = head_rows(2 * d)
        b = scan_lanes(_log_sigmoid(head_rows(2 * d + 1)), bool(d), jnp.add, 0.0)
        g_col = b[:, 0:1] if d else b[:, t - 1:t]
        a = g_col - b + ig
        m_col = jnp.max(a, axis=-1, keepdims=True)
        w = jnp.exp(a - m_col)
        rs = ig - b
        run_max = scan_lanes(rs, bool(d), jnp.maximum, NEG_BIG)
        g128 = jnp.broadcast_to(g_col, (nc, LANES))
        l128 = jnp.broadcast_to(m_col, (nc, LANES))
        m = jnp.zeros((1, LANES), F32)
        m_prev = jnp.zeros((nc, LANES), F32)
        s_old = jnp.zeros((nc, LANES), F32)
        s_new = jnp.zeros((nc, LANES), F32)
        for step in range(nc):
            cc = nc - 1 - step if d else step
            g_c = g128[cc:cc + 1, :]
            l_c = l128[cc:cc + 1, :]
            m_next = jnp.maximum(g_c + m, l_c)
            pick = subc == cc
            m_prev = jnp.where(pick, m, m_prev)
            s_old = jnp.where(pick, jnp.exp(g_c + m - m_next), s_old)
            s_new = jnp.where(pick, jnp.exp(l_c - m_next), s_new)
            m = m_next
        mp = m_prev[:, 0:1]
        u = jnp.maximum(mp, run_max)
        inter = jnp.exp(mp - u) * scale
        floor = jnp.exp(-(b + u))
        base = ML_TAB * d
        a_rows = [p.astype(F32) for x in (u, inter, floor) for p in _split3(x)]
        b_rows = [p.astype(F32) for p in _split3(rs)]
        for cc in range(nc):
            ls = slice(cc * t, (cc + 1) * t)
            for j, x in enumerate(a_rows):
                rall_scr[base + j:base + j + 1, ls] = x[cc:cc + 1, :]
            for j, x in enumerate(b_rows):
                ball_scr[d, base + 9 + j:base + 10 + j, ls] = x[cc:cc + 1, :]
            w_scr[d, :, ls] = jnp.broadcast_to(w[cc:cc + 1, :], (SUBLANES, t))
            sc_scr[d, cc, 0] = jnp.broadcast_to(s_old[cc:cc + 1, :], (SUBLANES, LANES))
            sc_scr[d, cc, 1] = jnp.broadcast_to(s_new[cc:cc + 1, :], (SUBLANES, LANES))

    def build_cols(c, carry):
        r0 = pl.multiple_of(c * t, t)
        cols = rall_scr[:, pl.ds(r0, t)].T
        a_scr[pl.ds(r0, t), 0:2 * ML_TAB] = cols.astype(BF16)
        return carry

    lax.fori_loop(0, nc, build_cols, 0)

    c_st[...] = jnp.zeros_like(c_st)
    n_st[...] = jnp.zeros_like(n_st)

    def scan_step(i, carry):
        for d in range(2):
            c = nc - 1 - i if d else i
            r0 = pl.multiple_of(c * t, t)
            kw_t = k_ref[pl.ds(r0, t), :].astype(F32).T * w_scr[d, 0:1, pl.ds(r0, t)]
            s_loc = _dot(kw_t.astype(BF16), v_ref[pl.ds(r0, t), :])
            n_loc = jnp.sum(kw_t, axis=-1, keepdims=True)
            c_prev = c_st[d]
            n_prev = n_st[d]
            cp_scr[d, c] = jnp.concatenate([c_prev, n_prev], axis=1).astype(BF16)
            s_old = sc_scr[d, c, 0][0:1, :]
            s_new = sc_scr[d, c, 1][0:1, :]
            c_st[d] = (jnp.concatenate([s_old, s_old], axis=1) * c_prev
                       + jnp.concatenate([s_new, s_new], axis=1) * s_loc)
            n_st[d] = s_old * n_prev + s_new * n_loc
        return carry

    lax.fori_loop(0, nc, scan_step, 0, unroll=2)

    kk = _iota((LANES, 2 * LANES), 0)
    ll = _iota((LANES, 2 * LANES), 1)
    pad_rows = jnp.zeros((LANES - 2 * ML_TAB, t), BF16)

    def scores(c, slot):
        r0 = pl.multiple_of(c * t, t)
        qk = _dot_nt(q_ref[pl.ds(r0, t), :], k_ref[pl.ds(r0, t), :]) * scale
        cols = a_scr[pl.ds(r0, t), :]
        for d in range(2):
            b_arg = jnp.concatenate([ball_scr[d, :, pl.ds(r0, t)].astype(BF16), pad_rows], axis=0)
            arg = _dot(cols, b_arg)
            sc = qk * jnp.exp(jnp.where(tris[d], arg, NEG_BIG))
            psum_scr[slot, d] = jnp.broadcast_to(jnp.sum(sc, axis=-1, keepdims=True), (t, LANES))
            p_scr[slot, d] = sc.astype(BF16)

    def values(c, slot):
        r0 = pl.multiple_of(c * t, t)
        q = q_ref[pl.ds(r0, t), :]
        v = v_ref[pl.ds(r0, t), :]
        cols = a_scr[pl.ds(r0, t), :]
        hsum = None
        for d in range(2):
            base = ML_TAB * d
            pick = (((ll < LANES) & (kk >= base + 3) & (kk < base + 6))
                    | ((ll >= LANES) & (kk >= base + 6) & (kk < base + 9))).astype(BF16)
            rep = _dot(cols, pick)
            inter = rep[:, 0:LANES]
            ext = _dot(q, cp_scr[d, c])
            den = psum_scr[slot, d] + inter * ext[:, ML_DV:ML_DV + LANES]
            r = 1.0 / jnp.maximum(jnp.abs(den), rep[:, LANES:2 * LANES])
            num = _dot(p_scr[slot, d], v) + jnp.concatenate([inter, inter], axis=1) * ext[:, 0:ML_DV]
            hd = num * jnp.concatenate([r, r], axis=1)
            hsum = hd if hsum is None else hsum + hd
        og = og_ref[pl.ds(r0, t), :].astype(F32)
        o_ref[pl.ds(r0, t), :] = (_sigmoid(og) * (_rms(hsum) * hg_ref[0])).astype(o_ref.dtype)

    p_scr[...] = jnp.zeros_like(p_scr)
    psum_scr[...] = jnp.zeros_like(psum_scr)

    def out_step(i2, carry):
        for slot in range(2):
            i = 2 * i2 + slot
            values(jnp.clip(i - 1, 0, nc - 1), 1 - slot)
            scores(jnp.minimum(i, nc - 1), slot)
        return carry

    lax.fori_loop(0, nc // 2 + 1, out_step, 0)


def _mlstm(main, gates_t, gate_bt, head_g, bsz, seq):
    n = main.shape[0]
    t = ML_CHUNK
    nc = seq // t
    nh = ML_HEADS
    r = gates_t.shape[0]
    gates_c = gates_t.reshape(r, bsz, nc, t).transpose(1, 0, 2, 3)
    return pl.pallas_call(
        functools.partial(_mlstm_kernel, nchunks=nc),
        grid=(bsz, nh),
        in_specs=[pl.BlockSpec((seq, ML_DK), lambda b, h: (b, h)),
                  pl.BlockSpec((seq, ML_DK), lambda b, h: (b, nh + h)),
                  pl.BlockSpec((seq, ML_DV), lambda b, h: (b, nh + h)),
                  pl.BlockSpec((seq, ML_DV), lambda b, h: (b, 2 * nh + h)),
                  pl.BlockSpec((1, r, nc, t), lambda b, h: (b, 0, 0, 0)),
                  pl.BlockSpec((r, 1), lambda b, h: (0, 0)),
                  pl.BlockSpec((1, 1, ML_DV), lambda b, h: (h, 0, 0))],
        out_specs=pl.BlockSpec((seq, ML_DV), lambda b, h: (b, h)),
        out_shape=jax.ShapeDtypeStruct((n, ML_WIDTH), BF16),
        scratch_shapes=[pltpu.VMEM((2 * ML_TAB, seq), F32),
                        pltpu.VMEM((2, 2 * ML_TAB, seq), F32),
                        pltpu.VMEM((seq, LANES), BF16),
                        pltpu.VMEM((2, SUBLANES, seq), F32),
                        pltpu.VMEM((2, nc, 2, SUBLANES, LANES), F32),
                        pltpu.VMEM((2, nc, ML_DK, ML_DV + LANES), BF16),
                        pltpu.VMEM((2, ML_DK, ML_DV), F32),
                        pltpu.VMEM((2, ML_DK, LANES), F32),
                        pltpu.VMEM((2, 2, t, t), BF16),
                        pltpu.VMEM((2, 2, t, LANES), F32)],
        compiler_params=_cparams(("arbitrary", "arbitrary")),
        name="mlstm",
    )(main, main, main, main, gates_c, gate_bt, head_g.reshape(nh, 1, ML_DV))


def _pad_cols(a, width):
    return jnp.pad(a, ((0, 0), (0, width - a.shape[1])))


def _ssd_na_layer(x2, mods, norm_g, w_in, conv_w, conv_b, dt_bias, a_log, d_skip, ssd_norm, rpb, bsz, seq):
    sh1, sc1 = mods
    s1 = SSD_WIDTH
    s2 = s1 + SSD_CONV_CH
    s3 = s2 + 2 * SSD_HEADS
    w_main = jnp.concatenate([w_in[:, :s2], w_in[:, s3:]], axis=1).astype(BF16)
    w_dt = _pad_cols(w_in[:, s2:s3], LANES).astype(BF16)
    main, dtf = _project(x2, norm_g[0], sh1, sc1, w_main, w_dt, 0, seq, tm=min(512, seq), tn=1408,
                         name="proj_ssd_na")

    conv = _conv_silu(main, s1, conv_w, conv_b, seq, tr=min(512, seq), tc=512)
    dtb = _pad_cols(dt_bias.reshape(1, 2 * SSD_HEADS), LANES)
    alog = _pad_cols(a_log.reshape(1, 2 * SSD_HEADS), LANES)
    dskip = jnp.repeat(d_skip, SSD_HEAD_DIM)[None, :]
    y_f = _ssd_pass(conv, dtf, dtb, alog, bsz, seq, reverse=False)
    y_ssd = _ssd_pass(conv, dtf, dtb, alog, bsz, seq, reverse=True,
                      extra=(main, y_f, dskip, ssd_norm[None, :]))

    q_blk = (s1 + SSD_CONV_CH) // LANES
    y_na = _na(main, _na_bias(rpb), bsz, seq, q_blk)
    return y_ssd, 0, y_na, 0


def _mlstm_layer(x2, mods, norm_g, w_in, gate_b, head_g, bsz, seq):
    sh1, sc1 = mods
    wm = 2 * ML_QK + 2 * ML_WIDTH
    w_main = w_in[:, :wm].astype(BF16)
    main, gates_t = _project(x2, norm_g[0], sh1, sc1, w_main, _pad_cols(w_in[:, wm:], LANES).astype(BF16),
                             4 * ML_HEADS, seq, tm=min(512, seq), tn=1536, name="proj_mlstm")
    y = _mlstm(main, gates_t, gate_b.reshape(4 * ML_HEADS, 1).astype(F32), head_g, bsz, seq)
    return y, 0, y, 1


def kernel(x, c, ada_w, ada_b, norm_g, mlp_w1, mlp_w2, ab_w_in, ab_conv_w, ab_conv_b, ab_dt_bias, ab_a_log,
           ab_d_skip, ab_ssd_norm, ab_rpb, ab_w_out, ml_w_in, ml_gate_b, ml_head_norm, ml_w_out):
    bsz, seq, d = x.shape
    depth = ada_w.shape[0]
    mod = _adaln(c, ada_w, ada_b)
    x2 = x.reshape(bsz * seq, d)
    w1_all, w2_all = mlp_w1.astype(BF16), mlp_w2.astype(BF16)
    w_outs = (ab_w_out.astype(BF16), ml_w_out.astype(BF16))
    for layer in range(depth):
        sh1, sc1, g1, sh2, sc2, g2 = [mod[layer, :, i * d:(i + 1) * d] for i in range(6)]
        j = layer // 2
        if layer % 2 == 0:
            mixed = _ssd_na_layer(x2, (sh1, sc1), norm_g[layer], ab_w_in[j], ab_conv_w[j], ab_conv_b[j],
                                  ab_dt_bias[j], ab_a_log[j], ab_d_skip[j], ab_ssd_norm[j], ab_rpb[j], bsz, seq)
        else:
            mixed = _mlstm_layer(x2, (sh1, sc1), norm_g[layer], ml_w_in[j], ml_gate_b[j], ml_head_norm[j],
                                 bsz, seq)
        x2 = _tail(*mixed, w_outs[layer % 2], j, x2, jnp.stack([g1, sh2, sc2, g2], axis=1), norm_g[layer, 1:4],
                   w1_all, w2_all, layer, seq, tm=min(512, seq), tf=1024, name="tail%d" % layer)
    return x2.reshape(bsz, seq, d)
```

```python
import functools

import numpy as np
import jax
import jax.numpy as jnp
from jax import lax
from jax.experimental import pallas as pl
from jax.experimental.pallas import tpu as pltpu

F32 = jnp.float32
BF16 = jnp.bfloat16

NORM_EPS = 1e-6
GRID_W = 64

SSD_HEAD_DIM = 64
SSD_HEADS = 16
SSD_GROUPS = 2
SSD_STATE = 128
SSD_CONV = 5
SSD_CHUNK = 128
SSD_BLOCK_CHUNKS = 4
SSD_WIDTH = SSD_HEADS * SSD_HEAD_DIM
SSD_BC = 2 * SSD_GROUPS * SSD_STATE
SSD_CONV_CH = SSD_WIDTH + SSD_BC

NA_HEAD_DIM = 64
NA_HEADS = 16
NA_KH = 8
NA_KW = 16
NA_WIDTH = NA_HEADS * NA_HEAD_DIM

ML_HEADS = 8
ML_DV = 256
ML_DK = 128
ML_CHUNK = 256
ML_WIDTH = ML_HEADS * ML_DV
ML_QK = ML_HEADS * ML_DK

LANES = 128
SUBLANES = 8
HALO = 16
NEG_BIG = -1e30
VMEM_LIMIT = 56 * 1024 * 1024


def _cparams(sem):
    return pltpu.CompilerParams(dimension_semantics=sem, vmem_limit_bytes=VMEM_LIMIT)


def _dot(a, b):
    return jnp.dot(a, b, preferred_element_type=F32)


def _dot_nt(a, b):
    return lax.dot_general(a, b, (((1,), (1,)), ((), ())), preferred_element_type=F32)


def _split3(x):
    hi = x.astype(BF16)
    r1 = x - hi.astype(F32)
    mid = r1.astype(BF16)
    lo = (r1 - mid.astype(F32)).astype(BF16)
    return hi, mid, lo


def _dot_exact_lhs01(sel, x):
    hi, mid, lo = _split3(x)
    return _dot(sel, hi) + _dot(sel, mid) + _dot(sel, lo)


def _iota(shape, dim):
    return lax.broadcasted_iota(jnp.int32, shape, dim)


def _sigmoid(x):
    return 0.5 * (jnp.tanh(0.5 * x) + 1.0)


def _softplus(x):
    return jnp.maximum(x, 0.0) + jnp.log(1.0 + jnp.exp(-jnp.abs(x)))


def _log_sigmoid(x):
    return jnp.minimum(x, 0.0) - jnp.log(1.0 + jnp.exp(-jnp.abs(x)))


def _rms(x):
    return x * lax.rsqrt(jnp.mean(x * x, axis=-1, keepdims=True) + NORM_EPS)


def _mod_kernel(c_ref, w_ref, b_ref, o_ref):
    c = c_ref[...]
    cond = c * _sigmoid(c)
    o_ref[0] = _dot(cond.astype(BF16), w_ref[0].astype(BF16)) + b_ref[0]


def _adaln(c, ada_w, ada_b):
    depth, d, d6 = ada_w.shape
    bsz = c.shape[0]
    tn = 1024
    return pl.pallas_call(
        _mod_kernel,
        grid=(depth, d6 // tn),
        in_specs=[pl.BlockSpec((bsz, d), lambda l, j: (0, 0)),
                  pl.BlockSpec((1, d, tn), lambda l, j: (l, 0, j)),
                  pl.BlockSpec((1, 1, tn), lambda l, j: (l, 0, j))],
        out_specs=pl.BlockSpec((1, bsz, tn), lambda l, j: (l, 0, j)),
        out_shape=jax.ShapeDtypeStruct((depth, bsz, d6), F32),
        compiler_params=_cparams(("arbitrary", "arbitrary")),
        name="adaln",
    )(c, ada_w, ada_b.reshape(depth, 1, d6))


def _proj_kernel(x_ref, g_ref, sh_ref, sc_ref, w_ref, ws_ref, o_ref, os_ref, *, small_transposed, tn):
    hb = (_rms(x_ref[...]) * g_ref[...] * (1.0 + sc_ref[0]) + sh_ref[0]).astype(BF16)
    small = _dot(hb, ws_ref[...])
    if small_transposed:
        os_ref[...] = small.T[0:os_ref.shape[0], :]
    else:
        os_ref[...] = small
    for j in range(w_ref.shape[1] // tn):
        o_ref[:, j * tn:(j + 1) * tn] = _dot(hb, w_ref[:, j * tn:(j + 1) * tn]).astype(o_ref.dtype)


def _project(x2, g, shift, scale, w, ws, small_rows, seq, tm, tn, name):
    n, d = x2.shape
    wn = w.shape[1]
    tiles_per_batch = seq // tm
    bsz = shift.shape[0]
    if small_rows:
        small_spec, small_shape = pl.BlockSpec((small_rows, tm), lambda i: (0, i)), (small_rows, n)
    else:
        small_spec, small_shape = pl.BlockSpec((tm, LANES), lambda i: (i, 0)), (n, LANES)
    resident = pl.Buffered(1)
    return pl.pallas_call(
        functools.partial(_proj_kernel, small_transposed=bool(small_rows), tn=tn),
        grid=(n // tm,),
        in_specs=[pl.BlockSpec((tm, d), lambda i: (i, 0)),
                  pl.BlockSpec((1, d), lambda i: (0, 0)),
                  pl.BlockSpec((1, 1, d), lambda i: (i // tiles_per_batch, 0, 0)),
                  pl.BlockSpec((1, 1, d), lambda i: (i // tiles_per_batch, 0, 0)),
                  pl.BlockSpec((d, wn), lambda i: (0, 0), pipeline_mode=resident),
                  pl.BlockSpec(ws.shape, lambda i: (0, 0), pipeline_mode=resident)],
        out_specs=[pl.BlockSpec((tm, wn), lambda i: (i, 0)), small_spec],
        out_shape=[jax.ShapeDtypeStruct((n, wn), BF16), jax.ShapeDtypeStruct(small_shape, F32)],
        compiler_params=_cparams(("arbitrary",)),
        name=name,
    )(x2, g.reshape(1, d), shift.reshape(bsz, 1, d), scale.reshape(bsz, 1, d), w, ws)


def _tail_kernel(ya_ref, yb_ref, wa_ref, wb_ref, x_ref, mod_ref, ng_ref, w1_ref, w2_ref, o_ref, *, tf):
    mixed = _dot(ya_ref[...], wa_ref[...]) + _dot(yb_ref[...], wb_ref[...])
    x = x_ref[...] + mod_ref[0, 0:1, :] * (_rms(mixed) * ng_ref[0:1, :])
    h = (_rms(x) * ng_ref[1:2, :] * (1.0 + mod_ref[0, 2:3, :]) + mod_ref[0, 1:2, :]).astype(BF16)
    u = None
    for f in range(w1_ref.shape[1] // tf):
        a = jnp.maximum(_dot(h, w1_ref[:, f * tf:(f + 1) * tf]), 0.0)
        part = _dot((a * a).astype(BF16), w2_ref[f * tf:(f + 1) * tf, :])
        u = part if u is None else u + part
    o_ref[...] = x + mod_ref[0, 3:4, :] * (_rms(u) * ng_ref[2:3, :])


def _tail(ya, ia, yb, ib, w_out, jo, x2, mods, gains, w1, w2, layer, seq, tm, tf, name):
    n, d = x2.shape
    kh = w_out.shape[1] // 2
    dff = w1.shape[2]
    tiles_per_batch = seq // tm
    resident = pl.Buffered(1)
    return pl.pallas_call(
        functools.partial(_tail_kernel, tf=tf),
        grid=(n // tm,),
        in_specs=[pl.BlockSpec((tm, kh), lambda i: (i, ia)),
                  pl.BlockSpec((tm, kh), lambda i: (i, ib)),
                  pl.BlockSpec((None, kh, d), lambda i: (jo, 0, 0), pipeline_mode=resident),
                  pl.BlockSpec((None, kh, d), lambda i: (jo, 1, 0), pipeline_mode=resident),
                  pl.BlockSpec((tm, d), lambda i: (i, 0)),
                  pl.BlockSpec((1, 4, d), lambda i: (i // tiles_per_batch, 0, 0)),
                  pl.BlockSpec((3, d), lambda i: (0, 0)),
                  pl.BlockSpec((None, d, dff), lambda i: (layer, 0, 0), pipeline_mode=resident),
                  pl.BlockSpec((None, dff, d), lambda i: (layer, 0, 0), pipeline_mode=resident)],
        out_specs=pl.BlockSpec((tm, d), lambda i: (i, 0)),
        out_shape=jax.ShapeDtypeStruct((n, d), F32),
        compiler_params=_cparams(("arbitrary",)),
        name=name,
    )(ya, yb, w_out, w_out, x2, mods, gains, w1, w2)


def _conv_kernel(main_ref, prev_ref, next_ref, w_ref, b_ref, o_ref, scr, *, blocks_per_seq):
    t = main_ref.shape[0]
    i = pl.program_id(0) % blocks_per_seq
    keep_prev = (i > 0).astype(F32)
    keep_next = (i < blocks_per_seq - 1).astype(F32)
    scr[0:HALO, :] = prev_ref[...].astype(F32) * keep_prev
    scr[HALO:HALO + t, :] = main_ref[...].astype(F32)
    scr[HALO + t:2 * HALO + t, :] = next_ref[...].astype(F32) * keep_next
    pad = SSD_CONV // 2
    acc = b_ref[...] + w_ref[0:1, :] * scr[HALO - pad:HALO - pad + t, :]
    for k in range(1, SSD_CONV):
        acc = acc + w_ref[k:k + 1, :] * scr[HALO - pad + k:HALO - pad + k + t, :]
    o_ref[...] = (acc * _sigmoid(acc)).astype(o_ref.dtype)


def _conv_silu(main, col0, conv_w, conv_b, seq, tr, tc):
    n = main.shape[0]
    ch = conv_w.shape[1]
    cb0 = col0 // tc
    rpb = tr // HALO
    last_halo = n // HALO - 1
    return pl.pallas_call(
        functools.partial(_conv_kernel, blocks_per_seq=seq // tr),
        grid=(n // tr, ch // tc),
        in_specs=[pl.BlockSpec((tr, tc), lambda i, j: (i, cb0 + j)),
                  pl.BlockSpec((HALO, tc), lambda i, j: (jnp.maximum(i * rpb - 1, 0), cb0 + j)),
                  pl.BlockSpec((HALO, tc), lambda i, j: (jnp.minimum((i + 1) * rpb, last_halo), cb0 + j)),
                  pl.BlockSpec((SSD_CONV, tc), lambda i, j: (0, j)),
                  pl.BlockSpec((1, tc), lambda i, j: (0, j))],
        out_specs=pl.BlockSpec((tr, tc), lambda i, j: (i, j)),
        out_shape=jax.ShapeDtypeStruct((n, ch), BF16),
        scratch_shapes=[pltpu.VMEM((tr + 2 * HALO, tc), F32)],
        compiler_params=_cparams(("arbitrary", "arbitrary")),
        name="conv_silu",
    )(main, main, main, conv_w, conv_b.reshape(1, ch))


def _ssd_kernel(xs_ref, bc_ref, dt_ref, dtb_ref, alog_ref, *rest, reverse):
    t = SSD_CHUNK

    @pl.when(pl.program_id(1) == 0)
    def _():
        rest[-1][...] = jnp.zeros_like(rest[-1])

    subs = range(xs_ref.shape[0] // t)
    for sub in (reversed(subs) if reverse else subs):
        rows = pl.ds(sub * t, t)
        per_row = [r.at[rows] for r in rest[:2]] + list(rest[2:4]) if reverse else []
        _ssd_chunk(xs_ref.at[rows], bc_ref.at[rows], dt_ref.at[rows], dtb_ref, alog_ref,
                   *per_row, rest[-2].at[rows], rest[-1], reverse=reverse)


def _ssd_chunk(xs_ref, bc_ref, dt_ref, dtb_ref, alog_ref, *rest, reverse):
    if reverse:
        z_ref, yf_ref, dskip_ref, ng_ref, o_ref, st_scr = rest
    else:
        o_ref, st_scr = rest
    t = SSD_CHUNK
    hd = SSD_HEAD_DIM
    hpg = SSD_HEADS // SSD_GROUPS

    xs = xs_ref[...].astype(F32)
    bc = bc_ref[...]

    dt = _softplus(dt_ref[...] + dtb_ref[...])
    adt = dt * (-jnp.exp(alog_ref[...]))
    row = _iota((t, t), 0)
    col = _iota((t, t), 1)
    tri = (col >= row) if reverse else (col <= row)
    cs = _dot_exact_lhs01(tri.astype(BF16), adt)
    cs_t = cs.T
    dt_t = dt.T
    base = SSD_HEADS if reverse else 0
    last = 0 if reverse else t - 1

    first_half = _iota((t, LANES), 1) < hd
    ys = []
    for g in range(SSD_GROUPS):
        bg = bc[:, g * SSD_STATE:(g + 1) * SSD_STATE]
        cg = bc[:, (SSD_GROUPS + g) * SSD_STATE:(SSD_GROUPS + g + 1) * SSD_STATE]
        cb = _dot_nt(cg, bg)
        bg_t = bg.astype(F32).T
        s_prev = st_scr[g]
        y_off = _dot(cg, s_prev.astype(BF16))
        s_new = []
        etot = []
        for pr in range(hpg // 2):
            pair = g * (hpg // 2) + pr
            xpair = xs[:, pair * LANES:(pair + 1) * LANES]
            cs_cols = []
            acc = None
            for sub in range(2):
                j = base + 2 * pair + sub
                cs_col = jnp.broadcast_to(cs[:, j:j + 1], (t, t))
                cs_cols.append(cs_col)
                dts = dt_t[j:j + 1, :]
                lmat = jnp.exp(jnp.where(tri, cs_col - cs_t[j:j + 1, :], NEG_BIG))
                m = (cb * lmat * dts).astype(BF16)
                keep = first_half if sub == 0 else jnp.logical_not(first_half)
                part = _dot(m, jnp.where(keep, xpair, 0.0).astype(BF16))
                acc = part if acc is None else acc + part
            cs_pair = jnp.where(first_half, cs_cols[0], cs_cols[1])
            tot_pair = cs_pair[last:last + 1, :]
            xd = (xpair * jnp.exp(tot_pair - cs_pair)).astype(BF16)
            sp = None
            for sub in range(2):
                j = base + 2 * pair + sub
                keep = first_half if sub == 0 else jnp.logical_not(first_half)
                part = _dot((bg_t * dt_t[j:j + 1, :]).astype(BF16), jnp.where(keep, xd, jnp.zeros((), BF16)))
                sp = part if sp is None else sp + part
            ys.append(acc + y_off[:, pr * LANES:(pr + 1) * LANES] * jnp.exp(cs_pair))
            s_new.append(sp)
            etot.append(jnp.exp(tot_pair))
        st_scr[g] = s_prev * jnp.concatenate(etot, axis=1) + jnp.concatenate(s_new, axis=1)
    y = jnp.concatenate(ys, axis=1)

    if reverse:
        y = y + yf_ref[...] + dskip_ref[...] * xs
        z = z_ref[...].astype(F32)
        y = y * (z * _sigmoid(z))
        o_ref[...] = (_rms(y) * ng_ref[...]).astype(o_ref.dtype)
    else:
        o_ref[...] = y


def _ssd_pass(conv, dtf, dtb, alog, bsz, seq, reverse, extra=None):
    n = conv.shape[0]
    t = SSD_BLOCK_CHUNKS * SSD_CHUNK
    nc = seq // t
    bc_blk = SSD_WIDTH // SSD_BC

    def rows(b, c):
        return b * nc + (nc - 1 - c if reverse else c)

    in_specs = [pl.BlockSpec((t, SSD_WIDTH), lambda b, c: (rows(b, c), 0)),
                pl.BlockSpec((t, SSD_BC), lambda b, c: (rows(b, c), bc_blk)),
                pl.BlockSpec((t, LANES), lambda b, c: (rows(b, c), 0)),
                pl.BlockSpec((1, LANES), lambda b, c: (0, 0)),
                pl.BlockSpec((1, LANES), lambda b, c: (0, 0))]
    args = [conv, conv, dtf, dtb, alog]
    if reverse:
        main, yf, dskip, ng = extra
        in_specs += [pl.BlockSpec((t, SSD_WIDTH), lambda b, c: (rows(b, c), 0)),
                     pl.BlockSpec((t, SSD_WIDTH), lambda b, c: (rows(b, c), 0)),
                     pl.BlockSpec((1, SSD_WIDTH), lambda b, c: (0, 0)),
                     pl.BlockSpec((1, SSD_WIDTH), lambda b, c: (0, 0))]
        args += [main, yf, dskip, ng]
        out_dtype = BF16
    else:
        out_dtype = F32
    return pl.pallas_call(
        functools.partial(_ssd_kernel, reverse=reverse),
        grid=(bsz, nc),
        in_specs=in_specs,
        out_specs=pl.BlockSpec((t, SSD_WIDTH), lambda b, c: (rows(b, c), 0)),
        out_shape=jax.ShapeDtypeStruct((n, SSD_WIDTH), out_dtype),
        scratch_shapes=[pltpu.VMEM((SSD_GROUPS, SSD_STATE, SSD_WIDTH // SSD_GROUPS), F32)],
        compiler_params=_cparams(("arbitrary", "arbitrary")),
        name="ssd_bwd" if reverse else "ssd_fwd",
    )(*args)


def _na_kernel(q_ref, k_ref, v_ref, bias_ref, o_ref, s_scr, p_scr, l_scr, *, nrows):
    w = GRID_W
    band = NA_KH * w
    first = _iota((w, LANES), 1) < NA_HEAD_DIM
    zero = jnp.zeros((), BF16)

    def band_start(r):
        return jnp.clip(r - NA_KH // 2, 0, nrows - NA_KH)

    def scores(r, slot):
        rs = band_start(r)
        q = q_ref[pl.ds(pl.multiple_of(r * w, w), w), :] * jnp.asarray(NA_HEAD_DIM ** -0.5, BF16)
        qs = jnp.concatenate([jnp.where(first, q, zero), jnp.where(first, zero, q)], axis=0)
        kb = k_ref[pl.ds(pl.multiple_of(rs * w, w), band), :]
        j0 = NA_KH - 1 - (r - rs)
        bias = jnp.concatenate([bias_ref[0, j0 + 2 * a] for a in range(NA_KH // 2)], axis=1)
        s_scr[slot] = _dot_nt(qs, kb) + bias

    def softmax(slot):
        s = s_scr[slot]
        p = jnp.exp(s - jnp.max(s, axis=-1, keepdims=True))
        l_scr[slot] = jnp.broadcast_to(jnp.sum(p, axis=-1, keepdims=True), (2 * w, LANES))
        p_scr[slot] = p.astype(BF16)

    def values(r, slot):
        rs = band_start(r)
        vb = v_ref[pl.ds(pl.multiple_of(rs * w, w), band), :]
        o = _dot(p_scr[slot], vb) / l_scr[slot]
        o_ref[pl.ds(pl.multiple_of(r * w, w), w), :] = jnp.where(first, o[0:w], o[w:2 * w]).astype(o_ref.dtype)

    s_scr[...] = jnp.zeros_like(s_scr)
    p_scr[...] = jnp.zeros_like(p_scr)
    l_scr[...] = jnp.ones_like(l_scr)

    def body(i4, carry):
        for sub in range(4):
            i = 4 * i4 + sub
            slot = sub % 2
            values(jnp.clip(i - 2, 0, nrows - 1), slot)
            softmax(1 - slot)
            scores(jnp.minimum(i, nrows - 1), slot)
        return carry

    lax.fori_loop(0, (nrows + 2 + 3) // 4, body, 0)


def _na_bias(rpb):
    w = GRID_W
    h, nro, nrel = rpb.shape
    c = np.arange(w)
    kc = np.arange(w)
    col_start = np.clip(c - NA_KW // 2, 0, w - NA_KW)
    valid = (kc[None, :] >= col_start[:, None]) & (kc[None, :] < col_start[:, None] + NA_KW)
    co = kc[None, :] - c[:, None] + NA_KW - 1
    onehot = ((np.arange(nrel)[:, None, None] == co[None]) & valid[None]).astype(np.float32)
    sel = np.zeros((2, nrel, w, 2, w), np.float32)
    sel[0, :, :, 0, :] = onehot
    sel[1, :, :, 1, :] = onehot
    mask = np.broadcast_to(np.where(valid, 0.0, NEG_BIG)[:, None, :], (w, 2, w)).astype(np.float32)
    table = np.concatenate([sel.reshape(2 * nrel, 2 * w * w), mask.reshape(1, 2 * w * w)], axis=0)
    r = rpb.astype(F32).reshape(h // 2, 2, nro, nrel)
    rows = jnp.stack([r[:, :, :nro - 1], r[:, :, 1:]], axis=3).transpose(0, 2, 1, 3, 4)
    rows = rows.reshape(h * (nro - 1), 2 * nrel)
    rows = jnp.concatenate([rows, jnp.ones((rows.shape[0], 1), F32)], axis=1)
    tiles = jnp.dot(rows, table, precision=lax.Precision.HIGHEST)
    return tiles.reshape(h // 2, nro - 1, 2 * w, 2 * w)


def _na(main, bias, bsz, seq, q_blk):
    n = main.shape[0]
    nrows = seq // GRID_W
    assert nrows >= NA_KH
    pairs = NA_HEADS // 2
    return pl.pallas_call(
        functools.partial(_na_kernel, nrows=nrows),
        grid=(bsz, pairs),
        in_specs=[pl.BlockSpec((seq, LANES), lambda b, p: (b, q_blk + p)),
                  pl.BlockSpec((seq, LANES), lambda b, p: (b, q_blk + pairs + p)),
                  pl.BlockSpec((seq, LANES), lambda b, p: (b, q_blk + 2 * pairs + p)),
                  pl.BlockSpec((1,) + bias.shape[1:], lambda b, p: (p, 0, 0, 0))],
        out_specs=pl.BlockSpec((seq, LANES), lambda b, p: (b, p)),
        out_shape=jax.ShapeDtypeStruct((n, NA_WIDTH), BF16),
        scratch_shapes=[pltpu.VMEM((2, 2 * GRID_W, NA_KH * GRID_W), F32),
                        pltpu.VMEM((2, 2 * GRID_W, NA_KH * GRID_W), BF16),
                        pltpu.VMEM((2, 2 * GRID_W, LANES), F32)],
        compiler_params=_cparams(("arbitrary", "arbitrary")),
        name="nbr_attn",
    )(main, main, main, bias)


ML_TAB = 16


def _mlstm_kernel(q_ref, k_ref, v_ref, og_ref, gt_ref, gbt_ref, hg_ref, o_ref,
                  rall_scr, ball_scr, a_scr, w_scr, sc_scr, cp_scr, c_st, n_st, p_scr, psum_scr, *, nchunks):
    t = ML_CHUNK
    nc = nchunks
    head = pl.program_id(1)
    scale = ML_DK ** -0.5
    row = _iota((t, t), 0)
    col = _iota((t, t), 1)
    tris = (col <= row, col >= row)

    @pl.when((pl.program_id(0) == 0) & (head == 0))
    def _():
        rall_scr[...] = jnp.zeros_like(rall_scr)
        ball_scr[...] = jnp.zeros_like(ball_scr)
        a_scr[...] = jnp.zeros_like(a_scr)
        for d in range(2):
            rall_scr[ML_TAB * d + 9:ML_TAB * d + 12, :] = jnp.ones((3, nc * t), F32)
            ball_scr[d, ML_TAB * d:ML_TAB * d + 3, :] = jnp.full((3, nc * t), -1.0, F32)

    lane = _iota((nc, t), 1)
    subc = _iota((nc, LANES), 0)
    sub32 = _iota((4 * ML_HEADS, 1), 0)

    def head_rows(kind):
        idx = kind * ML_HEADS + head
        bias = jnp.sum(jnp.where(sub32 == idx, gbt_ref[...], 0.0), axis=0, keepdims=True)
        return gt_ref[0, idx] + bias

    def scan_lanes(x, reverse, op, fill):
        sh = 1
        while sh < t:
            if reverse:
                x = op(x, jnp.where(lane < t - sh, pltpu.roll(x, t - sh, axis=1), fill))
            else:
                x = op(x, jnp.where(lane >= sh, pltpu.roll(x, sh, axis=1), fill))
            sh *= 2
        return x

    for d in range(2):
        ig = head_rows(2 * d)
        b = scan_lanes(_log_sigmoid(head_rows(2 * d + 1)), bool(d), jnp.add, 0.0)
        g_col = b[:, 0:1] if d else b[:, t - 1:t]
        a = g_col - b + ig
        m_col = jnp.max(a, axis=-1, keepdims=True)
        w = jnp.exp(a - m_col)
        rs = ig - b
        run_max = scan_lanes(rs, bool(d), jnp.maximum, NEG_BIG)
        g128 = jnp.broadcast_to(g_col, (nc, LANES))
        l128 = jnp.broadcast_to(m_col, (nc, LANES))
        m = jnp.zeros((1, LANES), F32)
        m_prev = jnp.zeros((nc, LANES), F32)
        s_old = jnp.zeros((nc, LANES), F32)
        s_new = jnp.zeros((nc, LANES), F32)
        for step in range(nc):
            cc = nc - 1 - step if d else step
            g_c = g128[cc:cc + 1, :]
            l_c = l128[cc:cc + 1, :]
            m_next = jnp.maximum(g_c + m, l_c)
            pick = subc == cc
            m_prev = jnp.where(pick, m, m_prev)
            s_old = jnp.where(pick, jnp.exp(g_c + m - m_next), s_old)
            s_new = jnp.where(pick, jnp.exp(l_c - m_next), s_new)
            m = m_next
        mp = m_prev[:, 0:1]
        u = jnp.maximum(mp, run_max)
        inter = jnp.exp(mp - u) * scale
        floor = jnp.exp(-(b + u))
        base = ML_TAB * d
        a_rows = [p.astype(F32) for x in (u, inter, floor) for p in _split3(x)]
        b_rows = [p.astype(F32) for p in _split3(rs)]
        for cc in range(nc):
            ls = slice(cc * t, (cc + 1) * t)
            for j, x in enumerate(a_rows):
                rall_scr[base + j:base + j + 1, ls] = x[cc:cc + 1, :]
            for j, x in enumerate(b_rows):
                ball_scr[d, base + 9 + j:base + 10 + j, ls] = x[cc:cc + 1, :]
            w_scr[d, :, ls] = jnp.broadcast_to(w[cc:cc + 1, :], (SUBLANES, t))
            sc_scr[d, cc, 0] = jnp.broadcast_to(s_old[cc:cc + 1, :], (SUBLANES, LANES))
            sc_scr[d, cc, 1] = jnp.broadcast_to(s_new[cc:cc + 1, :], (SUBLANES, LANES))

    def build_cols(c, carry):
        r0 = pl.multiple_of(c * t, t)
        cols = rall_scr[:, pl.ds(r0, t)].T
        a_scr[pl.ds(r0, t), 0:2 * ML_TAB] = cols.astype(BF16)
        return carry

    lax.fori_loop(0, nc, build_cols, 0)

    c_st[...] = jnp.zeros_like(c_st)
    n_st[...] = jnp.zeros_like(n_st)

    def scan_step(i, carry):
        for d in range(2):
            c = nc - 1 - i if d else i
            r0 = pl.multiple_of(c * t, t)
            kw_t = k_ref[pl.ds(r0, t), :].astype(F32).T * w_scr[d, 0:1, pl.ds(r0, t)]
            s_loc = _dot(kw_t.astype(BF16), v_ref[pl.ds(r0, t), :])
            n_loc = jnp.sum(kw_t, axis=-1, keepdims=True)
            c_prev = c_st[d]
            n_prev = n_st[d]
            cp_scr[d, c] = jnp.concatenate([c_prev, n_prev], axis=1).astype(BF16)
            s_old = sc_scr[d, c, 0][0:1, :]
            s_new = sc_scr[d, c, 1][0:1, :]
            c_st[d] = (jnp.concatenate([s_old, s_old], axis=1) * c_prev
                       + jnp.concatenate([s_new, s_new], axis=1) * s_loc)
            n_st[d] = s_old * n_prev + s_new * n_loc
        return carry

    lax.fori_loop(0, nc, scan_step, 0, unroll=2)

    kk = _iota((LANES, 2 * LANES), 0)
    ll = _iota((LANES, 2 * LANES), 1)
    pad_rows = jnp.zeros((LANES - 2 * ML_TAB, t), BF16)

    def scores(c, slot):
        r0 = pl.multiple_of(c * t, t)
        qk = _dot_nt(q_ref[pl.ds(r0, t), :], k_ref[pl.ds(r0, t), :]) * scale
        cols = a_scr[pl.ds(r0, t), :]
        for d in range(2):
            b_arg = jnp.concatenate([ball_scr[d, :, pl.ds(r0, t)].astype(BF16), pad_rows], axis=0)
            arg = _dot(cols, b_arg)
            sc = qk * jnp.exp(jnp.where(tris[d], arg, NEG_BIG))
            psum_scr[slot, d] = jnp.broadcast_to(jnp.sum(sc, axis=-1, keepdims=True), (t, LANES))
            p_scr[slot, d] = sc.astype(BF16)

    def values(c, slot):
        r0 = pl.multiple_of(c * t, t)
        q = q_ref[pl.ds(r0, t), :]
        v = v_ref[pl.ds(r0, t), :]
        cols = a_scr[pl.ds(r0, t), :]
        hsum = None
        for d in range(2):
            base = ML_TAB * d
            pick = (((ll < LANES) & (kk >= base + 3) & (kk < base + 6))
                    | ((ll >= LANES) & (kk >= base + 6) & (kk < base + 9))).astype(BF16)
            rep = _dot(cols, pick)
            inter = rep[:, 0:LANES]
            ext = _dot(q, cp_scr[d, c])
            den = psum_scr[slot, d] + inter * ext[:, ML_DV:ML_DV + LANES]
            r = 1.0 / jnp.maximum(jnp.abs(den), rep[:, LANES:2 * LANES])
            num = _dot(p_scr[slot, d], v) + jnp.concatenate([inter, inter], axis=1) * ext[:, 0:ML_DV]
            hd = num * jnp.concatenate([r, r], axis=1)
            hsum = hd if hsum is None else hsum + hd
        og = og_ref[pl.ds(r0, t), :].astype(F32)
        o_ref[pl.ds(r0, t), :] = (_sigmoid(og) * (_rms(hsum) * hg_ref[0])).astype(o_ref.dtype)

    p_scr[...] = jnp.zeros_like(p_scr)
    psum_scr[...] = jnp.zeros_like(psum_scr)

    def out_step(i2, carry):
        for slot in range(2):
            i = 2 * i2 + slot
            values(jnp.clip(i - 1, 0, nc - 1), 1 - slot)
            scores(jnp.minimum(i, nc - 1), slot)
        return carry

    lax.fori_loop(0, nc // 2 + 1, out_step, 0)


def _mlstm(main, gates_t, gate_bt, head_g, bsz, seq):
    n = main.shape[0]
    t = ML_CHUNK
    nc = seq // t
    nh = ML_HEADS
    r = gates_t.shape[0]
    gates_c = gates_t.reshape(r, bsz, nc, t).transpose(1, 0, 2, 3)
    return pl.pallas_call(
        functools.partial(_mlstm_kernel, nchunks=nc),
        grid=(bsz, nh),
        in_specs=[pl.BlockSpec((seq, ML_DK), lambda b, h: (b, h)),
                  pl.BlockSpec((seq, ML_DK), lambda b, h: (b, nh + h)),
                  pl.BlockSpec((seq, ML_DV), lambda b, h: (b, nh + h)),
                  pl.BlockSpec((seq, ML_DV), lambda b, h: (b, 2 * nh + h)),
                  pl.BlockSpec((1, r, nc, t), lambda b, h: (b, 0, 0, 0)),
                  pl.BlockSpec((r, 1), lambda b, h: (0, 0)),
                  pl.BlockSpec((1, 1, ML_DV), lambda b, h: (h, 0, 0))],
        out_specs=pl.BlockSpec((seq, ML_DV), lambda b, h: (b, h)),
        out_shape=jax.ShapeDtypeStruct((n, ML_WIDTH), BF16),
        scratch_shapes=[pltpu.VMEM((2 * ML_TAB, seq), F32),
                        pltpu.VMEM((2, 2 * ML_TAB, seq), F32),
                        pltpu.VMEM((seq, LANES), BF16),
                        pltpu.VMEM((2, SUBLANES, seq), F32),
                        pltpu.VMEM((2, nc, 2, SUBLANES, LANES), F32),
                        pltpu.VMEM((2, nc, ML_DK, ML_DV + LANES), BF16),
                        pltpu.VMEM((2, ML_DK, ML_DV), F32),
                        pltpu.VMEM((2, ML_DK, LANES), F32),
                        pltpu.VMEM((2, 2, t, t), BF16),
                        pltpu.VMEM((2, 2, t, LANES), F32)],
        compiler_params=_cparams(("arbitrary", "arbitrary")),
        name="mlstm",
    )(main, main, main, main, gates_c, gate_bt, head_g.reshape(nh, 1, ML_DV))


def _pad_cols(a, width):
    return jnp.pad(a, ((0, 0), (0, width - a.shape[1])))


def _ssd_na_layer(x2, mods, norm_g, w_in, conv_w, conv_b, dt_bias, a_log, d_skip, ssd_norm, rpb, bsz, seq):
    sh1, sc1 = mods
    s1 = SSD_WIDTH
    s2 = s1 + SSD_CONV_CH
    s3 = s2 + 2 * SSD_HEADS
    w_main = jnp.concatenate([w_in[:, :s2], w_in[:, s3:]], axis=1).astype(BF16)
    w_dt = _pad_cols(w_in[:, s2:s3], LANES).astype(BF16)
    main, dtf = _project(x2, norm_g[0], sh1, sc1, w_main, w_dt, 0, seq, tm=min(512, seq), tn=1408,
                         name="proj_ssd_na")

    conv = _conv_silu(main, s1, conv_w, conv_b, seq, tr=min(512, seq), tc=512)
    dtb = _pad_cols(dt_bias.reshape(1, 2 * SSD_HEADS), LANES)
    alog = _pad_cols(a_log.reshape(1, 2 * SSD_HEADS), LANES)
    dskip = jnp.repeat(d_skip, SSD_HEAD_DIM)[None, :]
    y_f = _ssd_pass(conv, dtf, dtb, alog, bsz, seq, reverse=False)
    y_ssd = _ssd_pass(conv, dtf, dtb, alog, bsz, seq, reverse=True,
                      extra=(main, y_f, dskip, ssd_norm[None, :]))

    q_blk = (s1 + SSD_CONV_CH) // LANES
    y_na = _na(main, _na_bias(rpb), bsz, seq, q_blk)
    return y_ssd, 0, y_na, 0


def _mlstm_layer(x2, mods, norm_g, w_in, gate_b, head_g, bsz, seq):
    sh1, sc1 = mods
    wm = 2 * ML_QK + 2 * ML_WIDTH
    w_main = w_in[:, :wm].astype(BF16)
    main, gates_t = _project(x2, norm_g[0], sh1, sc1, w_main, _pad_cols(w_in[:, wm:], LANES).astype(BF16),
                             4 * ML_HEADS, seq, tm=min(512, seq), tn=1536, name="proj_mlstm")
    y = _mlstm(main, gates_t, gate_b.reshape(4 * ML_HEADS, 1).astype(F32), head_g, bsz, seq)
    return y, 0, y, 1


def kernel(x, c, ada_w, ada_b, norm_g, mlp_w1, mlp_w2, ab_w_in, ab_conv_w, ab_conv_b, ab_dt_bias, ab_a_log,
           ab_d_skip, ab_ssd_norm, ab_rpb, ab_w_out, ml_w_in, ml_gate_b, ml_head_norm, ml_w_out):
    bsz, seq, d = x.shape
    depth = ada_w.shape[0]
    mod = _adaln(c, ada_w, ada_b)
    x2 = x.reshape(bsz * seq, d)
    w1_all, w2_all = mlp_w1.astype(BF16), mlp_w2.astype(BF16)
    w_outs = (ab_w_out.astype(BF16), ml_w_out.astype(BF16))
    for layer in range(depth):
        sh1, sc1, g1, sh2, sc2, g2 = [mod[layer, :, i * d:(i + 1) * d] for i in range(6)]
        j = layer // 2
        if layer % 2 == 0:
            mixed = _ssd_na_layer(x2, (sh1, sc1), norm_g[layer], ab_w_in[j], ab_conv_w[j], ab_conv_b[j],
                                  ab_dt_bias[j], ab_a_log[j], ab_d_skip[j], ab_ssd_norm[j], ab_rpb[j], bsz, seq)
        else:
            mixed = _mlstm_layer(x2, (sh1, sc1), norm_g[layer], ml_w_in[j], ml_gate_b[j], ml_head_norm[j],
                                 bsz, seq)
        x2 = _tail(*mixed, w_outs[layer % 2], j, x2, jnp.stack([g1, sh2, sc2, g2], axis=1), norm_g[layer, 1:4],
                   w1_all, w2_all, layer, seq, tm=min(512, seq), tf=1024, name="tail%d" % layer)
    return x2.reshape(bsz, seq, d)
```

```python
import functools

import numpy as np
import jax
import jax.numpy as jnp
from jax import lax
from jax.experimental import pallas as pl
from jax.experimental.pallas import tpu as pltpu

F32 = jnp.float32
BF16 = jnp.bfloat16

NORM_EPS = 1e-6
GRID_W = 64

SSD_HEAD_DIM = 64
SSD_HEADS = 16
SSD_GROUPS = 2
SSD_STATE = 128
SSD_CONV = 5
SSD_CHUNK = 128
SSD_BLOCK_CHUNKS = 4
SSD_WIDTH = SSD_HEADS * SSD_HEAD_DIM
SSD_BC = 2 * SSD_GROUPS * SSD_STATE
SSD_CONV_CH = SSD_WIDTH + SSD_BC

NA_HEAD_DIM = 64
NA_HEADS = 16
NA_KH = 8
NA_KW = 16
NA_WIDTH = NA_HEADS * NA_HEAD_DIM

ML_HEADS = 8
ML_DV = 256
ML_DK = 128
ML_CHUNK = 256
ML_WIDTH = ML_HEADS * ML_DV
ML_QK = ML_HEADS * ML_DK

LANES = 128
SUBLANES = 8
HALO = 16
NEG_BIG = -1e30
VMEM_LIMIT = 56 * 1024 * 1024


def _cparams(sem):
    return pltpu.CompilerParams(dimension_semantics=sem, vmem_limit_bytes=VMEM_LIMIT)


def _dot(a, b):
    return jnp.dot(a, b, preferred_element_type=F32)


def _dot_nt(a, b):
    return lax.dot_general(a, b, (((1,), (1,)), ((), ())), preferred_element_type=F32)


def _split3(x):
    hi = x.astype(BF16)
    r1 = x - hi.astype(F32)
    mid = r1.astype(BF16)
    lo = (r1 - mid.astype(F32)).astype(BF16)
    return hi, mid, lo


def _dot_exact_lhs01(sel, x):
    hi, mid, lo = _split3(x)
    return _dot(sel, hi) + _dot(sel, mid) + _dot(sel, lo)


def _iota(shape, dim):
    return lax.broadcasted_iota(jnp.int32, shape, dim)


def _sigmoid(x):
    return 0.5 * (jnp.tanh(0.5 * x) + 1.0)


def _softplus(x):
    return jnp.maximum(x, 0.0) + jnp.log(1.0 + jnp.exp(-jnp.abs(x)))


def _log_sigmoid(x):
    return jnp.minimum(x, 0.0) - jnp.log(1.0 + jnp.exp(-jnp.abs(x)))


def _rms(x):
    return x * lax.rsqrt(jnp.mean(x * x, axis=-1, keepdims=True) + NORM_EPS)


def _mod_kernel(c_ref, w_ref, b_ref, o_ref):
    c = c_ref[...]
    cond = c * _sigmoid(c)
    o_ref[0] = _dot(cond.astype(BF16), w_ref[0].astype(BF16)) + b_ref[0]


def _adaln(c, ada_w, ada_b):
    depth, d, d6 = ada_w.shape
    bsz = c.shape[0]
    tn = 1024
    return pl.pallas_call(
        _mod_kernel,
        grid=(depth, d6 // tn),
        in_specs=[pl.BlockSpec((bsz, d), lambda l, j: (0, 0)),
                  pl.BlockSpec((1, d, tn), lambda l, j: (l, 0, j)),
                  pl.BlockSpec((1, 1, tn), lambda l, j: (l, 0, j))],
        out_specs=pl.BlockSpec((1, bsz, tn), lambda l, j: (l, 0, j)),
        out_shape=jax.ShapeDtypeStruct((depth, bsz, d6), F32),
        compiler_params=_cparams(("arbitrary", "arbitrary")),
        name="adaln",
    )(c, ada_w, ada_b.reshape(depth, 1, d6))


def _proj_kernel(x_ref, g_ref, sh_ref, sc_ref, w_ref, ws_ref, o_ref, os_ref, *, small_transposed, tn):
    hb = (_rms(x_ref[...]) * g_ref[...] * (1.0 + sc_ref[0]) + sh_ref[0]).astype(BF16)
    small = _dot(hb, ws_ref[...])
    if small_transposed:
        os_ref[...] = small.T[0:os_ref.shape[0], :]
    else:
        os_ref[...] = small
    for j in range(w_ref.shape[1] // tn):
        o_ref[:, j * tn:(j + 1) * tn] = _dot(hb, w_ref[:, j * tn:(j + 1) * tn]).astype(o_ref.dtype)


def _project(x2, g, shift, scale, w, ws, small_rows, seq, tm, tn, name):
    n, d = x2.shape
    wn = w.shape[1]
    tiles_per_batch = seq // tm
    bsz = shift.shape[0]
    if small_rows:
        small_spec, small_shape = pl.BlockSpec((small_rows, tm), lambda i: (0, i)), (small_rows, n)
    else:
        small_spec, small_shape = pl.BlockSpec((tm, LANES), lambda i: (i, 0)), (n, LANES)
    resident = pl.Buffered(1)
    return pl.pallas_call(
        functools.partial(_proj_kernel, small_transposed=bool(small_rows), tn=tn),
        grid=(n // tm,),
        in_specs=[pl.BlockSpec((tm, d), lambda i: (i, 0)),
                  pl.BlockSpec((1, d), lambda i: (0, 0)),
                  pl.BlockSpec((1, 1, d), lambda i: (i // tiles_per_batch, 0, 0)),
                  pl.BlockSpec((1, 1, d), lambda i: (i // tiles_per_batch, 0, 0)),
                  pl.BlockSpec((d, wn), lambda i: (0, 0), pipeline_mode=resident),
                  pl.BlockSpec(ws.shape, lambda i: (0, 0), pipeline_mode=resident)],
        out_specs=[pl.BlockSpec((tm, wn), lambda i: (i, 0)), small_spec],
        out_shape=[jax.ShapeDtypeStruct((n, wn), BF16), jax.ShapeDtypeStruct(small_shape, F32)],
        compiler_params=_cparams(("arbitrary",)),
        name=name,
    )(x2, g.reshape(1, d), shift.reshape(bsz, 1, d), scale.reshape(bsz, 1, d), w, ws)


def _tail_kernel(ya_ref, yb_ref, wa_ref, wb_ref, x_ref, mod_ref, ng_ref, w1_ref, w2_ref, o_ref, *, tf):
    mixed = _dot(ya_ref[...], wa_ref[...]) + _dot(yb_ref[...], wb_ref[...])
    x = x_ref[...] + mod_ref[0, 0:1, :] * (_rms(mixed) * ng_ref[0:1, :])
    h = (_rms(x) * ng_ref[1:2, :] * (1.0 + mod_ref[0, 2:3, :]) + mod_ref[0, 1:2, :]).astype(BF16)
    u = None
    for f in range(w1_ref.shape[1] // tf):
        a = jnp.maximum(_dot(h, w1_ref[:, f * tf:(f + 1) * tf]), 0.0)
        part = _dot((a * a).astype(BF16), w2_ref[f * tf:(f + 1) * tf, :])
        u = part if u is None else u + part
    o_ref[...] = x + mod_ref[0, 3:4, :] * (_rms(u) * ng_ref[2:3, :])


def _tail(ya, ia, yb, ib, w_out, jo, x2, mods, gains, w1, w2, layer, seq, tm, tf, name):
    n, d = x2.shape
    kh = w_out.shape[1] // 2
    dff = w1.shape[2]
    tiles_per_batch = seq // tm
    resident = pl.Buffered(1)
    return pl.pallas_call(
        functools.partial(_tail_kernel, tf=tf),
        grid=(n // tm,),
        in_specs=[pl.BlockSpec((tm, kh), lambda i: (i, ia)),
                  pl.BlockSpec((tm, kh), lambda i: (i, ib)),
                  pl.BlockSpec((None, kh, d), lambda i: (jo, 0, 0), pipeline_mode=resident),
                  pl.BlockSpec((None, kh, d), lambda i: (jo, 1, 0), pipeline_mode=resident),
                  pl.BlockSpec((tm, d), lambda i: (i, 0)),
                  pl.BlockSpec((1, 4, d), lambda i: (i // tiles_per_batch, 0, 0)),
                  pl.BlockSpec((3, d), lambda i: (0, 0)),
                  pl.BlockSpec((None, d, dff), lambda i: (layer, 0, 0), pipeline_mode=resident),
                  pl.BlockSpec((None, dff, d), lambda i: (layer, 0, 0), pipeline_mode=resident)],
        out_specs=pl.BlockSpec((tm, d), lambda i: (i, 0)),
        out_shape=jax.ShapeDtypeStruct((n, d), F32),
        compiler_params=_cparams(("arbitrary",)),
        name=name,
    )(ya, yb, w_out, w_out, x2, mods, gains, w1, w2)


def _conv_kernel(main_ref, prev_ref, next_ref, w_ref, b_ref, o_ref, scr, *, blocks_per_seq):
    t = main_ref.shape[0]
    i = pl.program_id(0) % blocks_per_seq
    keep_prev = (i > 0).astype(F32)
    keep_next = (i < blocks_per_seq - 1).astype(F32)
    scr[0:HALO, :] = prev_ref[...].astype(F32) * keep_prev
    scr[HALO:HALO + t, :] = main_ref[...].astype(F32)
    scr[HALO + t:2 * HALO + t, :] = next_ref[...].astype(F32) * keep_next
    pad = SSD_CONV // 2
    acc = b_ref[...] + w_ref[0:1, :] * scr[HALO - pad:HALO - pad + t, :]
    for k in range(1, SSD_CONV):
        acc = acc + w_ref[k:k + 1, :] * scr[HALO - pad + k:HALO - pad + k + t, :]
    o_ref[...] = (acc * _sigmoid(acc)).astype(o_ref.dtype)


def _conv_silu(main, col0, conv_w, conv_b, seq, tr, tc):
    n = main.shape[0]
    ch = conv_w.shape[1]
    cb0 = col0 // tc
    rpb = tr // HALO
    last_halo = n // HALO - 1
    return pl.pallas_call(
        functools.partial(_conv_kernel, blocks_per_seq=seq // tr),
        grid=(n // tr, ch // tc),
        in_specs=[pl.BlockSpec((tr, tc), lambda i, j: (i, cb0 + j)),
                  pl.BlockSpec((HALO, tc), lambda i, j: (jnp.maximum(i * rpb - 1, 0), cb0 + j)),
                  pl.BlockSpec((HALO, tc), lambda i, j: (jnp.minimum((i + 1) * rpb, last_halo), cb0 + j)),
                  pl.BlockSpec((SSD_CONV, tc), lambda i, j: (0, j)),
                  pl.BlockSpec((1, tc), lambda i, j: (0, j))],
        out_specs=pl.BlockSpec((tr, tc), lambda i, j: (i, j)),
        out_shape=jax.ShapeDtypeStruct((n, ch), BF16),
        scratch_shapes=[pltpu.VMEM((tr + 2 * HALO, tc), F32)],
        compiler_params=_cparams(("arbitrary", "arbitrary")),
        name="conv_silu",
    )(main, main, main, conv_w, conv_b.reshape(1, ch))


def _ssd_kernel(xs_ref, bc_ref, dt_ref, dtb_ref, alog_ref, *rest, reverse):
    t = SSD_CHUNK

    @pl.when(pl.program_id(1) == 0)
    def _():
        rest[-1][...] = jnp.zeros_like(rest[-1])

    subs = range(xs_ref.shape[0] // t)
    for sub in (reversed(subs) if reverse else subs):
        rows = pl.ds(sub * t, t)
        per_row = [r.at[rows] for r in rest[:2]] + list(rest[2:4]) if reverse else []
        _ssd_chunk(xs_ref.at[rows], bc_ref.at[rows], dt_ref.at[rows], dtb_ref, alog_ref,
                   *per_row, rest[-2].at[rows], rest[-1], reverse=reverse)


def _ssd_chunk(xs_ref, bc_ref, dt_ref, dtb_ref, alog_ref, *rest, reverse):
    if reverse:
        z_ref, yf_ref, dskip_ref, ng_ref, o_ref, st_scr = rest
    else:
        o_ref, st_scr = rest
    t = SSD_CHUNK
    hd = SSD_HEAD_DIM
    hpg = SSD_HEADS // SSD_GROUPS

    xs = xs_ref[...].astype(F32)
    bc = bc_ref[...]

    dt = _softplus(dt_ref[...] + dtb_ref[...])
    adt = dt * (-jnp.exp(alog_ref[...]))
    row = _iota((t, t), 0)
    col = _iota((t, t), 1)
    tri = (col >= row) if reverse else (col <= row)
    cs = _dot_exact_lhs01(tri.astype(BF16), adt)
    cs_t = cs.T
    dt_t = dt.T
    base = SSD_HEADS if reverse else 0
    last = 0 if reverse else t - 1

    first_half = _iota((t, LANES), 1) < hd
    ys = []
    for g in range(SSD_GROUPS):
        bg = bc[:, g * SSD_STATE:(g + 1) * SSD_STATE]
        cg = bc[:, (SSD_GROUPS + g) * SSD_STATE:(SSD_GROUPS + g + 1) * SSD_STATE]
        cb = _dot_nt(cg, bg)
        bg_t = bg.astype(F32).T
        s_prev = st_scr[g]
        y_off = _dot(cg, s_prev.astype(BF16))
        s_new = []
        etot = []
        for pr in range(hpg // 2):
            pair = g * (hpg // 2) + pr
            xpair = xs[:, pair * LANES:(pair + 1) * LANES]
            cs_cols = []
            acc = None
            for sub in range(2):
                j = base + 2 * pair + sub
                cs_col = jnp.broadcast_to(cs[:, j:j + 1], (t, t))
                cs_cols.append(cs_col)
                dts = dt_t[j:j + 1, :]
                lmat = jnp.exp(jnp.where(tri, cs_col - cs_t[j:j + 1, :], NEG_BIG))
                m = (cb * lmat * dts).astype(BF16)
                keep = first_half if sub == 0 else jnp.logical_not(first_half)
                part = _dot(m, jnp.where(keep, xpair, 0.0).astype(BF16))
                acc = part if acc is None else acc + part
            cs_pair = jnp.where(first_half, cs_cols[0], cs_cols[1])
            tot_pair = cs_pair[last:last + 1, :]
            xd = (xpair * jnp.exp(tot_pair - cs_pair)).astype(BF16)
            sp = None
            for sub in range(2):
                j = base + 2 * pair + sub
                keep = first_half if sub == 0 else jnp.logical_not(first_half)
                part = _dot((bg_t * dt_t[j:j + 1, :]).astype(BF16), jnp.where(keep, xd, jnp.zeros((), BF16)))
                sp = part if sp is None else sp + part
            ys.append(acc + y_off[:, pr * LANES:(pr + 1) * LANES] * jnp.exp(cs_pair))
            s_new.append(sp)
            etot.append(jnp.exp(tot_pair))
        st_scr[g] = s_prev * jnp.concatenate(etot, axis=1) + jnp.concatenate(s_new, axis=1)
    y = jnp.concatenate(ys, axis=1)

    if reverse:
        y = y + yf_ref[...] + dskip_ref[...] * xs
        z = z_ref[...].astype(F32)
        y = y * (z * _sigmoid(z))
        o_ref[...] = (_rms(y) * ng_ref[...]).astype(o_ref.dtype)
    else:
        o_ref[...] = y


def _ssd_pass(conv, dtf, dtb, alog, bsz, seq, reverse, extra=None):
    n = conv.shape[0]
    t = SSD_BLOCK_CHUNKS * SSD_CHUNK
    nc = seq // t
    bc_blk = SSD_WIDTH // SSD_BC

    def rows(b, c):
        return b * nc + (nc - 1 - c if reverse else c)

    in_specs = [pl.BlockSpec((t, SSD_WIDTH), lambda b, c: (rows(b, c), 0)),
                pl.BlockSpec((t, SSD_BC), lambda b, c: (rows(b, c), bc_blk)),
                pl.BlockSpec((t, LANES), lambda b, c: (rows(b, c), 0)),
                pl.BlockSpec((1, LANES), lambda b, c: (0, 0)),
                pl.BlockSpec((1, LANES), lambda b, c: (0, 0))]
    args = [conv, conv, dtf, dtb, alog]
    if reverse:
        main, yf, dskip, ng = extra
        in_specs += [pl.BlockSpec((t, SSD_WIDTH), lambda b, c: (rows(b, c), 0)),
                     pl.BlockSpec((t, SSD_WIDTH), lambda b, c: (rows(b, c), 0)),
                     pl.BlockSpec((1, SSD_WIDTH), lambda b, c: (0, 0)),
                     pl.BlockSpec((1, SSD_WIDTH), lambda b, c: (0, 0))]
        args += [main, yf, dskip, ng]
        out_dtype = BF16
    else:
        out_dtype = F32
    return pl.pallas_call(
        functools.partial(_ssd_kernel, reverse=reverse),
        grid=(bsz, nc),
        in_specs=in_specs,
        out_specs=pl.BlockSpec((t, SSD_WIDTH), lambda b, c: (rows(b, c), 0)),
        out_shape=jax.ShapeDtypeStruct((n, SSD_WIDTH), out_dtype),
        scratch_shapes=[pltpu.VMEM((SSD_GROUPS, SSD_STATE, SSD_WIDTH // SSD_GROUPS), F32)],
        compiler_params=_cparams(("arbitrary", "arbitrary")),
        name="ssd_bwd" if reverse else "ssd_fwd",
    )(*args)


def _na_kernel(q_ref, k_ref, v_ref, bias_ref, o_ref, s_scr, p_scr, l_scr, *, nrows):
    w = GRID_W
    band = NA_KH * w
    first = _iota((w, LANES), 1) < NA_HEAD_DIM
    zero = jnp.zeros((), BF16)

    def band_start(r):
        return jnp.clip(r - NA_KH // 2, 0, nrows - NA_KH)

    def scores(r, slot):
        rs = band_start(r)
        q = q_ref[pl.ds(pl.multiple_of(r * w, w), w), :] * jnp.asarray(NA_HEAD_DIM ** -0.5, BF16)
        qs = jnp.concatenate([jnp.where(first, q, zero), jnp.where(first, zero, q)], axis=0)
        kb = k_ref[pl.ds(pl.multiple_of(rs * w, w), band), :]
        j0 = NA_KH - 1 - (r - rs)
        bias = jnp.concatenate([bias_ref[0, j0 + 2 * a] for a in range(NA_KH // 2)], axis=1)
        s_scr[slot] = _dot_nt(qs, kb) + bias

    def softmax(slot):
        s = s_scr[slot]
        p = jnp.exp(s - jnp.max(s, axis=-1, keepdims=True))
        l_scr[slot] = jnp.broadcast_to(jnp.sum(p, axis=-1, keepdims=True), (2 * w, LANES))
        p_scr[slot] = p.astype(BF16)

    def values(r, slot):
        rs = band_start(r)
        vb = v_ref[pl.ds(pl.multiple_of(rs * w, w), band), :]
        o = _dot(p_scr[slot], vb) / l_scr[slot]
        o_ref[pl.ds(pl.multiple_of(r * w, w), w), :] = jnp.where(first, o[0:w], o[w:2 * w]).astype(o_ref.dtype)

    s_scr[...] = jnp.zeros_like(s_scr)
    p_scr[...] = jnp.zeros_like(p_scr)
    l_scr[...] = jnp.ones_like(l_scr)

    def body(i4, carry):
        for sub in range(4):
            i = 4 * i4 + sub
            slot = sub % 2
            values(jnp.clip(i - 2, 0, nrows - 1), slot)
            softmax(1 - slot)
            scores(jnp.minimum(i, nrows - 1), slot)
        return carry

    lax.fori_loop(0, (nrows + 2 + 3) // 4, body, 0)


def _na_bias(rpb):
    w = GRID_W
    h, nro, nrel = rpb.shape
    c = np.arange(w)
    kc = np.arange(w)
    col_start = np.clip(c - NA_KW // 2, 0, w - NA_KW)
    valid = (kc[None, :] >= col_start[:, None]) & (kc[None, :] < col_start[:, None] + NA_KW)
    co = kc[None, :] - c[:, None] + NA_KW - 1
    onehot = ((np.arange(nrel)[:, None, None] == co[None]) & valid[None]).astype(np.float32)
    sel = np.zeros((2, nrel, w, 2, w), np.float32)
    sel[0, :, :, 0, :] = onehot
    sel[1, :, :, 1, :] = onehot
    mask = np.broadcast_to(np.where(valid, 0.0, NEG_BIG)[:, None, :], (w, 2, w)).astype(np.float32)
    table = np.concatenate([sel.reshape(2 * nrel, 2 * w * w), mask.reshape(1, 2 * w * w)], axis=0)
    r = rpb.astype(F32).reshape(h // 2, 2, nro, nrel)
    rows = jnp.stack([r[:, :, :nro - 1], r[:, :, 1:]], axis=3).transpose(0, 2, 1, 3, 4)
    rows = rows.reshape(h * (nro - 1), 2 * nrel)
    rows = jnp.concatenate([rows, jnp.ones((rows.shape[0], 1), F32)], axis=1)
    tiles = jnp.dot(rows, table, precision=lax.Precision.HIGHEST)
    return tiles.reshape(h // 2, nro - 1, 2 * w, 2 * w)


def _na(main, bias, bsz, seq, q_blk):
    n = main.shape[0]
    nrows = seq // GRID_W
    assert nrows >= NA_KH
    pairs = NA_HEADS // 2
    return pl.pallas_call(
        functools.partial(_na_kernel, nrows=nrows),
        grid=(bsz, pairs),
        in_specs=[pl.BlockSpec((seq, LANES), lambda b, p: (b, q_blk + p)),
                  pl.BlockSpec((seq, LANES), lambda b, p: (b, q_blk + pairs + p)),
                  pl.BlockSpec((seq, LANES), lambda b, p: (b, q_blk + 2 * pairs + p)),
                  pl.BlockSpec((1,) + bias.shape[1:], lambda b, p: (p, 0, 0, 0))],
        out_specs=pl.BlockSpec((seq, LANES), lambda b, p: (b, p)),
        out_shape=jax.ShapeDtypeStruct((n, NA_WIDTH), BF16),
        scratch_shapes=[pltpu.VMEM((2, 2 * GRID_W, NA_KH * GRID_W), F32),
                        pltpu.VMEM((2, 2 * GRID_W, NA_KH * GRID_W), BF16),
                        pltpu.VMEM((2, 2 * GRID_W, LANES), F32)],
        compiler_params=_cparams(("arbitrary", "arbitrary")),
        name="nbr_attn",
    )(main, main, main, bias)


ML_TAB = 16


def _mlstm_kernel(q_ref, k_ref, v_ref, og_ref, gt_ref, gbt_ref, hg_ref, o_ref,
                  rall_scr, ball_scr, a_scr, w_scr, sc_scr, cp_scr, np_scr, c_st, n_st, p_scr, psum_scr,
                  *, nchunks):
    t = ML_CHUNK
    nc = nchunks
    head = pl.program_id(1)
    scale = ML_DK ** -0.5
    row = _iota((t, t), 0)
    col = _iota((t, t), 1)
    tris = (col <= row, col >= row)

    @pl.when((pl.program_id(0) == 0) & (head == 0))
    def _():
        rall_scr[...] = jnp.zeros_like(rall_scr)
        ball_scr[...] = jnp.zeros_like(ball_scr)
        a_scr[...] = jnp.zeros_like(a_scr)
        for d in range(2):
            rall_scr[ML_TAB * d + 9:ML_TAB * d + 12, :] = jnp.ones((3, nc * t), F32)
            ball_scr[d, ML_TAB * d:ML_TAB * d + 3, :] = jnp.full((3, nc * t), -1.0, F32)

    lane = _iota((nc, t), 1)
    subc = _iota((nc, LANES), 0)
    sub32 = _iota((4 * ML_HEADS, 1), 0)

    def head_rows(kind):
        idx = kind * ML_HEADS + head
        bias = jnp.sum(jnp.where(sub32 == idx, gbt_ref[...], 0.0), axis=0, keepdims=True)
        return gt_ref[0, idx] + bias

    def scan_lanes(x, reverse, op, fill):
        sh = 1
        while sh < t:
            if reverse:
                x = op(x, jnp.where(lane < t - sh, pltpu.roll(x, t - sh, axis=1), fill))
            else:
                x = op(x, jnp.where(lane >= sh, pltpu.roll(x, sh, axis=1), fill))
            sh *= 2
        return x

    for d in range(2):
        ig = head_rows(2 * d)
        b = scan_lanes(_log_sigmoid(head_rows(2 * d + 1)), bool(d), jnp.add, 0.0)
        g_col = b[:, 0:1] if d else b[:, t - 1:t]
        a = g_col - b + ig
        m_col = jnp.max(a, axis=-1, keepdims=True)
        w = jnp.exp(a - m_col)
        rs = ig - b
        run_max = scan_lanes(rs, bool(d), jnp.maximum, NEG_BIG)
        g128 = jnp.broadcast_to(g_col, (nc, LANES))
        l128 = jnp.broadcast_to(m_col, (nc, LANES))
        m = jnp.zeros((1, LANES), F32)
        m_prev = jnp.zeros((nc, LANES), F32)
        s_old = jnp.zeros((nc, LANES), F32)
        s_new = jnp.zeros((nc, LANES), F32)
        for step in range(nc):
            cc = nc - 1 - step if d else step
            g_c = g128[cc:cc + 1, :]
            l_c = l128[cc:cc + 1, :]
            m_next = jnp.maximum(g_c + m, l_c)
            pick = subc == cc
            m_prev = jnp.where(pick, m, m_prev)
            s_old = jnp.where(pick, jnp.exp(g_c + m - m_next), s_old)
            s_new = jnp.where(pick, jnp.exp(l_c - m_next), s_new)
            m = m_next
        mp = m_prev[:, 0:1]
        u = jnp.maximum(mp, run_max)
        inter = jnp.exp(mp - u) * scale
        floor = jnp.exp(-(b + u))
        base = ML_TAB * d
        a_rows = [p.astype(F32) for x in (u, inter, floor) for p in _split3(x)]
        b_rows = [p.astype(F32) for p in _split3(rs)]
        for cc in range(nc):
            ls = slice(cc * t, (cc + 1) * t)
            for j, x in enumerate(a_rows):
                rall_scr[base + j:base + j + 1, ls] = x[cc:cc + 1, :]
            for j, x in enumerate(b_rows):
                ball_scr[d, base + 9 + j:base + 10 + j, ls] = x[cc:cc + 1, :]
            w_scr[d, :, ls] = jnp.broadcast_to(w[cc:cc + 1, :], (SUBLANES, t))
            sc_scr[d, cc, 0] = jnp.broadcast_to(s_old[cc:cc + 1, :], (SUBLANES, LANES))
            sc_scr[d, cc, 1] = jnp.broadcast_to(s_new[cc:cc + 1, :], (SUBLANES, LANES))

    def build_cols(c, carry):
        r0 = pl.multiple_of(c * t, t)
        cols = rall_scr[:, pl.ds(r0, t)].T
        a_scr[pl.ds(r0, t), 0:2 * ML_TAB] = cols.astype(BF16)
        return carry

    lax.fori_loop(0, nc, build_cols, 0)

    c_st[...] = jnp.zeros_like(c_st)
    n_st[...] = jnp.zeros_like(n_st)

    def scan_step(i, carry):
        for d in range(2):
            c = nc - 1 - i if d else i
            r0 = pl.multiple_of(c * t, t)
            k = k_ref[pl.ds(r0, t), :]
            w8 = w_scr[d, :, pl.ds(r0, t)]
            kw_t = k.astype(F32).T * w8[0:1, :]
            s_loc = _dot(kw_t.astype(BF16), v_ref[pl.ds(r0, t), :])
            n_loc = _dot(w8.astype(BF16), k)
            c_prev = c_st[d]
            n_prev = n_st[d]
            cp_scr[d, c] = c_prev.astype(BF16)
            np_scr[d, c] = n_prev
            s_old = sc_scr[d, c, 0][0:1, :]
            s_new = sc_scr[d, c, 1][0:1, :]
            c_st[d] = (jnp.concatenate([s_old, s_old], axis=1) * c_prev
                       + jnp.concatenate([s_new, s_new], axis=1) * s_loc)
            n_st[d] = s_old * n_prev + s_new * n_loc
        return carry

    lax.fori_loop(0, nc, scan_step, 0, unroll=min(4, nc))

    kk = _iota((LANES, 2 * LANES), 0)
    ll = _iota((LANES, 2 * LANES), 1)
    pad_rows = jnp.zeros((LANES - 2 * ML_TAB, t), BF16)

    def scores(c, slot):
        r0 = pl.multiple_of(c * t, t)
        qk = _dot_nt(q_ref[pl.ds(r0, t), :], k_ref[pl.ds(r0, t), :]) * scale
        cols = a_scr[pl.ds(r0, t), :]
        for d in range(2):
            b_arg = jnp.concatenate([ball_scr[d, :, pl.ds(r0, t)].astype(BF16), pad_rows], axis=0)
            arg = _dot(cols, b_arg)
            sc = qk * jnp.exp(jnp.where(tris[d], arg, NEG_BIG))
            psum_scr[slot, d] = jnp.broadcast_to(jnp.sum(sc, axis=-1, keepdims=True), (t, LANES))
            p_scr[slot, d] = sc.astype(BF16)

    def values(c, slot):
        r0 = pl.multiple_of(c * t, t)
        q = q_ref[pl.ds(r0, t), :]
        qf = q.astype(F32)
        v = v_ref[pl.ds(r0, t), :]
        cols = a_scr[pl.ds(r0, t), :]
        hsum = None
        for d in range(2):
            base = ML_TAB * d
            pick = (((ll < LANES) & (kk >= base + 3) & (kk < base + 6))
                    | ((ll >= LANES) & (kk >= base + 6) & (kk < base + 9))).astype(BF16)
            rep = _dot(cols, pick)
            inter = rep[:, 0:LANES]
            q_n = jnp.sum(qf * np_scr[d, c][0:1, :], axis=-1, keepdims=True)
            den = psum_scr[slot, d] + inter * q_n
            r = 1.0 / jnp.maximum(jnp.abs(den), rep[:, LANES:2 * LANES])
            num = _dot(p_scr[slot, d], v) + jnp.concatenate([inter, inter], axis=1) * _dot(q, cp_scr[d, c])
            hd = num * jnp.concatenate([r, r], axis=1)
            hsum = hd if hsum is None else hsum + hd
        og = og_ref[pl.ds(r0, t), :].astype(F32)
        o_ref[pl.ds(r0, t), :] = (_sigmoid(og) * (_rms(hsum) * hg_ref[0])).astype(o_ref.dtype)

    p_scr[...] = jnp.zeros_like(p_scr)
    psum_scr[...] = jnp.zeros_like(psum_scr)

    def out_step(i2, carry):
        for slot in range(2):
            i = 2 * i2 + slot
            values(jnp.clip(i - 1, 0, nc - 1), 1 - slot)
            scores(jnp.minimum(i, nc - 1), slot)
        return carry

    lax.fori_loop(0, nc // 2 + 1, out_step, 0)


def _mlstm(main, gates_t, gate_bt, head_g, bsz, seq):
    n = main.shape[0]
    t = ML_CHUNK
    nc = seq // t
    nh = ML_HEADS
    r = gates_t.shape[0]
    gates_c = gates_t.reshape(r, bsz, nc, t).transpose(1, 0, 2, 3)
    return pl.pallas_call(
        functools.partial(_mlstm_kernel, nchunks=nc),
        grid=(bsz, nh),
        in_specs=[pl.BlockSpec((seq, ML_DK), lambda b, h: (b, h)),
                  pl.BlockSpec((seq, ML_DK), lambda b, h: (b, nh + h)),
                  pl.BlockSpec((seq, ML_DV), lambda b, h: (b, nh + h)),
                  pl.BlockSpec((seq, ML_DV), lambda b, h: (b, 2 * nh + h)),
                  pl.BlockSpec((1, r, nc, t), lambda b, h: (b, 0, 0, 0)),
                  pl.BlockSpec((r, 1), lambda b, h: (0, 0)),
                  pl.BlockSpec((1, 1, ML_DV), lambda b, h: (h, 0, 0))],
        out_specs=pl.BlockSpec((seq, ML_DV), lambda b, h: (b, h)),
        out_shape=jax.ShapeDtypeStruct((n, ML_WIDTH), BF16),
        scratch_shapes=[pltpu.VMEM((2 * ML_TAB, seq), F32),
                        pltpu.VMEM((2, 2 * ML_TAB, seq), F32),
                        pltpu.VMEM((seq, LANES), BF16),
                        pltpu.VMEM((2, SUBLANES, seq), F32),
                        pltpu.VMEM((2, nc, 2, SUBLANES, LANES), F32),
                        pltpu.VMEM((2, nc, ML_DK, ML_DV), BF16),
                        pltpu.VMEM((2, nc, SUBLANES, ML_DK), F32),
                        pltpu.VMEM((2, ML_DK, ML_DV), F32),
                        pltpu.VMEM((2, SUBLANES, ML_DK), F32),
                        pltpu.VMEM((2, 2, t, t), BF16),
                        pltpu.VMEM((2, 2, t, LANES), F32)],
        compiler_params=_cparams(("arbitrary", "arbitrary")),
        name="mlstm",
    )(main, main, main, main, gates_c, gate_bt, head_g.reshape(nh, 1, ML_DV))


def _pad_cols(a, width):
    return jnp.pad(a, ((0, 0), (0, width - a.shape[1])))


def _ssd_na_layer(x2, mods, norm_g, w_in, conv_w, conv_b, dt_bias, a_log, d_skip, ssd_norm, rpb, bsz, seq):
    sh1, sc1 = mods
    s1 = SSD_WIDTH
    s2 = s1 + SSD_CONV_CH
    s3 = s2 + 2 * SSD_HEADS
    w_main = jnp.concatenate([w_in[:, :s2], w_in[:, s3:]], axis=1).astype(BF16)
    w_dt = _pad_cols(w_in[:, s2:s3], LANES).astype(BF16)
    main, dtf = _project(x2, norm_g[0], sh1, sc1, w_main, w_dt, 0, seq, tm=min(512, seq), tn=1408,
                         name="proj_ssd_na")

    conv = _conv_silu(main, s1, conv_w, conv_b, seq, tr=min(512, seq), tc=512)
    dtb = _pad_cols(dt_bias.reshape(1, 2 * SSD_HEADS), LANES)
    alog = _pad_cols(a_log.reshape(1, 2 * SSD_HEADS), LANES)
    dskip = jnp.repeat(d_skip, SSD_HEAD_DIM)[None, :]
    y_f = _ssd_pass(conv, dtf, dtb, alog, bsz, seq, reverse=False)
    y_ssd = _ssd_pass(conv, dtf, dtb, alog, bsz, seq, reverse=True,
                      extra=(main, y_f, dskip, ssd_norm[None, :]))

    q_blk = (s1 + SSD_CONV_CH) // LANES
    y_na = _na(main, _na_bias(rpb), bsz, seq, q_blk)
    return y_ssd, 0, y_na, 0


def _mlstm_layer(x2, mods, norm_g, w_in, gate_b, head_g, bsz, seq):
    sh1, sc1 = mods
    wm = 2 * ML_QK + 2 * ML_WIDTH
    w_main = w_in[:, :wm].astype(BF16)
    main, gates_t = _project(x2, norm_g[0], sh1, sc1, w_main, _pad_cols(w_in[:, wm:], LANES).astype(BF16),
                             4 * ML_HEADS, seq, tm=min(512, seq), tn=1536, name="proj_mlstm")
    y = _mlstm(main, gates_t, gate_b.reshape(4 * ML_HEADS, 1).astype(F32), head_g, bsz, seq)
    return y, 0, y, 1


def kernel(x, c, ada_w, ada_b, norm_g, mlp_w1, mlp_w2, ab_w_in, ab_conv_w, ab_conv_b, ab_dt_bias, ab_a_log,
           ab_d_skip, ab_ssd_norm, ab_rpb, ab_w_out, ml_w_in, ml_gate_b, ml_head_norm, ml_w_out):
    bsz, seq, d = x.shape
    depth = ada_w.shape[0]
    mod = _adaln(c, ada_w, ada_b)
    x2 = x.reshape(bsz * seq, d)
    w1_all, w2_all = mlp_w1.astype(BF16), mlp_w2.astype(BF16)
    w_outs = (ab_w_out.astype(BF16), ml_w_out.astype(BF16))
    for layer in range(depth):
        sh1, sc1, g1, sh2, sc2, g2 = [mod[layer, :, i * d:(i + 1) * d] for i in range(6)]
        j = layer // 2
        if layer % 2 == 0:
            mixed = _ssd_na_layer(x2, (sh1, sc1), norm_g[layer], ab_w_in[j], ab_conv_w[j], ab_conv_b[j],
                                  ab_dt_bias[j], ab_a_log[j], ab_d_skip[j], ab_ssd_norm[j], ab_rpb[j], bsz, seq)
        else:
            mixed = _mlstm_layer(x2, (sh1, sc1), norm_g[layer], ml_w_in[j], ml_gate_b[j], ml_head_norm[j],
                                 bsz, seq)
        x2 = _tail(*mixed, w_outs[layer % 2], j, x2, jnp.stack([g1, sh2, sc2, g2], axis=1), norm_g[layer, 1:4],
                   w1_all, w2_all, layer, seq, tm=min(512, seq), tf=1024, name="tail%d" % layer)
    return x2.reshape(bsz, seq, d)
```

```python
import functools

import numpy as np
import jax
import jax.numpy as jnp
from jax import lax
from jax.experimental import pallas as pl
from jax.experimental.pallas import tpu as pltpu

F32 = jnp.float32
BF16 = jnp.bfloat16

NORM_EPS = 1e-6
GRID_W = 64

SSD_HEAD_DIM = 64
SSD_HEADS = 16
SSD_GROUPS = 2
SSD_STATE = 128
SSD_CONV = 5
SSD_CHUNK = 128
SSD_BLOCK_CHUNKS = 4
SSD_WIDTH = SSD_HEADS * SSD_HEAD_DIM
SSD_BC = 2 * SSD_GROUPS * SSD_STATE
SSD_CONV_CH = SSD_WIDTH + SSD_BC

NA_HEAD_DIM = 64
NA_HEADS = 16
NA_KH = 8
NA_KW = 16
NA_WIDTH = NA_HEADS * NA_HEAD_DIM

ML_HEADS = 8
ML_DV = 256
ML_DK = 128
ML_CHUNK = 256
ML_TAB = 16
ML_WIDTH = ML_HEADS * ML_DV
ML_QK = ML_HEADS * ML_DK

LANES = 128
SUBLANES = 8
HALO = 16
NEG_BIG = -1e30
VMEM_LIMIT = 56 * 1024 * 1024


def _cparams(sem):
    return pltpu.CompilerParams(dimension_semantics=sem, vmem_limit_bytes=VMEM_LIMIT)


def _dot(a, b):
    return jnp.dot(a, b, preferred_element_type=F32)


def _dot_nt(a, b):
    return lax.dot_general(a, b, (((1,), (1,)), ((), ())), preferred_element_type=F32)


def _split3(x):
    hi = x.astype(BF16)
    r1 = x - hi.astype(F32)
    mid = r1.astype(BF16)
    lo = (r1 - mid.astype(F32)).astype(BF16)
    return hi, mid, lo


def _dot_exact_lhs01(sel, x):
    hi, mid, lo = _split3(x)
    return _dot(sel, hi) + _dot(sel, mid) + _dot(sel, lo)


def _iota(shape, dim):
    return lax.broadcasted_iota(jnp.int32, shape, dim)


def _sigmoid(x):
    return 0.5 * (jnp.tanh(0.5 * x) + 1.0)


def _softplus(x):
    return jnp.maximum(x, 0.0) + jnp.log(1.0 + jnp.exp(-jnp.abs(x)))


def _log_sigmoid(x):
    return jnp.minimum(x, 0.0) - jnp.log(1.0 + jnp.exp(-jnp.abs(x)))


def _rms(x):
    return x * lax.rsqrt(jnp.mean(x * x, axis=-1, keepdims=True) + NORM_EPS)


def _mod_kernel(c_ref, w_ref, b_ref, o_ref):
    c = c_ref[...]
    cond = c * _sigmoid(c)
    o_ref[0] = _dot(cond.astype(BF16), w_ref[0].astype(BF16)) + b_ref[0]


def _adaln(c, ada_w, ada_b):
    depth, d, d6 = ada_w.shape
    bsz = c.shape[0]
    tn = 1024
    return pl.pallas_call(
        _mod_kernel,
        grid=(depth, d6 // tn),
        in_specs=[pl.BlockSpec((bsz, d), lambda l, j: (0, 0)),
                  pl.BlockSpec((1, d, tn), lambda l, j: (l, 0, j)),
                  pl.BlockSpec((1, 1, tn), lambda l, j: (l, 0, j))],
        out_specs=pl.BlockSpec((1, bsz, tn), lambda l, j: (l, 0, j)),
        out_shape=jax.ShapeDtypeStruct((depth, bsz, d6), F32),
        compiler_params=_cparams(("arbitrary", "arbitrary")),
        name="adaln",
    )(c, ada_w, ada_b.reshape(depth, 1, d6))


def _proj_kernel(x_ref, g_ref, sh_ref, sc_ref, w_ref, ws_ref, o_ref, os_ref, *, small_transposed, tn):
    hb = (_rms(x_ref[...]) * g_ref[...] * (1.0 + sc_ref[0]) + sh_ref[0]).astype(BF16)
    small = _dot(hb, ws_ref[...])
    if small_transposed:
        os_ref[...] = small.T[0:os_ref.shape[0], :]
    else:
        os_ref[...] = small
    for j in range(w_ref.shape[1] // tn):
        o_ref[:, j * tn:(j + 1) * tn] = _dot(hb, w_ref[:, j * tn:(j + 1) * tn]).astype(o_ref.dtype)


def _project(x2, g, shift, scale, w, ws, small_rows, seq, tm, tn, name):
    n, d = x2.shape
    wn = w.shape[1]
    tiles_per_batch = seq // tm
    bsz = shift.shape[0]
    if small_rows:
        small_spec, small_shape = pl.BlockSpec((small_rows, tm), lambda i: (0, i)), (small_rows, n)
    else:
        small_spec, small_shape = pl.BlockSpec((tm, LANES), lambda i: (i, 0)), (n, LANES)
    resident = pl.Buffered(1)
    return pl.pallas_call(
        functools.partial(_proj_kernel, small_transposed=bool(small_rows), tn=tn),
        grid=(n // tm,),
        in_specs=[pl.BlockSpec((tm, d), lambda i: (i, 0)),
                  pl.BlockSpec((1, d), lambda i: (0, 0)),
                  pl.BlockSpec((1, 1, d), lambda i: (i // tiles_per_batch, 0, 0)),
                  pl.BlockSpec((1, 1, d), lambda i: (i // tiles_per_batch, 0, 0)),
                  pl.BlockSpec((d, wn), lambda i: (0, 0), pipeline_mode=resident),
                  pl.BlockSpec(ws.shape, lambda i: (0, 0), pipeline_mode=resident)],
        out_specs=[pl.BlockSpec((tm, wn), lambda i: (i, 0)), small_spec],
        out_shape=[jax.ShapeDtypeStruct((n, wn), BF16), jax.ShapeDtypeStruct(small_shape, F32)],
        compiler_params=_cparams(("arbitrary",)),
        name=name,
    )(x2, g.reshape(1, d), shift.reshape(bsz, 1, d), scale.reshape(bsz, 1, d), w, ws)


def _tail_kernel(ya_ref, yb_ref, wa_ref, wb_ref, x_ref, mod_ref, ng_ref, w1_ref, w2_ref, o_ref, *, tf):
    mixed = _dot(ya_ref[...], wa_ref[...]) + _dot(yb_ref[...], wb_ref[...])
    x = x_ref[...] + mod_ref[0, 0:1, :] * (_rms(mixed) * ng_ref[0:1, :])
    h = (_rms(x) * ng_ref[1:2, :] * (1.0 + mod_ref[0, 2:3, :]) + mod_ref[0, 1:2, :]).astype(BF16)
    u = None
    for f in range(w1_ref.shape[1] // tf):
        a = jnp.maximum(_dot(h, w1_ref[:, f * tf:(f + 1) * tf]), 0.0)
        part = _dot((a * a).astype(BF16), w2_ref[f * tf:(f + 1) * tf, :])
        u = part if u is None else u + part
    o_ref[...] = x + mod_ref[0, 3:4, :] * (_rms(u) * ng_ref[2:3, :])


def _tail(ya, ia, yb, ib, w_out, jo, x2, mods, gains, w1, w2, layer, seq, tm, tf, name):
    n, d = x2.shape
    kh = w_out.shape[1] // 2
    dff = w1.shape[2]
    tiles_per_batch = seq // tm
    resident = pl.Buffered(1)
    return pl.pallas_call(
        functools.partial(_tail_kernel, tf=tf),
        grid=(n // tm,),
        in_specs=[pl.BlockSpec((tm, kh), lambda i: (i, ia)),
                  pl.BlockSpec((tm, kh), lambda i: (i, ib)),
                  pl.BlockSpec((None, kh, d), lambda i: (jo, 0, 0), pipeline_mode=resident),
                  pl.BlockSpec((None, kh, d), lambda i: (jo, 1, 0), pipeline_mode=resident),
                  pl.BlockSpec((tm, d), lambda i: (i, 0)),
                  pl.BlockSpec((1, 4, d), lambda i: (i // tiles_per_batch, 0, 0)),
                  pl.BlockSpec((3, d), lambda i: (0, 0)),
                  pl.BlockSpec((None, d, dff), lambda i: (layer, 0, 0), pipeline_mode=resident),
                  pl.BlockSpec((None, dff, d), lambda i: (layer, 0, 0), pipeline_mode=resident)],
        out_specs=pl.BlockSpec((tm, d), lambda i: (i, 0)),
        out_shape=jax.ShapeDtypeStruct((n, d), F32),
        compiler_params=_cparams(("arbitrary",)),
        name=name,
    )(ya, yb, w_out, w_out, x2, mods, gains, w1, w2)


def _conv_kernel(main_ref, prev_ref, next_ref, w_ref, b_ref, o_ref, scr, *, blocks_per_seq):
    t = main_ref.shape[0]
    i = pl.program_id(0) % blocks_per_seq
    keep_prev = (i > 0).astype(F32)
    keep_next = (i < blocks_per_seq - 1).astype(F32)
    scr[0:HALO, :] = prev_ref[...].astype(F32) * keep_prev
    scr[HALO:HALO + t, :] = main_ref[...].astype(F32)
    scr[HALO + t:2 * HALO + t, :] = next_ref[...].astype(F32) * keep_next
    pad = SSD_CONV // 2
    acc = b_ref[...] + w_ref[0:1, :] * scr[HALO - pad:HALO - pad + t, :]
    for k in range(1, SSD_CONV):
        acc = acc + w_ref[k:k + 1, :] * scr[HALO - pad + k:HALO - pad + k + t, :]
    o_ref[...] = (acc * _sigmoid(acc)).astype(o_ref.dtype)


def _conv_silu(main, col0, conv_w, conv_b, seq, tr, tc):
    n = main.shape[0]
    ch = conv_w.shape[1]
    cb0 = col0 // tc
    rpb = tr // HALO
    last_halo = n // HALO - 1
    return pl.pallas_call(
        functools.partial(_conv_kernel, blocks_per_seq=seq // tr),
        grid=(n // tr, ch // tc),
        in_specs=[pl.BlockSpec((tr, tc), lambda i, j: (i, cb0 + j)),
                  pl.BlockSpec((HALO, tc), lambda i, j: (jnp.maximum(i * rpb - 1, 0), cb0 + j)),
                  pl.BlockSpec((HALO, tc), lambda i, j: (jnp.minimum((i + 1) * rpb, last_halo), cb0 + j)),
                  pl.BlockSpec((SSD_CONV, tc), lambda i, j: (0, j)),
                  pl.BlockSpec((1, tc), lambda i, j: (0, j))],
        out_specs=pl.BlockSpec((tr, tc), lambda i, j: (i, j)),
        out_shape=jax.ShapeDtypeStruct((n, ch), BF16),
        scratch_shapes=[pltpu.VMEM((tr + 2 * HALO, tc), F32)],
        compiler_params=_cparams(("arbitrary", "arbitrary")),
        name="conv_silu",
    )(main, main, main, conv_w, conv_b.reshape(1, ch))


def _ssd_kernel(xs_ref, bc_ref, dt_ref, dtb_ref, alog_ref, *rest, reverse):
    t = SSD_CHUNK

    @pl.when(pl.program_id(1) == 0)
    def _():
        rest[-1][...] = jnp.zeros_like(rest[-1])

    subs = range(xs_ref.shape[0] // t)
    for sub in (reversed(subs) if reverse else subs):
        rows = pl.ds(sub * t, t)
        per_row = [r.at[rows] for r in rest[:2]] + list(rest[2:4]) if reverse else []
        _ssd_chunk(xs_ref.at[rows], bc_ref.at[rows], dt_ref.at[rows], dtb_ref, alog_ref,
                   *per_row, rest[-2].at[rows], rest[-1], reverse=reverse)


def _ssd_chunk(xs_ref, bc_ref, dt_ref, dtb_ref, alog_ref, *rest, reverse):
    if reverse:
        z_ref, yf_ref, dskip_ref, ng_ref, o_ref, st_scr = rest
    else:
        o_ref, st_scr = rest
    t = SSD_CHUNK
    hd = SSD_HEAD_DIM
    hpg = SSD_HEADS // SSD_GROUPS

    xs = xs_ref[...].astype(F32)
    bc = bc_ref[...]

    dt = _softplus(dt_ref[...] + dtb_ref[...])
    adt = dt * (-jnp.exp(alog_ref[...]))
    row = _iota((t, t), 0)
    col = _iota((t, t), 1)
    tri = (col >= row) if reverse else (col <= row)
    cs = _dot_exact_lhs01(tri.astype(BF16), adt)
    cs_t = cs.T
    dt_t = dt.T
    base = SSD_HEADS if reverse else 0
    last = 0 if reverse else t - 1

    first_half = _iota((t, LANES), 1) < hd
    ys = []
    for g in range(SSD_GROUPS):
        bg = bc[:, g * SSD_STATE:(g + 1) * SSD_STATE]
        cg = bc[:, (SSD_GROUPS + g) * SSD_STATE:(SSD_GROUPS + g + 1) * SSD_STATE]
        cb = _dot_nt(cg, bg)
        bg_t = bg.astype(F32).T
        s_prev = st_scr[g]
        y_off = _dot(cg, s_prev.astype(BF16))
        s_new = []
        etot = []
        for pr in range(hpg // 2):
            pair = g * (hpg // 2) + pr
            xpair = xs[:, pair * LANES:(pair + 1) * LANES]
            cs_cols = []
            acc = None
            for sub in range(2):
                j = base + 2 * pair + sub
                cs_col = jnp.broadcast_to(cs[:, j:j + 1], (t, t))
                cs_cols.append(cs_col)
                dts = dt_t[j:j + 1, :]
                lmat = jnp.exp(jnp.where(tri, cs_col - cs_t[j:j + 1, :], NEG_BIG))
                m = (cb * lmat * dts).astype(BF16)
                keep = first_half if sub == 0 else jnp.logical_not(first_half)
                part = _dot(m, jnp.where(keep, xpair, 0.0).astype(BF16))
                acc = part if acc is None else acc + part
            cs_pair = jnp.where(first_half, cs_cols[0], cs_cols[1])
            tot_pair = cs_pair[last:last + 1, :]
            xd = (xpair * jnp.exp(tot_pair - cs_pair)).astype(BF16)
            sp = None
            for sub in range(2):
                j = base + 2 * pair + sub
                keep = first_half if sub == 0 else jnp.logical_not(first_half)
                part = _dot((bg_t * dt_t[j:j + 1, :]).astype(BF16), jnp.where(keep, xd, jnp.zeros((), BF16)))
                sp = part if sp is None else sp + part
            ys.append(acc + y_off[:, pr * LANES:(pr + 1) * LANES] * jnp.exp(cs_pair))
            s_new.append(sp)
            etot.append(jnp.exp(tot_pair))
        st_scr[g] = s_prev * jnp.concatenate(etot, axis=1) + jnp.concatenate(s_new, axis=1)
    y = jnp.concatenate(ys, axis=1)

    if reverse:
        y = y + yf_ref[...] + dskip_ref[...] * xs
        z = z_ref[...].astype(F32)
        y = y * (z * _sigmoid(z))
        o_ref[...] = (_rms(y) * ng_ref[...]).astype(o_ref.dtype)
    else:
        o_ref[...] = y


def _ssd_pass(conv, dtf, dtb, alog, bsz, seq, reverse, extra=None):
    n = conv.shape[0]
    t = SSD_BLOCK_CHUNKS * SSD_CHUNK
    nc = seq // t
    bc_blk = SSD_WIDTH // SSD_BC

    def rows(b, c):
        return b * nc + (nc - 1 - c if reverse else c)

    in_specs = [pl.BlockSpec((t, SSD_WIDTH), lambda b, c: (rows(b, c), 0)),
                pl.BlockSpec((t, SSD_BC), lambda b, c: (rows(b, c), bc_blk)),
                pl.BlockSpec((t, LANES), lambda b, c: (rows(b, c), 0)),
                pl.BlockSpec((1, LANES), lambda b, c: (0, 0)),
                pl.BlockSpec((1, LANES), lambda b, c: (0, 0))]
    args = [conv, conv, dtf, dtb, alog]
    if reverse:
        main, yf, dskip, ng = extra
        in_specs += [pl.BlockSpec((t, SSD_WIDTH), lambda b, c: (rows(b, c), 0)),
                     pl.BlockSpec((t, SSD_WIDTH), lambda b, c: (rows(b, c), 0)),
                     pl.BlockSpec((1, SSD_WIDTH), lambda b, c: (0, 0)),
                     pl.BlockSpec((1, SSD_WIDTH), lambda b, c: (0, 0))]
        args += [main, yf, dskip, ng]
        out_dtype = BF16
    else:
        out_dtype = F32
    return pl.pallas_call(
        functools.partial(_ssd_kernel, reverse=reverse),
        grid=(bsz, nc),
        in_specs=in_specs,
        out_specs=pl.BlockSpec((t, SSD_WIDTH), lambda b, c: (rows(b, c), 0)),
        out_shape=jax.ShapeDtypeStruct((n, SSD_WIDTH), out_dtype),
        scratch_shapes=[pltpu.VMEM((SSD_GROUPS, SSD_STATE, SSD_WIDTH // SSD_GROUPS), F32)],
        compiler_params=_cparams(("arbitrary", "arbitrary")),
        name="ssd_bwd" if reverse else "ssd_fwd",
    )(*args)


def _na_kernel(q_ref, k_ref, v_ref, bias_ref, o_ref, s_scr, p_scr, l_scr, *, nrows):
    w = GRID_W
    band = NA_KH * w
    first = _iota((w, LANES), 1) < NA_HEAD_DIM
    zero = jnp.zeros((), BF16)

    def band_start(r):
        return jnp.clip(r - NA_KH // 2, 0, nrows - NA_KH)

    def scores(r, slot):
        rs = band_start(r)
        q = q_ref[pl.ds(pl.multiple_of(r * w, w), w), :] * jnp.asarray(NA_HEAD_DIM ** -0.5, BF16)
        qs = jnp.concatenate([jnp.where(first, q, zero), jnp.where(first, zero, q)], axis=0)
        kb = k_ref[pl.ds(pl.multiple_of(rs * w, w), band), :]
        j0 = NA_KH - 1 - (r - rs)
        bias = jnp.concatenate([bias_ref[0, j0 + 2 * a] for a in range(NA_KH // 2)], axis=1)
        s_scr[slot] = _dot_nt(qs, kb) + bias

    def softmax(slot):
        s = s_scr[slot]
        p = jnp.exp(s - jnp.max(s, axis=-1, keepdims=True))
        l_scr[slot] = jnp.broadcast_to(jnp.sum(p, axis=-1, keepdims=True), (2 * w, LANES))
        p_scr[slot] = p.astype(BF16)

    def values(r, slot):
        rs = band_start(r)
        vb = v_ref[pl.ds(pl.multiple_of(rs * w, w), band), :]
        o = _dot(p_scr[slot], vb) / l_scr[slot]
        o_ref[pl.ds(pl.multiple_of(r * w, w), w), :] = jnp.where(first, o[0:w], o[w:2 * w]).astype(o_ref.dtype)

    s_scr[...] = jnp.zeros_like(s_scr)
    p_scr[...] = jnp.zeros_like(p_scr)
    l_scr[...] = jnp.ones_like(l_scr)

    def body(i4, carry):
        for sub in range(4):
            i = 4 * i4 + sub
            slot = sub % 2
            values(jnp.clip(i - 2, 0, nrows - 1), slot)
            softmax(1 - slot)
            scores(jnp.minimum(i, nrows - 1), slot)
        return carry

    lax.fori_loop(0, (nrows + 2 + 3) // 4, body, 0)


def _na_bias(rpb):
    w = GRID_W
    h, nro, nrel = rpb.shape
    c = np.arange(w)
    kc = np.arange(w)
    col_start = np.clip(c - NA_KW // 2, 0, w - NA_KW)
    valid = (kc[None, :] >= col_start[:, None]) & (kc[None, :] < col_start[:, None] + NA_KW)
    co = kc[None, :] - c[:, None] + NA_KW - 1
    onehot = ((np.arange(nrel)[:, None, None] == co[None]) & valid[None]).astype(np.float32)
    sel = np.zeros((2, nrel, w, 2, w), np.float32)
    sel[0, :, :, 0, :] = onehot
    sel[1, :, :, 1, :] = onehot
    mask = np.broadcast_to(np.where(valid, 0.0, NEG_BIG)[:, None, :], (w, 2, w)).astype(np.float32)
    table = np.concatenate([sel.reshape(2 * nrel, 2 * w * w), mask.reshape(1, 2 * w * w)], axis=0)
    r = rpb.astype(F32).reshape(h // 2, 2, nro, nrel)
    rows = jnp.stack([r[:, :, :nro - 1], r[:, :, 1:]], axis=3).transpose(0, 2, 1, 3, 4)
    rows = rows.reshape(h * (nro - 1), 2 * nrel)
    rows = jnp.concatenate([rows, jnp.ones((rows.shape[0], 1), F32)], axis=1)
    tiles = jnp.dot(rows, table, precision=lax.Precision.HIGHEST)
    return tiles.reshape(h // 2, nro - 1, 2 * w, 2 * w)


def _na(main, bias, bsz, seq, q_blk):
    n = main.shape[0]
    nrows = seq // GRID_W
    assert nrows >= NA_KH
    pairs = NA_HEADS // 2
    return pl.pallas_call(
        functools.partial(_na_kernel, nrows=nrows),
        grid=(bsz, pairs),
        in_specs=[pl.BlockSpec((seq, LANES), lambda b, p: (b, q_blk + p)),
                  pl.BlockSpec((seq, LANES), lambda b, p: (b, q_blk + pairs + p)),
                  pl.BlockSpec((seq, LANES), lambda b, p: (b, q_blk + 2 * pairs + p)),
                  pl.BlockSpec((1,) + bias.shape[1:], lambda b, p: (p, 0, 0, 0))],
        out_specs=pl.BlockSpec((seq, LANES), lambda b, p: (b, p)),
        out_shape=jax.ShapeDtypeStruct((n, NA_WIDTH), BF16),
        scratch_shapes=[pltpu.VMEM((2, 2 * GRID_W, NA_KH * GRID_W), F32),
                        pltpu.VMEM((2, 2 * GRID_W, NA_KH * GRID_W), BF16),
                        pltpu.VMEM((2, 2 * GRID_W, LANES), F32)],
        compiler_params=_cparams(("arbitrary", "arbitrary")),
        name="nbr_attn",
    )(main, main, main, bias)


def _mlstm_kernel(q_ref, k_ref, v_ref, og_ref, gt_ref, gbt_ref, hg_ref, o_ref,
                  rall_scr, ball_scr, a_scr, w_scr, sc_scr, cp_scr, np_scr, c_st, n_st, p_scr, psum_scr,
                  *, nchunks):
    t = ML_CHUNK
    nc = nchunks
    head = pl.program_id(1)
    scale = ML_DK ** -0.5
    row = _iota((t, t), 0)
    col = _iota((t, t), 1)
    tris = (col <= row, col >= row)

    @pl.when((pl.program_id(0) == 0) & (head == 0))
    def _():
        rall_scr[...] = jnp.zeros_like(rall_scr)
        ball_scr[...] = jnp.zeros_like(ball_scr)
        a_scr[...] = jnp.zeros_like(a_scr)
        for d in range(2):
            rall_scr[ML_TAB * d + 9:ML_TAB * d + 12, :] = jnp.ones((3, nc * t), F32)
            ball_scr[d, ML_TAB * d:ML_TAB * d + 3, :] = jnp.full((3, nc * t), -1.0, F32)

    lane = _iota((nc, t), 1)
    subc = _iota((nc, LANES), 0)
    sub32 = _iota((4 * ML_HEADS, 1), 0)

    def head_rows(kind):
        idx = kind * ML_HEADS + head
        bias = jnp.sum(jnp.where(sub32 == idx, gbt_ref[...], 0.0), axis=0, keepdims=True)
        return gt_ref[0, idx] + bias

    def scan_lanes(x, reverse, op, fill):
        sh = 1
        while sh < t:
            if reverse:
                x = op(x, jnp.where(lane < t - sh, pltpu.roll(x, t - sh, axis=1), fill))
            else:
                x = op(x, jnp.where(lane >= sh, pltpu.roll(x, sh, axis=1), fill))
            sh *= 2
        return x

    for d in range(2):
        ig = head_rows(2 * d)
        b = scan_lanes(_log_sigmoid(head_rows(2 * d + 1)), bool(d), jnp.add, 0.0)
        g_col = b[:, 0:1] if d else b[:, t - 1:t]
        a = g_col - b + ig
        m_col = jnp.max(a, axis=-1, keepdims=True)
        w = jnp.exp(a - m_col)
        rs = ig - b
        run_max = scan_lanes(rs, bool(d), jnp.maximum, NEG_BIG)
        g128 = jnp.broadcast_to(g_col, (nc, LANES))
        l128 = jnp.broadcast_to(m_col, (nc, LANES))
        m = jnp.zeros((1, LANES), F32)
        m_prev = jnp.zeros((nc, LANES), F32)
        s_old = jnp.zeros((nc, LANES), F32)
        s_new = jnp.zeros((nc, LANES), F32)
        for step in range(nc):
            cc = nc - 1 - step if d else step
            g_c = g128[cc:cc + 1, :]
            l_c = l128[cc:cc + 1, :]
            m_next = jnp.maximum(g_c + m, l_c)
            pick = subc == cc
            m_prev = jnp.where(pick, m, m_prev)
            s_old = jnp.where(pick, jnp.exp(g_c + m - m_next), s_old)
            s_new = jnp.where(pick, jnp.exp(l_c - m_next), s_new)
            m = m_next
        mp = m_prev[:, 0:1]
        u = jnp.maximum(mp, run_max)
        inter = jnp.exp(mp - u) * scale
        floor = jnp.exp(-(b + u))
        base = ML_TAB * d
        a_rows = [p.astype(F32) for x in (u, inter, floor) for p in _split3(x)]
        b_rows = [p.astype(F32) for p in _split3(rs)]
        for cc in range(nc):
            ls = slice(cc * t, (cc + 1) * t)
            for j, x in enumerate(a_rows):
                rall_scr[base + j:base + j + 1, ls] = x[cc:cc + 1, :]
            for j, x in enumerate(b_rows):
                ball_scr[d, base + 9 + j:base + 10 + j, ls] = x[cc:cc + 1, :]
            w_scr[d, :, ls] = jnp.broadcast_to(w[cc:cc + 1, :], (SUBLANES, t))
            sc_scr[d, cc, 0] = jnp.broadcast_to(s_old[cc:cc + 1, :], (SUBLANES, LANES))
            sc_scr[d, cc, 1] = jnp.broadcast_to(s_new[cc:cc + 1, :], (SUBLANES, LANES))

    def build_cols(c, carry):
        r0 = pl.multiple_of(c * t, t)
        cols = rall_scr[:, pl.ds(r0, t)].T
        a_scr[pl.ds(r0, t), 0:2 * ML_TAB] = cols.astype(BF16)
        return carry

    lax.fori_loop(0, nc, build_cols, 0)

    c_st[...] = jnp.zeros_like(c_st)
    n_st[...] = jnp.zeros_like(n_st)

    def scan_step(i, carry):
        for d in range(2):
            c = nc - 1 - i if d else i
            r0 = pl.multiple_of(c * t, t)
            k = k_ref[pl.ds(r0, t), :]
            w8 = w_scr[d, :, pl.ds(r0, t)]
            kw_t = k.astype(F32).T * w8[0:1, :]
            s_loc = _dot(kw_t.astype(BF16), v_ref[pl.ds(r0, t), :])
            n_loc = _dot(w8.astype(BF16), k)
            c_prev = c_st[d]
            n_prev = n_st[d]
            cp_scr[d, c] = c_prev.astype(BF16)
            np_scr[d, c] = n_prev
            s_old = sc_scr[d, c, 0][0:1, :]
            s_new = sc_scr[d, c, 1][0:1, :]
            c_st[d] = (jnp.concatenate([s_old, s_old], axis=1) * c_prev
                       + jnp.concatenate([s_new, s_new], axis=1) * s_loc)
            n_st[d] = s_old * n_prev + s_new * n_loc
        return carry

    lax.fori_loop(0, nc, scan_step, 0, unroll=min(4, nc))

    kk = _iota((LANES, 2 * LANES), 0)
    ll = _iota((LANES, 2 * LANES), 1)
    pad_rows = jnp.zeros((LANES - 2 * ML_TAB, t), BF16)

    def scores(c, slot):
        r0 = pl.multiple_of(c * t, t)
        qk = _dot_nt(q_ref[pl.ds(r0, t), :], k_ref[pl.ds(r0, t), :]) * scale
        cols = a_scr[pl.ds(r0, t), :]
        for d in range(2):
            b_arg = jnp.concatenate([ball_scr[d, :, pl.ds(r0, t)].astype(BF16), pad_rows], axis=0)
            arg = _dot(cols, b_arg)
            sc = qk * jnp.exp(jnp.where(tris[d], arg, NEG_BIG))
            psum_scr[slot, d] = jnp.broadcast_to(jnp.sum(sc, axis=-1, keepdims=True), (t, LANES))
            p_scr[slot, d] = sc.astype(BF16)

    def values(c, slot):
        r0 = pl.multiple_of(c * t, t)
        q = q_ref[pl.ds(r0, t), :]
        qf = q.astype(F32)
        v = v_ref[pl.ds(r0, t), :]
        cols = a_scr[pl.ds(r0, t), :]
        hsum = None
        for d in range(2):
            base = ML_TAB * d
            pick = (((ll < LANES) & (kk >= base + 3) & (kk < base + 6))
                    | ((ll >= LANES) & (kk >= base + 6) & (kk < base + 9))).astype(BF16)
            rep = _dot(cols, pick)
            inter = rep[:, 0:LANES]
            q_n = jnp.sum(qf * np_scr[d, c][0:1, :], axis=-1, keepdims=True)
            den = psum_scr[slot, d] + inter * q_n
            r = 1.0 / jnp.maximum(jnp.abs(den), rep[:, LANES:2 * LANES])
            num = _dot(p_scr[slot, d], v) + jnp.concatenate([inter, inter], axis=1) * _dot(q, cp_scr[d, c])
            hd = num * jnp.concatenate([r, r], axis=1)
            hsum = hd if hsum is None else hsum + hd
        og = og_ref[pl.ds(r0, t), :].astype(F32)
        o_ref[pl.ds(r0, t), :] = (_sigmoid(og) * (_rms(hsum) * hg_ref[0])).astype(o_ref.dtype)

    p_scr[...] = jnp.zeros_like(p_scr)
    psum_scr[...] = jnp.zeros_like(psum_scr)

    def out_step(i2, carry):
        for slot in range(2):
            i = 2 * i2 + slot
            values(jnp.clip(i - 1, 0, nc - 1), 1 - slot)
            scores(jnp.minimum(i, nc - 1), slot)
        return carry

    lax.fori_loop(0, nc // 2 + 1, out_step, 0)


def _mlstm(main, gates_t, gate_bt, head_g, bsz, seq):
    n = main.shape[0]
    t = ML_CHUNK
    nc = seq // t
    nh = ML_HEADS
    r = gates_t.shape[0]
    gates_c = gates_t.reshape(r, bsz, nc, t).transpose(1, 0, 2, 3)
    return pl.pallas_call(
        functools.partial(_mlstm_kernel, nchunks=nc),
        grid=(bsz, nh),
        in_specs=[pl.BlockSpec((seq, ML_DK), lambda b, h: (b, h)),
                  pl.BlockSpec((seq, ML_DK), lambda b, h: (b, nh + h)),
                  pl.BlockSpec((seq, ML_DV), lambda b, h: (b, nh + h)),
                  pl.BlockSpec((seq, ML_DV), lambda b, h: (b, 2 * nh + h)),
                  pl.BlockSpec((1, r, nc, t), lambda b, h: (b, 0, 0, 0)),
                  pl.BlockSpec((r, 1), lambda b, h: (0, 0)),
                  pl.BlockSpec((1, 1, ML_DV), lambda b, h: (h, 0, 0))],
        out_specs=pl.BlockSpec((seq, ML_DV), lambda b, h: (b, h)),
        out_shape=jax.ShapeDtypeStruct((n, ML_WIDTH), BF16),
        scratch_shapes=[pltpu.VMEM((2 * ML_TAB, seq), F32),
                        pltpu.VMEM((2, 2 * ML_TAB, seq), F32),
                        pltpu.VMEM((seq, LANES), BF16),
                        pltpu.VMEM((2, SUBLANES, seq), F32),
                        pltpu.VMEM((2, nc, 2, SUBLANES, LANES), F32),
                        pltpu.VMEM((2, nc, ML_DK, ML_DV), BF16),
                        pltpu.VMEM((2, nc, SUBLANES, ML_DK), F32),
                        pltpu.VMEM((2, ML_DK, ML_DV), F32),
                        pltpu.VMEM((2, SUBLANES, ML_DK), F32),
                        pltpu.VMEM((2, 2, t, t), BF16),
                        pltpu.VMEM((2, 2, t, LANES), F32)],
        compiler_params=_cparams(("arbitrary", "arbitrary")),
        name="mlstm",
    )(main, main, main, main, gates_c, gate_bt, head_g.reshape(nh, 1, ML_DV))


def _pad_cols(a, width):
    return jnp.pad(a, ((0, 0), (0, width - a.shape[1])))


def _ssd_na_layer(x2, mods, norm_g, w_in, conv_w, conv_b, dt_bias, a_log, d_skip, ssd_norm, rpb, bsz, seq):
    sh1, sc1 = mods
    s1 = SSD_WIDTH
    s2 = s1 + SSD_CONV_CH
    s3 = s2 + 2 * SSD_HEADS
    w_main = jnp.concatenate([w_in[:, :s2], w_in[:, s3:]], axis=1).astype(BF16)
    w_dt = _pad_cols(w_in[:, s2:s3], LANES).astype(BF16)
    main, dtf = _project(x2, norm_g[0], sh1, sc1, w_main, w_dt, 0, seq, tm=min(512, seq), tn=1408,
                         name="proj_ssd_na")

    conv = _conv_silu(main, s1, conv_w, conv_b, seq, tr=min(512, seq), tc=512)
    dtb = _pad_cols(dt_bias.reshape(1, 2 * SSD_HEADS), LANES)
    alog = _pad_cols(a_log.reshape(1, 2 * SSD_HEADS), LANES)
    dskip = jnp.repeat(d_skip, SSD_HEAD_DIM)[None, :]
    y_f = _ssd_pass(conv, dtf, dtb, alog, bsz, seq, reverse=False)
    y_ssd = _ssd_pass(conv, dtf, dtb, alog, bsz, seq, reverse=True,
                      extra=(main, y_f, dskip, ssd_norm[None, :]))

    q_blk = (s1 + SSD_CONV_CH) // LANES
    y_na = _na(main, _na_bias(rpb), bsz, seq, q_blk)
    return y_ssd, 0, y_na, 0


def _mlstm_layer(x2, mods, norm_g, w_in, gate_b, head_g, bsz, seq):
    sh1, sc1 = mods
    wm = 2 * ML_QK + 2 * ML_WIDTH
    w_main = w_in[:, :wm].astype(BF16)
    main, gates_t = _project(x2, norm_g[0], sh1, sc1, w_main, _pad_cols(w_in[:, wm:], LANES).astype(BF16),
                             4 * ML_HEADS, seq, tm=min(512, seq), tn=1536, name="proj_mlstm")
    y = _mlstm(main, gates_t, gate_b.reshape(4 * ML_HEADS, 1).astype(F32), head_g, bsz, seq)
    return y, 0, y, 1


def kernel(x, c, ada_w, ada_b, norm_g, mlp_w1, mlp_w2, ab_w_in, ab_conv_w, ab_conv_b, ab_dt_bias, ab_a_log,
           ab_d_skip, ab_ssd_norm, ab_rpb, ab_w_out, ml_w_in, ml_gate_b, ml_head_norm, ml_w_out):
    bsz, seq, d = x.shape
    depth = ada_w.shape[0]
    mod = _adaln(c, ada_w, ada_b)
    x2 = x.reshape(bsz * seq, d)
    w1_all, w2_all = mlp_w1.astype(BF16), mlp_w2.astype(BF16)
    w_outs = (ab_w_out.astype(BF16), ml_w_out.astype(BF16))
    for layer in range(depth):
        sh1, sc1, g1, sh2, sc2, g2 = [mod[layer, :, i * d:(i + 1) * d] for i in range(6)]
        j = layer // 2
        if layer % 2 == 0:
            mixed = _ssd_na_layer(x2, (sh1, sc1), norm_g[layer], ab_w_in[j], ab_conv_w[j], ab_conv_b[j],
                                  ab_dt_bias[j], ab_a_log[j], ab_d_skip[j], ab_ssd_norm[j], ab_rpb[j], bsz, seq)
        else:
            mixed = _mlstm_layer(x2, (sh1, sc1), norm_g[layer], ml_w_in[j], ml_gate_b[j], ml_head_norm[j],
                                 bsz, seq)
        x2 = _tail(*mixed, w_outs[layer % 2], j, x2, jnp.stack([g1, sh2, sc2, g2], axis=1), norm_g[layer, 1:4],
                   w1_all, w2_all, layer, seq, tm=min(512, seq), tf=1024, name="tail%d" % layer)
    return x2.reshape(bsz, seq, d)
```
